```python
import math
import jax
import jax.numpy as jnp
from jax import lax
import numpy as np

D_MODEL = 2048
BATCH = 1
SEQ = 8192
DEPTH = 2
DEC_BATCH = 128
DEC_SEQ = 1
PAST_LEN = 2048
PAGE_SIZE = 128

H_A = 8
KVH_A = 4
HD_A = 128
GROUP_A = H_A // KVH_A
MOBA_BLOCK = 256
MOBA_TOPK = 3
Q_BLOCK = 128
H_B = 8
DK_B = 128
DV_B = 128
CONV_W = 4
DN_CHUNK = 64
W_C = 1024
S5_GROUP_CH = 16
S5_GROUPS = W_C // S5_GROUP_CH
S5_STATE = 64
PLE_DIM = 256
N_BRANCH = 3
BR_W = 1024
Q_A = H_A * HD_A
KV_A = KVH_A * HD_A
V_B = H_B * DV_B
CONV_CH = 2 * H_B * DK_B + V_B
SPLIT_SIZES = (Q_A, KV_A, KV_A, Q_A, CONV_CH, H_B, H_B, V_B, W_C, W_C, N_BRANCH * D_MODEL)
IN_COLS = sum(SPLIT_SIZES)
EPS = 1e-6

kernel_name = "hybrid_moba_gdn_s5_decode_step"


def rmsnorm(x, g):
    xf = x.astype(jnp.float32)
    y = xf * lax.rsqrt(jnp.mean(xf * xf, axis=-1, keepdims=True) + EPS)
    return (y * g.astype(jnp.float32)).astype(x.dtype)


def l2norm(x):
    return x * lax.rsqrt(jnp.sum(x * x, axis=-1, keepdims=True) + EPS)


def alibi_slopes():
    return jnp.exp2(-8.0 * jnp.arange(1, H_A + 1, dtype=jnp.float32) / H_A)


def moba_attention(q, k_all, v_all, q_start):
    bsz, lq = q.shape[0], q.shape[1]
    lk = k_all.shape[1]
    nb = -(-lk // MOBA_BLOCK)
    pad = nb * MOBA_BLOCK - lk

    def blocks(a):
        a = jnp.pad(a, ((0, 0), (0, pad), (0, 0), (0, 0)))
        return a.reshape(bsz, nb, MOBA_BLOCK, KVH_A, HD_A).transpose(0, 3, 1, 2, 4)

    kb, vb = blocks(k_all), blocks(v_all)
    k_mean = jnp.repeat(jnp.mean(kb.astype(jnp.float32), axis=3), GROUP_A, axis=1)
    n_sel = min(MOBA_TOPK, nb)
    slopes = alibi_slopes()
    qb = math.gcd(lq, Q_BLOCK)
    nq = lq // qb
    q_blk = jnp.swapaxes(q.reshape(bsz, nq, qb, H_A, HD_A), 0, 1)
    pos = (q_start + jnp.arange(lq, dtype=jnp.int32)).reshape(nq, qb)
    b_ix = jnp.arange(bsz)[:, None, None, None]
    h_ix = (jnp.arange(H_A) // GROUP_A)[None, None, :, None]
    offs = jnp.arange(MOBA_BLOCK, dtype=jnp.int32)

    def one_block(args):
        qc, t = args
        bt = t // MOBA_BLOCK
        score = jnp.einsum("bqhd,bhnd->bqhn", qc.astype(jnp.float32), k_mean)
        past = jnp.arange(nb)[None, :] < bt[:, None]
        score = jnp.where(past[None, :, None, :], score, -jnp.inf)
        _, sel = lax.top_k(score, n_sel)
        own = jnp.broadcast_to(bt[None, :, None, None], (bsz, qb, H_A, 1)).astype(sel.dtype)
        idx = jnp.concatenate([sel, own], axis=-1)
        blk_ok = jnp.concatenate([sel < bt[None, :, None, None], jnp.ones(own.shape, dtype=bool)], axis=-1)
        kg = kb[b_ix, h_ix, idx]
        vg = vb[b_ix, h_ix, idx]
        dist = t[None, :, None, None, None] - (idx[..., None] * MOBA_BLOCK + offs)
        mask = blk_ok[..., None] & (dist >= 0)
        logits = (jnp.einsum("bqhd,bqhnsd->bqhns", qc, kg).astype(jnp.float32) * HD_A ** -0.5
                  - slopes[None, None, :, None, None] * dist.astype(jnp.float32))
        logits = jnp.where(mask, logits, -jnp.inf)
        p = jax.nn.softmax(logits.reshape(bsz, qb, H_A, -1), axis=-1).reshape(logits.shape)
        return jnp.einsum("bqhns,bqhnsd->bqhd", p.astype(vg.dtype), vg)

    out = lax.map(one_block, (q_blk, pos))
    return jnp.swapaxes(out, 0, 1).reshape(bsz, lq, Q_A)


def causal_conv(xin, buf, w):
    L = xin.shape[1]
    xp = jnp.concatenate([buf.astype(xin.dtype), xin], axis=1)
    y = sum(xp[:, j:j + L] * w[j] for j in range(CONV_W))
    return y, xp[:, L:]


def gated_delta_rule(q, k, v, beta, g, s0):
    bsz, L, H, _ = q.shape
    dv = v.shape[-1]
    c = min(DN_CHUNK, L)
    n = -(-L // c)
    pad = n * c - L

    def chunks(a):
        a = jnp.pad(a, [(0, 0), (0, pad)] + [(0, 0)] * (a.ndim - 2))
        a = a.reshape((bsz, n, c) + a.shape[2:])
        return jnp.moveaxis(a, 3, 1)

    qc, kc, vc, bc, gc = (chunks(a) for a in (q, k, v, beta, g))
    gam = jnp.cumsum(gc, axis=-1)
    tri_incl = jnp.tril(jnp.ones((c, c), dtype=bool))
    tri_strict = jnp.tril(jnp.ones((c, c), dtype=bool), -1)
    decay = jnp.exp(jnp.where(tri_incl, gam[..., :, None] - gam[..., None, :], -jnp.inf))
    kk = jnp.einsum("bhnid,bhnjd->bhnij", kc, kc)
    lhs = jnp.where(tri_strict, bc[..., :, None] * kk * decay, 0.0) + jnp.eye(c, dtype=jnp.float32)
    rhs = jnp.concatenate([vc * bc[..., None], kc * (bc * jnp.exp(gam))[..., None]], axis=-1)
    sol = lax.linalg.triangular_solve(lhs, rhs, left_side=True, lower=True, unit_diagonal=True)
    u_val, k_cum = sol[..., :dv], sol[..., dv:]
    qk = jnp.einsum("bhnid,bhnjd->bhnij", qc, kc) * decay
    g_last = gam[..., -1]
    k_end = kc * jnp.exp(g_last[..., None] - gam)[..., None]
    q_dec = qc * jnp.exp(gam)[..., None]
    xs = tuple(jnp.moveaxis(a, 2, 0) for a in (u_val, k_cum, qk, k_end, q_dec, g_last))

    def step(S, inp):
        uv, kcm, qkm, ke, qd, gl = inp
        v_new = uv - jnp.einsum("bhcd,bhde->bhce", kcm, S)
        o = jnp.einsum("bhcd,bhde->bhce", qd, S) + jnp.einsum("bhij,bhje->bhie", qkm, v_new)
        S = S * jnp.exp(gl)[..., None, None] + jnp.einsum("bhcd,bhce->bhde", ke, v_new)
        return S, o

    s_fin, o = lax.scan(step, s0, xs)
    o = jnp.moveaxis(o, 0, 2).reshape(bsz, H, n * c, dv)[:, :, :L]
    return jnp.swapaxes(o, 1, 2), s_fin


def complex_affine_combine(e1, e2):
    a1r, a1i, b1r, b1i = e1
    a2r, a2i, b2r, b2i = e2
    return (a2r * a1r - a2i * a1i, a2r * a1i + a2i * a1r,
            a2r * b1r - a2i * b1i + b2r, a2r * b1i + a2i * b1r + b2i)


def s5_branch(u, h0_re, h0_im, lw):
    bsz, L, _ = u.shape
    f32 = jnp.float32
    uf = u.astype(f32)
    lr = lw["s5_lambda_re"].astype(f32)
    li = lw["s5_lambda_im"].astype(f32)
    dt = jnp.exp(lw["s5_log_dt"].astype(f32))[:, None]
    mag = jnp.exp(lr * dt)
    ang = li * dt
    a_re, a_im = mag * jnp.cos(ang), mag * jnp.sin(ang)
    den = lr * lr + li * li
    f_re = ((a_re - 1.0) * lr + a_im * li) / den
    f_im = (a_im * lr - (a_re - 1.0) * li) / den
    b_re, b_im = lw["s5_b_re"].astype(f32), lw["s5_b_im"].astype(f32)
    bb_re = f_re[..., None] * b_re - f_im[..., None] * b_im
    bb_im = f_re[..., None] * b_im + f_im[..., None] * b_re
    ug = uf.reshape(bsz, L, S5_GROUPS, S5_GROUP_CH)
    bu_re = jnp.einsum("gpc,blgc->blgp", bb_re, ug)
    bu_im = jnp.einsum("gpc,blgc->blgp", bb_im, ug)
    elems = (jnp.broadcast_to(a_re, bu_re.shape), jnp.broadcast_to(a_im, bu_re.shape), bu_re, bu_im)
    cum_re, cum_im, s_re, s_im = lax.associative_scan(complex_affine_combine, elems, axis=1)
    h0r = h0_re.astype(f32)[:, None]
    h0i = h0_im.astype(f32)[:, None]
    hr = s_re + cum_re * h0r - cum_im * h0i
    hi = s_im + cum_re * h0i + cum_im * h0r
    y = (jnp.einsum("gcp,blgp->blgc", lw["s5_c_re"].astype(f32), hr)
         - jnp.einsum("gcp,blgp->blgc", lw["s5_c_im"].astype(f32), hi))
    y = y.reshape(bsz, L, W_C) + lw["s5_d"].astype(f32) * uf
    y = jax.nn.gelu(y, approximate=False)
    y = y * jax.nn.sigmoid(y @ lw["s5_w_glu"].astype(f32) + lw["s5_b_glu"].astype(f32))
    return y, hr[:, -1], hi[:, -1]


def mixer_layer(x, pe, q_start, k_past, v_past, conv_buf, s_delta, h_re, h_im, lw):
    f32 = jnp.float32
    bsz, L, _ = x.shape
    h = rmsnorm(x, lw["norm_g"])
    proj = h @ lw["w_in"]
    qa, ka, va, za, qkv_b, a_b, b_b, z_b, u_c, z_c, gates = jnp.split(
        proj, np.cumsum(SPLIT_SIZES)[:-1].tolist(), axis=-1)
    ka = ka.reshape(bsz, L, KVH_A, HD_A)
    va = va.reshape(bsz, L, KVH_A, HD_A)
    k_all = ka if k_past is None else jnp.concatenate([k_past.astype(ka.dtype), ka], axis=1)
    v_all = va if v_past is None else jnp.concatenate([v_past.astype(va.dtype), va], axis=1)
    out_a = moba_attention(qa.reshape(bsz, L, H_A, HD_A), k_all, v_all, q_start) * jax.nn.silu(za)
    c_out, conv_new = causal_conv(qkv_b, conv_buf, lw["dn_conv_w"])
    c_out = jax.nn.silu(c_out.astype(f32))
    q_b, k_b, v_b = jnp.split(c_out, [H_B * DK_B, 2 * H_B * DK_B], axis=-1)
    q_b = l2norm(q_b.reshape(bsz, L, H_B, DK_B)) * DK_B ** -0.5
    k_b = l2norm(k_b.reshape(bsz, L, H_B, DK_B))
    v_b = v_b.reshape(bsz, L, H_B, DV_B)
    beta = jax.nn.sigmoid(b_b.astype(f32))
    g = -jnp.exp(lw["dn_a_log"].astype(f32)) * jax.nn.softplus(a_b.astype(f32) + lw["dn_dt_bias"].astype(f32))
    o_b, s_new = gated_delta_rule(q_b, k_b, v_b, beta, g, s_delta.astype(f32))
    out_b = rmsnorm(o_b, lw["dn_norm_g"]).reshape(bsz, L, V_B) * jax.nn.silu(z_b.astype(f32))
    y_c, hr, hi = s5_branch(u_c, h_re, h_im, lw)
    out_c = y_c * jax.nn.silu(z_c.astype(f32))
    g_a, g_b, g_c = jnp.split(jax.nn.sigmoid(gates.astype(f32)), N_BRANCH, axis=-1)
    wb = lw["w_branch"]
    merged = (g_a * (out_a @ wb[0]) + g_b * (out_b.astype(x.dtype) @ wb[1])
              + g_c * (out_c.astype(x.dtype) @ wb[2]))
    x = x + merged.astype(x.dtype) @ lw["w_out"]
    ple = (pe @ lw["ple_w"]) * jax.nn.sigmoid(rmsnorm(x, lw["ple_norm_g"]) @ lw["ple_w_gate"])
    x = x + ple.astype(x.dtype)
    return x, (ka, va, conv_new, s_new, hr, hi)


def setup_inputs(seed: int = 0) -> dict:
    key = jax.random.key(seed)
    keys = iter(jax.random.split(key, 40))
    f32 = jnp.float32

    def nrm(shape, scale):
        return jax.random.normal(next(keys), shape, f32) * scale

    def unif(shape, lo, hi):
        return jax.random.uniform(next(keys), shape, f32, lo, hi)

    n_pages = PAST_LEN // PAGE_SIZE
    n_used = DEC_BATCH * n_pages
    n_phys = (n_used * 5 + 3) // 4
    x_prompt = nrm((BATCH, SEQ, D_MODEL), 1.0)
    x_sample = nrm((DEC_BATCH, DEC_SEQ, D_MODEL), 1.0)
    cache_k = nrm((DEPTH, n_phys, PAGE_SIZE, KVH_A, HD_A), 1.0)
    cache_v = nrm((DEPTH, n_phys, PAGE_SIZE, KVH_A, HD_A), 1.0)
    page_table = jax.random.permutation(next(keys), n_phys)[:n_used].reshape(DEC_BATCH, n_pages).astype(jnp.int32)
    state_conv = nrm((DEPTH, DEC_BATCH, CONV_W - 1, CONV_CH), 1.0)
    state_delta = nrm((DEPTH, DEC_BATCH, H_B, DK_B, DV_B), 0.1)
    state_s5_re = nrm((DEPTH, DEC_BATCH, S5_GROUPS, S5_STATE), 0.3)
    state_s5_im = nrm((DEPTH, DEC_BATCH, S5_GROUPS, S5_STATE), 0.3)
    p_prompt = nrm((DEPTH, BATCH, SEQ, PLE_DIM), 1.0)
    p_sample = nrm((DEPTH, DEC_BATCH, DEC_SEQ, PLE_DIM), 1.0)
    norm_g = 1.0 + nrm((DEPTH, D_MODEL), 0.01)
    w_in = nrm((DEPTH, D_MODEL, IN_COLS), D_MODEL ** -0.5)
    dn_conv_w = nrm((DEPTH, CONV_W, CONV_CH), 0.5)
    dn_a_log = jnp.log(unif((DEPTH, H_B), 1.0, 16.0))
    dt0 = jnp.exp(unif((DEPTH, H_B), math.log(1e-3), math.log(1e-1)))
    dn_dt_bias = dt0 + jnp.log(-jnp.expm1(-dt0))
    dn_norm_g = 1.0 + nrm((DEPTH, DV_B), 0.01)
    n_idx = jnp.arange(S5_STATE, dtype=f32)
    s5_lambda_re = -0.5 + nrm((DEPTH, S5_GROUPS, S5_STATE), 0.01)
    s5_lambda_im = math.pi * n_idx + nrm((DEPTH, S5_GROUPS, S5_STATE), 0.01)
    s5_b_re = nrm((DEPTH, S5_GROUPS, S5_STATE, S5_GROUP_CH), (2 * S5_GROUP_CH) ** -0.5)
    s5_b_im = nrm((DEPTH, S5_GROUPS, S5_STATE, S5_GROUP_CH), (2 * S5_GROUP_CH) ** -0.5)
    s5_c_re = nrm((DEPTH, S5_GROUPS, S5_GROUP_CH, S5_STATE), S5_STATE ** -0.5)
    s5_c_im = nrm((DEPTH, S5_GROUPS, S5_GROUP_CH, S5_STATE), S5_STATE ** -0.5)
    s5_d = nrm((DEPTH, W_C), 0.5)
    s5_log_dt = unif((DEPTH, S5_GROUPS), math.log(1e-3), math.log(1e-1))
    s5_w_glu = nrm((DEPTH, W_C, W_C), W_C ** -0.5)
    s5_b_glu = nrm((DEPTH, W_C), 0.01)
    w_branch = nrm((DEPTH, N_BRANCH, BR_W, D_MODEL), BR_W ** -0.5)
    w_out = nrm((DEPTH, D_MODEL, D_MODEL), D_MODEL ** -0.5)
    ple_w = nrm((DEPTH, PLE_DIM, D_MODEL), PLE_DIM ** -0.5)
    ple_norm_g = 1.0 + nrm((DEPTH, D_MODEL), 0.01)
    ple_w_gate = nrm((DEPTH, D_MODEL, D_MODEL), D_MODEL ** -0.5)
    final_norm_g = 1.0 + nrm((D_MODEL,), 0.01)
    return {"x_prompt": x_prompt, "x_sample": x_sample, "cache_k": cache_k, "cache_v": cache_v,
            "page_table": page_table, "state_conv": state_conv, "state_delta": state_delta,
            "state_s5_re": state_s5_re, "state_s5_im": state_s5_im, "p_prompt": p_prompt, "p_sample": p_sample,
            "norm_g": norm_g, "w_in": w_in, "dn_conv_w": dn_conv_w, "dn_a_log": dn_a_log,
            "dn_dt_bias": dn_dt_bias, "dn_norm_g": dn_norm_g, "s5_lambda_re": s5_lambda_re,
            "s5_lambda_im": s5_lambda_im, "s5_b_re": s5_b_re, "s5_b_im": s5_b_im, "s5_c_re": s5_c_re,
            "s5_c_im": s5_c_im, "s5_d": s5_d, "s5_log_dt": s5_log_dt, "s5_w_glu": s5_w_glu,
            "s5_b_glu": s5_b_glu, "w_branch": w_branch, "w_out": w_out, "ple_w": ple_w,
            "ple_norm_g": ple_norm_g, "ple_w_gate": ple_w_gate, "final_norm_g": final_norm_g}


def reference(x_prompt, x_sample, cache_k, cache_v, page_table, state_conv, state_delta, state_s5_re,
              state_s5_im, p_prompt, p_sample, norm_g, w_in, dn_conv_w, dn_a_log, dn_dt_bias, dn_norm_g,
              s5_lambda_re, s5_lambda_im, s5_b_re, s5_b_im, s5_c_re, s5_c_im, s5_d, s5_log_dt, s5_w_glu,
              s5_b_glu, w_branch, w_out, ple_w, ple_norm_g, ple_w_gate, final_norm_g):
    f32 = jnp.float32
    bp = x_prompt.shape[0]
    bs = x_sample.shape[0]
    past_len = page_table.shape[1] * PAGE_SIZE
    xp, xs = x_prompt, x_sample
    new_p, new_s = [], []
    for i in range(DEPTH):
        lw = {"norm_g": norm_g[i], "w_in": w_in[i], "dn_conv_w": dn_conv_w[i], "dn_a_log": dn_a_log[i],
              "dn_dt_bias": dn_dt_bias[i], "dn_norm_g": dn_norm_g[i], "s5_lambda_re": s5_lambda_re[i],
              "s5_lambda_im": s5_lambda_im[i], "s5_b_re": s5_b_re[i], "s5_b_im": s5_b_im[i],
              "s5_c_re": s5_c_re[i], "s5_c_im": s5_c_im[i], "s5_d": s5_d[i], "s5_log_dt": s5_log_dt[i],
              "s5_w_glu": s5_w_glu[i], "s5_b_glu": s5_b_glu[i], "w_branch": w_branch[i], "w_out": w_out[i],
              "ple_w": ple_w[i], "ple_norm_g": ple_norm_g[i], "ple_w_gate": ple_w_gate[i]}
        xp, st = mixer_layer(xp, p_prompt[i], 0, None, None,
                             jnp.zeros((bp, CONV_W - 1, CONV_CH), f32),
                             jnp.zeros((bp, H_B, DK_B, DV_B), f32),
                             jnp.zeros((bp, S5_GROUPS, S5_STATE), f32),
                             jnp.zeros((bp, S5_GROUPS, S5_STATE), f32), lw)
        new_p.append(st)
        k_past = cache_k[i][page_table].reshape(bs, past_len, KVH_A, HD_A)
        v_past = cache_v[i][page_table].reshape(bs, past_len, KVH_A, HD_A)
        xs, st = mixer_layer(xs, p_sample[i], past_len, k_past, v_past, state_conv[i], state_delta[i],
                             state_s5_re[i], state_s5_im[i], lw)
        new_s.append(st)
    y_prompt = rmsnorm(xp, final_norm_g)
    y_sample = rmsnorm(xs, final_norm_g)

    def stk(states, j):
        return jnp.stack([s[j] for s in states])

    return (y_prompt, y_sample,
            stk(new_p, 0), stk(new_p, 1), stk(new_p, 2), stk(new_p, 3), stk(new_p, 4), stk(new_p, 5),
            stk(new_s, 0), stk(new_s, 1), stk(new_s, 2), stk(new_s, 3), stk(new_s, 4), stk(new_s, 5))
```

```python
import functools
import math

import jax
import jax.numpy as jnp
from jax import lax
from jax.experimental import pallas as pl
from jax.experimental.pallas import tpu as pltpu

f32 = jnp.float32
bf16 = jnp.bfloat16

D_MODEL = 2048
H_A, KVH_A, HD_A = 8, 4, 128
GROUP_A = H_A // KVH_A
MOBA_BLOCK, MOBA_TOPK = 256, 3
H_B, DK_B, DV_B, CONV_W = 8, 128, 128, 4
DN_CHUNK = 64
W_C, S5_GROUP_CH, S5_GROUPS, S5_STATE = 1024, 16, 64, 64
PLE_DIM = 256
PAGE_SIZE = 128
EPS = 1e-6
Q_A = H_A * HD_A
KV_A = KVH_A * HD_A
V_B = H_B * DV_B
CONV_CH = 2 * H_B * DK_B + V_B
N_GATE = 3 * D_MODEL

OFF_G = 0
OFF_QA = OFF_G + N_GATE
OFF_KA = OFF_QA + Q_A
OFF_VA = OFF_KA + KV_A
OFF_ZA = OFF_VA + KV_A
OFF_CONV = OFF_ZA + Q_A
OFF_ZB = OFF_CONV + CONV_CH
OFF_UC = OFF_ZB + V_B
OFF_ZC = OFF_UC + W_C
OFF_AB = OFF_ZC + W_C
AB_W = 512
PW = OFF_AB + AB_W

LANE = 128
SUBLANE = 8
TN_IN = 512
NEG = -1e30
ATT_SCALE = HD_A ** -0.5
VMEM_BIG = 56 * 1024 * 1024
HIGHEST = lax.Precision.HIGHEST
NT = (((1,), (1,)), ((), ()))


def _cparams(n_axes, vmem=None):
    return pltpu.CompilerParams(dimension_semantics=("arbitrary",) * n_axes, vmem_limit_bytes=vmem)


def _sigmoid(x):
    return 1.0 / (1.0 + jnp.exp(-x))


def _silu(x):
    return x * _sigmoid(x)


def _softplus(x):
    return jnp.maximum(x, 0.0) + jnp.log1p(jnp.exp(-jnp.abs(x)))


def _rms(x, g):
    return x * lax.rsqrt(jnp.mean(x * x, axis=-1, keepdims=True) + EPS) * g


def _dot(a, b):
    return jnp.dot(a, b, preferred_element_type=f32)


def _inproj_kernel(x_ref, g_ref, w_ref, o_ref, h_scr):
    @pl.when(pl.program_id(1) == 0)
    def _():
        h_scr[...] = _rms(x_ref[...], g_ref[...]).astype(bf16)

    o_ref[...] = _dot(h_scr[...], w_ref[...])


def _inproj(x2d, g, w_pad, tm):
    m = x2d.shape[0]
    return pl.pallas_call(
        _inproj_kernel,
        out_shape=jax.ShapeDtypeStruct((m, PW), f32),
        grid=(m // tm, PW // TN_IN),
        in_specs=[pl.BlockSpec((tm, D_MODEL), lambda i, n: (i, 0)),
                  pl.BlockSpec((1, D_MODEL), lambda i, n: (0, 0)),
                  pl.BlockSpec((D_MODEL, TN_IN), lambda i, n: (0, n))],
        out_specs=pl.BlockSpec((tm, TN_IN), lambda i, n: (i, n)),
        scratch_shapes=[pltpu.VMEM((tm, D_MODEL), bf16)],
        compiler_params=_cparams(2, VMEM_BIG),
        name="inproj",
    )(x2d, g.reshape(1, D_MODEL), w_pad)


def _topk_select(scores, n_valid, lane_f, n_sel):
    sc = jnp.where(lane_f < n_valid, scores, -jnp.inf)
    sel = jnp.zeros_like(sc)
    for _ in range(n_sel):
        m = jnp.max(sc, axis=-1, keepdims=True)
        idx = jnp.min(jnp.where(sc == m, lane_f, float(LANE)), axis=-1, keepdims=True)
        hit = lane_f == idx
        sel = jnp.maximum(sel, jnp.where(hit, jnp.where(m > -jnp.inf, 1.0, 0.0), 0.0))
        sc = jnp.where(hit, -jnp.inf, sc)
    return sel


def _moba_prompt_kernel(q_ref, k_ref, v_ref, za_ref, o_ref,
                        kb_scr, vb_scr, kmean_scr, bias_scr, s256_scr, acc_scr, m_scr, *, nb):
    g = pl.program_id(0)
    i = pl.program_id(1)
    rows = GROUP_A * MOBA_BLOCK

    @pl.when(i == 0)
    def _():
        kb_scr[...] = k_ref[...].astype(bf16)
        vb_scr[:, 0:HD_A] = v_ref[...].astype(bf16)
        vb_scr[:, HD_A:] = jnp.ones((nb * MOBA_BLOCK, HD_A), bf16)
        kmean_scr[...] = jnp.zeros_like(kmean_scr)
        for n in range(nb):
            kmean_scr[n:n + 1, :] = jnp.mean(k_ref[n * MOBA_BLOCK:(n + 1) * MOBA_BLOCK, :], axis=0, keepdims=True)
        row = lax.broadcasted_iota(jnp.int32, (rows, MOBA_BLOCK), 0)
        col = lax.broadcasted_iota(jnp.int32, (rows, MOBA_BLOCK), 1)
        second = row >= MOBA_BLOCK
        gf = jnp.zeros((rows, MOBA_BLOCK), f32) + g.astype(f32)
        slope = jnp.exp2(-(2.0 * gf + 1.0) - jnp.where(second, 1.0, 0.0))
        rin = row - jnp.where(second, MOBA_BLOCK, 0)
        bias_scr[...] = slope * (col - rin).astype(f32)
        s256_scr[...] = slope * float(MOBA_BLOCK)

    q = q_ref[...]
    q2 = jnp.concatenate([q[:, :HD_A], q[:, HD_A:]], axis=0)
    lane_f = lax.broadcasted_iota(jnp.int32, (rows, LANE), 1).astype(f32)
    scores = lax.dot_general(q2, kmean_scr[...], NT, precision=HIGHEST, preferred_element_type=f32)
    i_f = jnp.zeros((rows, LANE), f32) + i.astype(f32)
    sel = _topk_select(scores, i_f, lane_f, min(MOBA_TOPK, nb))
    selneg = jnp.where(sel > 0.0, 0.0, NEG)
    lhs = jnp.concatenate([q2.astype(bf16), selneg.astype(bf16)], axis=1)

    acc_scr[...] = jnp.zeros_like(acc_scr)
    m_scr[...] = jnp.full_like(m_scr, -jnp.inf)

    def attend(n, aux, extra_bias, causal):
        off = pl.multiple_of(n * MOBA_BLOCK, MOBA_BLOCK)
        rhs = jnp.concatenate([kb_scr[pl.ds(off, MOBA_BLOCK), :], aux], axis=1)
        s = lax.dot_general(lhs, rhs, NT, preferred_element_type=f32)
        s = s * ATT_SCALE + bias_scr[...]
        if extra_bias is not None:
            s = s - extra_bias
        if causal:
            row = lax.broadcasted_iota(jnp.int32, (rows, MOBA_BLOCK), 0)
            col = lax.broadcasted_iota(jnp.int32, (rows, MOBA_BLOCK), 1)
            rin = row - jnp.where(row >= MOBA_BLOCK, MOBA_BLOCK, 0)
            s = jnp.where(col <= rin, s, NEG)
        m_prev = m_scr[...]
        m_new = jnp.maximum(m_prev, jnp.max(s, axis=1, keepdims=True))
        alpha = jnp.exp(m_prev - m_new)
        p = jnp.exp(s - jnp.concatenate([m_new, m_new], axis=1))
        pv = _dot(p.astype(bf16), vb_scr[pl.ds(off, MOBA_BLOCK), :])
        acc_scr[...] = acc_scr[...] * jnp.concatenate([alpha, alpha], axis=1) + pv
        m_scr[...] = m_new

    attend(i, jnp.zeros((MOBA_BLOCK, LANE), bf16), None, True)

    lane_k = lax.broadcasted_iota(jnp.int32, (MOBA_BLOCK, LANE), 1)

    def body(n, c):
        aux = jnp.where(lane_k == n, 1.0, 0.0).astype(bf16)
        d = jnp.zeros((rows, MOBA_BLOCK), f32) + (i - n).astype(f32)
        attend(n, aux, s256_scr[...] * d, False)
        return c

    lax.fori_loop(0, i, body, 0)

    acc = acc_scr[...]
    o = acc[:, :HD_A] / acc[:, HD_A:]
    o2 = jnp.concatenate([o[:MOBA_BLOCK], o[MOBA_BLOCK:]], axis=1)
    o_ref[...] = o2 * _silu(za_ref[...])


def _moba_prompt(proj):
    L = proj.shape[0]
    nb = L // MOBA_BLOCK
    rows = GROUP_A * MOBA_BLOCK
    wq = GROUP_A * HD_A
    return pl.pallas_call(
        functools.partial(_moba_prompt_kernel, nb=nb),
        out_shape=jax.ShapeDtypeStruct((L, Q_A), f32),
        grid=(KVH_A, nb),
        in_specs=[pl.BlockSpec((MOBA_BLOCK, wq), lambda g, i: (i, OFF_QA // wq + g)),
                  pl.BlockSpec((L, HD_A), lambda g, i: (0, OFF_KA // HD_A + g)),
                  pl.BlockSpec((L, HD_A), lambda g, i: (0, OFF_VA // HD_A + g)),
                  pl.BlockSpec((MOBA_BLOCK, wq), lambda g, i: (i, OFF_ZA // wq + g))],
        out_specs=pl.BlockSpec((MOBA_BLOCK, wq), lambda g, i: (i, g)),
        scratch_shapes=[pltpu.VMEM((L, HD_A), bf16),
                        pltpu.VMEM((L, 2 * HD_A), bf16),
                        pltpu.VMEM((LANE, HD_A), f32),
                        pltpu.VMEM((rows, MOBA_BLOCK), f32),
                        pltpu.VMEM((rows, MOBA_BLOCK), f32),
                        pltpu.VMEM((rows, 2 * HD_A), f32),
                        pltpu.VMEM((rows, HD_A), f32)],
        compiler_params=_cparams(2, VMEM_BIG),
        name="moba_prompt",
    )(proj, proj, proj, proj)


def _moba_decode_kernel(pt_ref, q_ref, kn_ref, vn_ref, za_ref, ck_ref, cv_ref, o_ref,
                        kbuf, vbuf, sems, *, layer, n_pages, nbatch):
    b = pl.program_id(0)
    past = n_pages * PAGE_SIZE
    nbp = past // MOBA_BLOCK

    def copies(bb, slot):
        out = []
        for p in range(n_pages):
            page = pt_ref[bb * n_pages + p]
            out.append(pltpu.make_async_copy(ck_ref.at[layer, page], kbuf.at[slot, pl.ds(p * PAGE_SIZE, PAGE_SIZE), :],
                                             sems.at[0, slot]))
            out.append(pltpu.make_async_copy(cv_ref.at[layer, page], vbuf.at[slot, pl.ds(p * PAGE_SIZE, PAGE_SIZE), :],
                                             sems.at[1, slot]))
        return out

    slot = lax.rem(b, 2)

    @pl.when(b == 0)
    def _():
        for c in copies(0, 0):
            c.start()

    @pl.when(b + 1 < nbatch)
    def _():
        for c in copies(b + 1, 1 - slot):
            c.start()

    for c in copies(b, slot):
        c.wait()

    kf = kbuf[slot]
    vf = vbuf[slot]
    q = q_ref[0]
    zrow = jnp.zeros((1, HD_A), f32)
    qbd = jnp.concatenate(
        [jnp.concatenate([q[:, h * HD_A:(h + 1) * HD_A] if h // GROUP_A == g else zrow for g in range(KVH_A)], axis=1)
         for h in range(H_A)], axis=0)
    kmean = jnp.concatenate(
        [jnp.mean(kf[n * MOBA_BLOCK:(n + 1) * MOBA_BLOCK, :], axis=0, keepdims=True) for n in range(nbp)]
        + [jnp.zeros((LANE - nbp, KV_A), f32)], axis=0)
    scores = lax.dot_general(qbd, kmean, NT, precision=HIGHEST, preferred_element_type=f32)
    lane_f = lax.broadcasted_iota(jnp.int32, (H_A, LANE), 1).astype(f32)
    sel = _topk_select(scores, float(nbp), lane_f, min(MOBA_TOPK, nbp + 1))

    qb = qbd.astype(bf16)
    s = lax.dot_general(qb, kf.astype(bf16), NT, preferred_element_type=f32)
    hrow = lax.broadcasted_iota(jnp.int32, (H_A, past), 0).astype(f32)
    pos = lax.broadcasted_iota(jnp.int32, (H_A, past), 1)
    slope = jnp.exp2(-(hrow + 1.0))
    s = s * ATT_SCALE - slope * (past - pos).astype(f32)
    selx = jnp.concatenate([jnp.broadcast_to(sel[:, n:n + 1], (H_A, MOBA_BLOCK)) for n in range(nbp)], axis=1)
    s = jnp.where(selx > 0.0, s, NEG)
    kn = kn_ref[0].astype(bf16).astype(f32)
    vn = vn_ref[0].astype(bf16).astype(f32)
    s_own = jnp.sum(qb.astype(f32) * kn, axis=1, keepdims=True) * ATT_SCALE
    m = jnp.maximum(jnp.max(s, axis=1, keepdims=True), s_own)
    p = jnp.exp(s - m)
    p_own = jnp.exp(s_own - m)
    pb = p.astype(bf16)
    l = jnp.sum(pb.astype(f32), axis=1, keepdims=True) + p_own.astype(bf16).astype(f32)
    o = _dot(pb, vf.astype(bf16)) + p_own.astype(bf16).astype(f32) * vn
    o = o / l
    orow = jnp.concatenate([o[h:h + 1, (h // GROUP_A) * HD_A:(h // GROUP_A + 1) * HD_A] for h in range(H_A)], axis=1)
    o_ref[0] = orow * _silu(za_ref[0])


def _moba_decode(proj_s, cache_k, cache_v, page_table, layer):
    nbatch, n_pages = page_table.shape
    past = n_pages * PAGE_SIZE
    n_phys = cache_k.shape[1]
    ck = cache_k.reshape(cache_k.shape[0], n_phys, PAGE_SIZE, KV_A)
    cv = cache_v.reshape(cache_v.shape[0], n_phys, PAGE_SIZE, KV_A)
    p3 = proj_s.reshape(nbatch, 1, PW)
    grid_spec = pltpu.PrefetchScalarGridSpec(
        num_scalar_prefetch=1,
        grid=(nbatch,),
        in_specs=[pl.BlockSpec((1, 1, Q_A), lambda b, pt: (b, 0, OFF_QA // Q_A)),
                  pl.BlockSpec((1, 1, KV_A), lambda b, pt: (b, 0, OFF_KA // KV_A)),
                  pl.BlockSpec((1, 1, KV_A), lambda b, pt: (b, 0, OFF_VA // KV_A)),
                  pl.BlockSpec((1, 1, Q_A), lambda b, pt: (b, 0, OFF_ZA // Q_A)),
                  pl.BlockSpec(memory_space=pl.ANY),
                  pl.BlockSpec(memory_space=pl.ANY)],
        out_specs=pl.BlockSpec((1, 1, Q_A), lambda b, pt: (b, 0, 0)),
        scratch_shapes=[pltpu.VMEM((2, past, KV_A), f32),
                        pltpu.VMEM((2, past, KV_A), f32),
                        pltpu.SemaphoreType.DMA((2, 2))],
    )
    out = pl.pallas_call(
        functools.partial(_moba_decode_kernel, layer=layer, n_pages=n_pages, nbatch=nbatch),
        out_shape=jax.ShapeDtypeStruct((nbatch, 1, Q_A), f32),
        grid_spec=grid_spec,
        compiler_params=_cparams(1, VMEM_BIG),
        name="moba_decode",
    )(page_table.reshape(-1), p3, p3, p3, p3, ck, cv)
    return out.reshape(nbatch, Q_A)


GDN_T = 256
GDN_HB = 2


def _gdn_prompt_kernel(alog_ref, dtb_ref, q_ref, k_ref, v_ref, ab_ref, zb_ref, wq_ref, wk_ref, wv_ref, ng_ref,
                       o_ref, s_out_ref, s_scr, prev_scr, vnew_scr, *, T):
    hp = pl.program_id(0)
    t = pl.program_id(1)
    C = min(DN_CHUNK, T)
    n_chunks = T // C
    shift = int(math.log2(C))

    @pl.when(t == 0)
    def _():
        s_scr[...] = jnp.zeros_like(s_scr)
        prev_scr[...] = jnp.zeros_like(prev_scr)
        vnew_scr[...] = jnp.zeros_like(vnew_scr)

    row = lax.broadcasted_iota(jnp.int32, (T, T), 0)
    col = lax.broadcasted_iota(jnp.int32, (T, T), 1)
    same = lax.shift_right_logical(row, shift) == lax.shift_right_logical(col, shift)
    tri_incl = jnp.logical_and(same, row >= col)
    tri_strict = jnp.logical_and(same, row > col)
    ltri = jnp.where(tri_incl, 1.0, 0.0)
    ab = ab_ref[...]
    lane = lax.broadcasted_iota(jnp.int32, (T, LANE), 1)
    r8 = lax.broadcasted_iota(jnp.int32, (SUBLANE, LANE), 0)
    colT = lax.broadcasted_iota(jnp.int32, (DK_B, T), 1)

    for hb in range(GDN_HB):
        h = hp * GDN_HB + hb
        sl = slice(hb * LANE, (hb + 1) * LANE)

        def conv(x_ref, w_ref, pidx):
            x = x_ref[:, sl]
            w = w_ref[:, sl]
            prev8 = prev_scr[pidx]
            acc = x * w[CONV_W - 1:CONV_W, :]
            for s in range(1, CONV_W):
                xs = pltpu.roll(x, s, axis=0)
                top = jnp.where(r8 < s, pltpu.roll(prev8, s, axis=0), xs[:SUBLANE])
                xs = jnp.concatenate([top, xs[SUBLANE:]], axis=0)
                acc = acc + xs * w[CONV_W - 1 - s:CONV_W - s, :]
            prev_scr[pidx] = x[T - SUBLANE:, :]
            return _silu(acc)

        qc = conv(q_ref, wq_ref, 3 * hb)
        kc = conv(k_ref, wk_ref, 3 * hb + 1)
        vc = conv(v_ref, wv_ref, 3 * hb + 2)
        qn = qc * lax.rsqrt(jnp.sum(qc * qc, axis=-1, keepdims=True) + EPS) * (DK_B ** -0.5)
        kn = kc * lax.rsqrt(jnp.sum(kc * kc, axis=-1, keepdims=True) + EPS)
        a = jnp.sum(jnp.where(lane == h, ab, 0.0), axis=-1, keepdims=True)
        b = jnp.sum(jnp.where(lane == H_B + h, ab, 0.0), axis=-1, keepdims=True)
        beta = _sigmoid(b)
        zcol = jnp.zeros((T, 1), f32)
        glog = -jnp.exp(zcol + alog_ref[h]) * _softplus(a + dtb_ref[h])
        g_rep = jnp.broadcast_to(glog, (T, LANE))
        gam = jnp.dot(ltri, g_rep, precision=HIGHEST, preferred_element_type=f32)
        gam_row = gam.T
        reps = T // LANE
        gam_i = jnp.concatenate([gam] * reps, axis=1) if reps > 1 else gam[:, :T]
        gam_j = jnp.concatenate([gam_row] * reps, axis=0) if reps > 1 else gam_row[:T]
        decay = jnp.exp(jnp.where(tri_incl, gam_i - gam_j, -jnp.inf))
        kb = kn.astype(bf16)
        kk = lax.dot_general(kb, kb, NT, preferred_element_type=f32)
        beta_i = jnp.broadcast_to(beta, (T, T))
        npow = jnp.where(tri_strict, -(beta_i * kk * decay), 0.0)
        x_acc = npow
        for _ in range(shift - 1):
            npb = npow.astype(bf16)
            npow = _dot(npb, npb)
            x_acc = x_acc + npow + _dot(npow.astype(bf16), x_acc.astype(bf16))
        egam = jnp.exp(gam)
        rhs = jnp.concatenate([vc * beta, kn * (beta * egam)], axis=1)
        sol = rhs + _dot(x_acc.astype(bf16), rhs.astype(bf16))
        u_val = sol[:, :DV_B]
        k_cum = sol[:, DV_B:]
        qk = lax.dot_general(qn.astype(bf16), kb, NT, preferred_element_type=f32) * decay
        qkb = qk.astype(bf16)
        q_dec = (qn * egam).astype(bf16)
        k_cum_b = k_cum.astype(bf16)
        knT = kn.T
        outs = []
        for c in range(n_chunks):
            rs = slice(c * C, (c + 1) * C)
            last = (c + 1) * C - 1
            s_old = s_scr[hb]
            s_b = s_old.astype(bf16)
            v_new = u_val[rs] - _dot(k_cum_b[rs], s_b)
            vnew_scr[hb, rs, :] = v_new
            vn_b = vnew_scr[hb].astype(bf16)
            outs.append(_dot(q_dec[rs], s_b) + _dot(qkb[rs, :], vn_b))
            in_chunk = jnp.logical_and(colT >= c * C, colT <= last)
            w_end = jnp.exp(jnp.where(in_chunk, gam_row[:, last:last + 1] - gam_row[:, :T], -jnp.inf))
            k_end_t = (knT * w_end).astype(bf16)
            s_scr[hb] = s_old * jnp.exp(gam[last:last + 1, :]) + _dot(k_end_t, vn_b)
        o = jnp.concatenate(outs, axis=0)
        o_ref[:, sl] = _rms(o, ng_ref[...]) * _silu(zb_ref[:, sl])
        s_out_ref[hb] = s_scr[hb]


def _gdn_prompt(proj, conv_w, a_log, dt_bias, norm_g):
    L = proj.shape[0]
    T = min(GDN_T, L)
    wb = GDN_HB * LANE
    nq = H_B * DK_B // wb
    return pl.pallas_call(
        functools.partial(_gdn_prompt_kernel, T=T),
        out_shape=(jax.ShapeDtypeStruct((L, V_B), f32), jax.ShapeDtypeStruct((H_B, DK_B, DV_B), f32)),
        grid=(H_B // GDN_HB, L // T),
        in_specs=[pl.BlockSpec(memory_space=pltpu.SMEM),
                  pl.BlockSpec(memory_space=pltpu.SMEM),
                  pl.BlockSpec((T, wb), lambda hp, t: (t, OFF_CONV // wb + hp)),
                  pl.BlockSpec((T, wb), lambda hp, t: (t, OFF_CONV // wb + nq + hp)),
                  pl.BlockSpec((T, wb), lambda hp, t: (t, OFF_CONV // wb + 2 * nq + hp)),
                  pl.BlockSpec((T, LANE), lambda hp, t: (t, OFF_AB // LANE)),
                  pl.BlockSpec((T, wb), lambda hp, t: (t, OFF_ZB // wb + hp)),
                  pl.BlockSpec((CONV_W, wb), lambda hp, t: (0, hp)),
                  pl.BlockSpec((CONV_W, wb), lambda hp, t: (0, nq + hp)),
                  pl.BlockSpec((CONV_W, wb), lambda hp, t: (0, 2 * nq + hp)),
                  pl.BlockSpec((1, DV_B), lambda hp, t: (0, 0))],
        out_specs=(pl.BlockSpec((T, wb), lambda hp, t: (t, hp)),
                   pl.BlockSpec((GDN_HB, DK_B, DV_B), lambda hp, t: (hp, 0, 0))),
        scratch_shapes=[pltpu.VMEM((GDN_HB, DK_B, DV_B), f32),
                        pltpu.VMEM((3 * GDN_HB, SUBLANE, LANE), f32),
                        pltpu.VMEM((GDN_HB, T, DV_B), f32)],
        compiler_params=_cparams(2, VMEM_BIG),
        name="gdn_prompt",
    )(a_log, dt_bias, proj, proj, proj, proj, proj, conv_w, conv_w, conv_w, norm_g.reshape(1, DV_B))


GDN_BB = 8


def _gdn_decode_kernel(x_ref, ab_ref, zb_ref, cs_ref, sd_ref, w_ref, alog_ref, dtb_ref, ng_ref,
                       o_ref, cs_out_ref, sd_out_ref):
    x = x_ref[...]
    w = w_ref[...]
    c0, c1, c2 = cs_ref[:, 0, :], cs_ref[:, 1, :], cs_ref[:, 2, :]
    y = c0 * w[0:1, :] + c1 * w[1:2, :] + c2 * w[2:3, :] + x * w[3:4, :]
    cs_out_ref[:, 0, :] = c1
    cs_out_ref[:, 1, :] = c2
    cs_out_ref[:, 2, :] = x
    y = _silu(y)
    ab = ab_ref[...]
    a = ab[:, 0:H_B]
    beta = _sigmoid(ab[:, H_B:2 * H_B])
    g = -jnp.exp(alog_ref[...]) * _softplus(a + dtb_ref[...])
    eg = jnp.exp(g)
    nq = H_B * DK_B
    outs = []
    for h in range(H_B):
        qc = y[:, h * DK_B:(h + 1) * DK_B]
        kc = y[:, nq + h * DK_B:nq + (h + 1) * DK_B]
        vc = y[:, 2 * nq + h * DV_B:2 * nq + (h + 1) * DV_B]
        qn = qc * lax.rsqrt(jnp.sum(qc * qc, axis=-1, keepdims=True) + EPS) * (DK_B ** -0.5)
        kn = kc * lax.rsqrt(jnp.sum(kc * kc, axis=-1, keepdims=True) + EPS)
        qk = jnp.sum(qn * kn, axis=-1, keepdims=True)
        knT = kn.T
        qnT = qn.T
        rows = []
        for bb in range(GDN_BB):
            s_old = sd_ref[bb, h]
            kcol = knT[:, bb:bb + 1]
            qcol = qnT[:, bb:bb + 1]
            be = beta[bb:bb + 1, h:h + 1]
            e = eg[bb:bb + 1, h:h + 1]
            ks = jnp.sum(kcol * s_old, axis=0, keepdims=True)
            qs = jnp.sum(qcol * s_old, axis=0, keepdims=True)
            v_new = be * vc[bb:bb + 1, :] - (be * e) * ks
            rows.append(e * qs + qk[bb:bb + 1, :] * v_new)
            sd_out_ref[bb, h] = s_old * e + kcol * v_new
        o = jnp.concatenate(rows, axis=0)
        outs.append(_rms(o, ng_ref[...]) * _silu(zb_ref[:, h * DV_B:(h + 1) * DV_B]))
    o_ref[...] = jnp.concatenate(outs, axis=1)


def _gdn_decode(proj_s, state_conv, state_delta, conv_w, a_log, dt_bias, norm_g, layer):
    nbatch = proj_s.shape[0]
    bb = GDN_BB
    return pl.pallas_call(
        _gdn_decode_kernel,
        out_shape=(jax.ShapeDtypeStruct((nbatch, V_B), f32),
                   jax.ShapeDtypeStruct((nbatch, CONV_W - 1, CONV_CH), f32),
                   jax.ShapeDtypeStruct((nbatch, H_B, DK_B, DV_B), f32)),
        grid=(nbatch // bb,),
        in_specs=[pl.BlockSpec((bb, CONV_CH), lambda i: (i, OFF_CONV // CONV_CH)),
                  pl.BlockSpec((bb, LANE), lambda i: (i, OFF_AB // LANE)),
                  pl.BlockSpec((bb, V_B), lambda i: (i, OFF_ZB // V_B)),
                  pl.BlockSpec((None, bb, CONV_W - 1, CONV_CH), lambda i: (layer, i, 0, 0)),
                  pl.BlockSpec((None, bb, H_B, DK_B, DV_B), lambda i: (layer, i, 0, 0, 0)),
                  pl.BlockSpec((CONV_W, CONV_CH), lambda i: (0, 0)),
                  pl.BlockSpec((1, H_B), lambda i: (0, 0)),
                  pl.BlockSpec((1, H_B), lambda i: (0, 0)),
                  pl.BlockSpec((1, DV_B), lambda i: (0, 0))],
        out_specs=(pl.BlockSpec((bb, V_B), lambda i: (i, 0)),
                   pl.BlockSpec((bb, CONV_W - 1, CONV_CH), lambda i: (i, 0, 0)),
                   pl.BlockSpec((bb, H_B, DK_B, DV_B), lambda i: (i, 0, 0, 0))),
        compiler_params=_cparams(1, VMEM_BIG),
        name="gdn_decode",
    )(proj_s, proj_s, proj_s, state_conv, state_delta, conv_w,
      a_log.reshape(1, H_B), dt_bias.reshape(1, H_B), norm_g.reshape(1, DV_B))


S5_JB = W_C // LANE
S5_GPB = LANE // S5_GROUP_CH
S5_SW = S5_GPB * S5_STATE
S5_TT = 512
S5_NSEG = SUBLANE


def _s5_param_kernel(lr_ref, li_ref, ldt_ref, bre_ref, bim_ref, are_ref, aim_ref, bbre_ref, bbim_ref):
    lr = lr_ref[...]
    li = li_ref[...]
    dt = jnp.exp(ldt_ref[...])
    mag = jnp.exp(lr * dt)
    ang = li * dt
    a_re = mag * jnp.cos(ang)
    a_im = mag * jnp.sin(ang)
    den = lr * lr + li * li
    f_re = ((a_re - 1.0) * lr + a_im * li) / den
    f_im = (a_im * lr - (a_re - 1.0) * li) / den
    are_ref[...] = a_re
    aim_ref[...] = a_im
    fr = f_re[:, None, :]
    fi = f_im[:, None, :]
    bre = bre_ref[...]
    bim = bim_ref[...]
    bbre_ref[...] = fr * bre - fi * bim
    bbim_ref[...] = fr * bim + fi * bre


def _s5_params(lam_re, lam_im, log_dt, b_re, b_im, c_re, c_im, d):
    G, P, CH = S5_GROUPS, S5_STATE, S5_GROUP_CH
    bt_re = jnp.swapaxes(b_re, 1, 2)
    bt_im = jnp.swapaxes(b_im, 1, 2)
    a_re, a_im, bb_re, bb_im = pl.pallas_call(
        _s5_param_kernel,
        out_shape=(jax.ShapeDtypeStruct((G, P), f32), jax.ShapeDtypeStruct((G, P), f32),
                   jax.ShapeDtypeStruct((G, CH, P), f32), jax.ShapeDtypeStruct((G, CH, P), f32)),
        name="s5_params",
    )(lam_re, lam_im, log_dt.reshape(G, 1), bt_re, bt_im)
    eye = jnp.eye(S5_GPB, dtype=f32)

    def bdiag_in(bb):
        x = bb.reshape(S5_JB, S5_GPB, CH, P)
        return jnp.einsum("jacp,ab->jacbp", x, eye).reshape(S5_JB, LANE, S5_SW)

    def bdiag_out(c):
        x = c.reshape(S5_JB, S5_GPB, CH, P)
        return jnp.einsum("jacp,ab->jbpac", x, eye).reshape(S5_JB, S5_SW, LANE)

    wb = jnp.concatenate([bdiag_in(bb_re), bdiag_in(bb_im)], axis=-1).astype(bf16)
    wc_re = bdiag_out(c_re).astype(bf16)
    wc_im = bdiag_out(c_im).astype(bf16)
    return dict(wb=wb, wc_re=wc_re, wc_im=wc_im,
                a_re=a_re.reshape(S5_JB, 1, S5_SW), a_im=a_im.reshape(S5_JB, 1, S5_SW),
                d=d.reshape(S5_JB, 1, LANE))


def _s5_prompt_kernel(u_ref, wb_ref, wcre_ref, wcim_ref, are_ref, aim_ref, d_ref,
                      y_ref, hre_ref, him_ref, hs_scr, carry_scr, *, TT):
    t = pl.program_id(1)
    nlb = S5_SW // LANE
    seg = TT // S5_NSEG

    @pl.when(t == 0)
    def _():
        carry_scr[...] = jnp.zeros_like(carry_scr)

    u = u_ref[...]
    bu = _dot(u.astype(bf16), wb_ref[...])
    for c in range(2 * nlb):
        hs_scr[c] = bu[:, c * LANE:(c + 1) * LANE]

    are = are_ref[...]
    aim = aim_ref[...]
    ar8 = [jnp.broadcast_to(are[:, c * LANE:(c + 1) * LANE], (S5_NSEG, LANE)) for c in range(nlb)]
    ai8 = [jnp.broadcast_to(aim[:, c * LANE:(c + 1) * LANE], (S5_NSEG, LANE)) for c in range(nlb)]

    def seg_rows(k):
        return pl.ds(k, S5_NSEG, stride=seg)

    def p1(k, hs):
        new_r, new_i = [], []
        for c in range(nlb):
            hr, hi = hs[c], hs[nlb + c]
            nr = ar8[c] * hr - ai8[c] * hi + hs_scr[c, seg_rows(k), :]
            ni = ar8[c] * hi + ai8[c] * hr + hs_scr[nlb + c, seg_rows(k), :]
            hs_scr[c, seg_rows(k), :] = nr
            hs_scr[nlb + c, seg_rows(k), :] = ni
            new_r.append(nr)
            new_i.append(ni)
        return tuple(new_r + new_i)

    zero8 = jnp.zeros((S5_NSEG, LANE), f32)
    ends = lax.fori_loop(0, seg, p1, tuple(zero8 for _ in range(2 * nlb)))

    pr, pi = are, aim
    for _ in range(int(math.log2(seg))):
        pr, pi = pr * pr - pi * pi, 2.0 * pr * pi

    carry = carry_scr[...]
    er, ei = carry[:, :S5_SW], carry[:, S5_SW:]
    rows_r, rows_i = [], []
    for s in range(S5_NSEG):
        rows_r.append(er)
        rows_i.append(ei)
        loc_r = jnp.concatenate([ends[c][s:s + 1, :] for c in range(nlb)], axis=1)
        loc_i = jnp.concatenate([ends[nlb + c][s:s + 1, :] for c in range(nlb)], axis=1)
        er, ei = pr * er - pi * ei + loc_r, pr * ei + pi * er + loc_i
    carry_scr[...] = jnp.concatenate([er, ei], axis=1)
    hre_ref[0] = er
    him_ref[0] = ei
    ein_r = jnp.concatenate(rows_r, axis=0)
    ein_i = jnp.concatenate(rows_i, axis=0)

    def p2(k, pw):
        new = []
        for c in range(nlb):
            p_r, p_i = pw[c], pw[nlb + c]
            e_r = ein_r[:, c * LANE:(c + 1) * LANE]
            e_i = ein_i[:, c * LANE:(c + 1) * LANE]
            hs_scr[c, seg_rows(k), :] = hs_scr[c, seg_rows(k), :] + (p_r * e_r - p_i * e_i)
            hs_scr[nlb + c, seg_rows(k), :] = hs_scr[nlb + c, seg_rows(k), :] + (p_r * e_i + p_i * e_r)
            new.append((p_r * ar8[c] - p_i * ai8[c], p_r * ai8[c] + p_i * ar8[c]))
        return tuple([n[0] for n in new] + [n[1] for n in new])

    lax.fori_loop(0, seg, p2, tuple(ar8 + ai8))

    hr = jnp.concatenate([hs_scr[c] for c in range(nlb)], axis=1).astype(bf16)
    hi = jnp.concatenate([hs_scr[nlb + c] for c in range(nlb)], axis=1).astype(bf16)
    y_ref[...] = _dot(hr, wcre_ref[...]) - _dot(hi, wcim_ref[...]) + d_ref[...] * u


def _s5_prompt(proj, sp):
    L = proj.shape[0]
    TT = min(S5_TT, L)
    jb_spec = lambda shape: pl.BlockSpec((None,) + shape, lambda j, t: (j, 0, 0))
    y, hre, him = pl.pallas_call(
        functools.partial(_s5_prompt_kernel, TT=TT),
        out_shape=(jax.ShapeDtypeStruct((L, W_C), f32),
                   jax.ShapeDtypeStruct((S5_JB, 1, S5_SW), f32),
                   jax.ShapeDtypeStruct((S5_JB, 1, S5_SW), f32)),
        grid=(S5_JB, L // TT),
        in_specs=[pl.BlockSpec((TT, LANE), lambda j, t: (t, OFF_UC // LANE + j)),
                  jb_spec((LANE, 2 * S5_SW)), jb_spec((S5_SW, LANE)), jb_spec((S5_SW, LANE)),
                  jb_spec((1, S5_SW)), jb_spec((1, S5_SW)), jb_spec((1, LANE))],
        out_specs=(pl.BlockSpec((TT, LANE), lambda j, t: (t, j)),
                   pl.BlockSpec((1, 1, S5_SW), lambda j, t: (j, 0, 0)),
                   pl.BlockSpec((1, 1, S5_SW), lambda j, t: (j, 0, 0))),
        scratch_shapes=[pltpu.VMEM((2 * S5_SW // LANE, TT, LANE), f32),
                        pltpu.VMEM((1, 2 * S5_SW), f32)],
        compiler_params=_cparams(2, VMEM_BIG),
        name="s5_prompt",
    )(proj, sp["wb"], sp["wc_re"], sp["wc_im"], sp["a_re"], sp["a_im"], sp["d"])
    return y, hre.reshape(S5_GROUPS, S5_STATE), him.reshape(S5_GROUPS, S5_STATE)


def _s5_decode_kernel(u_ref, h0r_ref, h0i_ref, wb_ref, wcre_ref, wcim_ref, are_ref, aim_ref, d_ref,
                      y_ref, hr_ref, hi_ref):
    u = u_ref[...]
    bu = _dot(u.astype(bf16), wb_ref[...])
    are, aim = are_ref[...], aim_ref[...]
    h0r, h0i = h0r_ref[...], h0i_ref[...]
    hr = bu[:, :S5_SW] + are * h0r - aim * h0i
    hi = bu[:, S5_SW:] + are * h0i + aim * h0r
    hr_ref[...] = hr
    hi_ref[...] = hi
    y_ref[...] = _dot(hr.astype(bf16), wcre_ref[...]) - _dot(hi.astype(bf16), wcim_ref[...]) + d_ref[...] * u


def _s5_decode(proj_s, st_re, st_im, sp, layer):
    nbatch = proj_s.shape[0]
    sre = st_re.reshape(st_re.shape[0], nbatch, S5_GROUPS * S5_STATE)
    sim = st_im.reshape(st_im.shape[0], nbatch, S5_GROUPS * S5_STATE)
    jb_spec = lambda shape: pl.BlockSpec((None,) + shape, lambda j: (j, 0, 0))
    st_spec = pl.BlockSpec((None, nbatch, S5_SW), lambda j: (layer, 0, j))
    y, hr, hi = pl.pallas_call(
        _s5_decode_kernel,
        out_shape=(jax.ShapeDtypeStruct((nbatch, W_C), f32),
                   jax.ShapeDtypeStruct((nbatch, S5_GROUPS * S5_STATE), f32),
                   jax.ShapeDtypeStruct((nbatch, S5_GROUPS * S5_STATE), f32)),
        grid=(S5_JB,),
        in_specs=[pl.BlockSpec((nbatch, LANE), lambda j: (0, OFF_UC // LANE + j)),
                  st_spec, st_spec,
                  jb_spec((LANE, 2 * S5_SW)), jb_spec((S5_SW, LANE)), jb_spec((S5_SW, LANE)),
                  jb_spec((1, S5_SW)), jb_spec((1, S5_SW)), jb_spec((1, LANE))],
        out_specs=(pl.BlockSpec((nbatch, LANE), lambda j: (0, j)),
                   pl.BlockSpec((nbatch, S5_SW), lambda j: (0, j)),
                   pl.BlockSpec((nbatch, S5_SW), lambda j: (0, j))),
        compiler_params=_cparams(1),
        name="s5_decode",
    )(proj_s, sre, sim, sp["wb"], sp["wc_re"], sp["wc_im"], sp["a_re"], sp["a_im"], sp["d"])
    return (y, hr.reshape(nbatch, S5_GROUPS, S5_STATE), hi.reshape(nbatch, S5_GROUPS, S5_STATE))


def _s5_glu_kernel(y_ref, zc_ref, w_ref, b_ref, o_ref):
    y = y_ref[...]
    ge = 0.5 * y * (1.0 + lax.erf(y * (0.5 ** 0.5)))
    gl = _dot(ge.astype(bf16), w_ref[...]) + b_ref[...]
    o_ref[...] = ge * _sigmoid(gl) * _silu(zc_ref[...])


def _s5_glu(y, proj, w_glu, b_glu, tm):
    m = y.shape[0]
    return pl.pallas_call(
        _s5_glu_kernel,
        out_shape=jax.ShapeDtypeStruct((m, W_C), f32),
        grid=(m // tm,),
        in_specs=[pl.BlockSpec((tm, W_C), lambda i: (i, 0)),
                  pl.BlockSpec((tm, W_C), lambda i: (i, OFF_ZC // W_C)),
                  pl.BlockSpec((W_C, W_C), lambda i: (0, 0)),
                  pl.BlockSpec((1, W_C), lambda i: (0, 0))],
        out_specs=pl.BlockSpec((tm, W_C), lambda i: (i, 0)),
        compiler_params=_cparams(1, VMEM_BIG),
        name="s5_glu",
    )(y, proj, w_glu, b_glu.reshape(1, W_C))


def _merge_kernel(oa_ref, ob_ref, oc_ref, ga_ref, gb_ref, gc_ref, wb_ref, o_ref):
    acc = _sigmoid(ga_ref[...]) * _dot(oa_ref[...].astype(bf16), wb_ref[0])
    acc = acc + _sigmoid(gb_ref[...]) * _dot(ob_ref[...].astype(bf16), wb_ref[1])
    acc = acc + _sigmoid(gc_ref[...]) * _dot(oc_ref[...].astype(bf16), wb_ref[2])
    o_ref[...] = acc.astype(bf16)


def _merge(out_a, out_b, out_c, proj, w_branch, tm):
    m = out_a.shape[0]
    act = pl.BlockSpec((tm, Q_A), lambda i: (i, 0))
    gate = lambda k: pl.BlockSpec((tm, D_MODEL), lambda i: (i, OFF_G // D_MODEL + k))
    return pl.pallas_call(
        _merge_kernel,
        out_shape=jax.ShapeDtypeStruct((m, D_MODEL), bf16),
        grid=(m // tm,),
        in_specs=[act, act, act, gate(0), gate(1), gate(2),
                  pl.BlockSpec((3, Q_A, D_MODEL), lambda i: (0, 0, 0), pipeline_mode=pl.Buffered(1))],
        out_specs=pl.BlockSpec((tm, D_MODEL), lambda i: (i, 0)),
        compiler_params=_cparams(1, VMEM_BIG),
        name="merge",
    )(out_a, out_b, out_c, proj, proj, proj, w_branch)


def _post_kernel(x_ref, mg_ref, pe_ref, wout_ref, plew_ref, pleg_ref, png_ref, fng_ref, o_ref, *, final):
    x2 = x_ref[...] + _dot(mg_ref[...], wout_ref[...])
    gate = _sigmoid(_dot(_rms(x2, png_ref[...]).astype(bf16), pleg_ref[...]))
    x3 = x2 + _dot(pe_ref[...].astype(bf16), plew_ref[...]) * gate
    if final:
        x3 = _rms(x3, fng_ref[...])
    o_ref[...] = x3


def _post(x2d, merged, pe, w_out, ple_w, ple_w_gate, ple_norm_g, final_norm_g, final, tm):
    m = x2d.shape[0]
    const = lambda shape: pl.BlockSpec(shape, lambda i: (0, 0), pipeline_mode=pl.Buffered(1))
    return pl.pallas_call(
        functools.partial(_post_kernel, final=final),
        out_shape=jax.ShapeDtypeStruct((m, D_MODEL), f32),
        grid=(m // tm,),
        in_specs=[pl.BlockSpec((tm, D_MODEL), lambda i: (i, 0)),
                  pl.BlockSpec((tm, D_MODEL), lambda i: (i, 0)),
                  pl.BlockSpec((tm, PLE_DIM), lambda i: (i, 0)),
                  const((D_MODEL, D_MODEL)), const((PLE_DIM, D_MODEL)), const((D_MODEL, D_MODEL)),
                  const((1, D_MODEL)), const((1, D_MODEL))],
        out_specs=pl.BlockSpec((tm, D_MODEL), lambda i: (i, 0)),
        compiler_params=_cparams(1, VMEM_BIG),
        name="post",
    )(x2d, merged, pe, w_out, ple_w, ple_w_gate, ple_norm_g.reshape(1, D_MODEL), final_norm_g.reshape(1, D_MODEL))


def _pad_w_in(w_in):
    o = [0, Q_A, Q_A + KV_A, Q_A + 2 * KV_A, 2 * Q_A + 2 * KV_A]
    c_end = o[4] + CONV_CH
    ab_end = c_end + 2 * H_B
    z_end = ab_end + V_B + 2 * W_C
    pad = jnp.zeros((w_in.shape[0], AB_W - 2 * H_B), w_in.dtype)
    return jnp.concatenate([w_in[:, z_end:], w_in[:, :c_end], w_in[:, ab_end:z_end], w_in[:, c_end:ab_end], pad],
                           axis=1).astype(bf16)


def _row_tile(m, pref):
    return pref if m % pref == 0 else m


def _layer_weights(i, norm_g, w_in, dn_conv_w, dn_a_log, dn_dt_bias, dn_norm_g, s5_lambda_re, s5_lambda_im,
                   s5_b_re, s5_b_im, s5_c_re, s5_c_im, s5_d, s5_log_dt, s5_w_glu, s5_b_glu, w_branch, w_out,
                   ple_w, ple_norm_g, ple_w_gate):
    return dict(
        norm_g=norm_g[i], w_pad=_pad_w_in(w_in[i]), conv_w=dn_conv_w[i], a_log=dn_a_log[i], dt_bias=dn_dt_bias[i],
        dn_norm_g=dn_norm_g[i],
        s5=_s5_params(s5_lambda_re[i], s5_lambda_im[i], s5_log_dt[i], s5_b_re[i], s5_b_im[i], s5_c_re[i],
                      s5_c_im[i], s5_d[i]),
        w_glu=s5_w_glu[i].astype(bf16), b_glu=s5_b_glu[i], w_branch=w_branch[i].astype(bf16),
        w_out=w_out[i].astype(bf16), ple_w=ple_w[i].astype(bf16), ple_norm_g=ple_norm_g[i],
        ple_w_gate=ple_w_gate[i].astype(bf16))


def _finish(x2d, proj, out_a, out_b, y_c, pe, lw, final_norm_g, final):
    m = x2d.shape[0]
    out_c = _s5_glu(y_c, proj, lw["w_glu"], lw["b_glu"], _row_tile(m, 512))
    merged = _merge(out_a, out_b, out_c, proj, lw["w_branch"], _row_tile(m, 256))
    return _post(x2d, merged, pe, lw["w_out"], lw["ple_w"], lw["ple_w_gate"], lw["ple_norm_g"], final_norm_g,
                 final, _row_tile(m, 256))


def _prompt_layer(x2d, pe, lw, final_norm_g, final):
    L = x2d.shape[0]
    proj = _inproj(x2d, lw["norm_g"], lw["w_pad"], _row_tile(L, 1024))
    out_a = _moba_prompt(proj)
    out_b, s_new = _gdn_prompt(proj, lw["conv_w"], lw["a_log"], lw["dt_bias"], lw["dn_norm_g"])
    y_c, hre, him = _s5_prompt(proj, lw["s5"])
    x_new = _finish(x2d, proj, out_a, out_b, y_c, pe, lw, final_norm_g, final)
    ka = proj[:, OFF_KA:OFF_KA + KV_A].reshape(1, L, KVH_A, HD_A)
    va = proj[:, OFF_VA:OFF_VA + KV_A].reshape(1, L, KVH_A, HD_A)
    conv_new = proj[L - (CONV_W - 1):, OFF_CONV:OFF_CONV + CONV_CH][None]
    return x_new, (ka, va, conv_new, s_new[None], hre[None], him[None])


def _sample_layer(x2d, pe, lw, final_norm_g, final, layer, cache_k, cache_v, page_table, state_conv, state_delta,
                  state_s5_re, state_s5_im):
    nbatch = x2d.shape[0]
    proj = _inproj(x2d, lw["norm_g"], lw["w_pad"], nbatch)
    out_a = _moba_decode(proj, cache_k, cache_v, page_table, layer)
    out_b, conv_new, s_new = _gdn_decode(proj, state_conv, state_delta, lw["conv_w"], lw["a_log"], lw["dt_bias"],
                                         lw["dn_norm_g"], layer)
    y_c, hr, hi = _s5_decode(proj, state_s5_re, state_s5_im, lw["s5"], layer)
    x_new = _finish(x2d, proj, out_a, out_b, y_c, pe, lw, final_norm_g, final)
    ka = proj[:, OFF_KA:OFF_KA + KV_A].reshape(nbatch, 1, KVH_A, HD_A)
    va = proj[:, OFF_VA:OFF_VA + KV_A].reshape(nbatch, 1, KVH_A, HD_A)
    return x_new, (ka, va, conv_new, s_new, hr, hi)


def kernel(x_prompt, x_sample, cache_k, cache_v, page_table, state_conv, state_delta, state_s5_re, state_s5_im,
           p_prompt, p_sample, norm_g, w_in, dn_conv_w, dn_a_log, dn_dt_bias, dn_norm_g, s5_lambda_re, s5_lambda_im,
           s5_b_re, s5_b_im, s5_c_re, s5_c_im, s5_d, s5_log_dt, s5_w_glu, s5_b_glu, w_branch, w_out, ple_w,
           ple_norm_g, ple_w_gate, final_norm_g):
    depth = w_in.shape[0]
    bp, L, _ = x_prompt.shape
    nbatch = x_sample.shape[0]
    assert bp == 1 and x_sample.shape[1] == 1
    xp = x_prompt.reshape(L, D_MODEL)
    xs = x_sample.reshape(nbatch, D_MODEL)
    new_p, new_s = [], []
    for i in range(depth):
        lw = _layer_weights(i, norm_g, w_in, dn_conv_w, dn_a_log, dn_dt_bias, dn_norm_g, s5_lambda_re, s5_lambda_im,
                            s5_b_re, s5_b_im, s5_c_re, s5_c_im, s5_d, s5_log_dt, s5_w_glu, s5_b_glu, w_branch,
                            w_out, ple_w, ple_norm_g, ple_w_gate)
        final = i == depth - 1
        xp, st = _prompt_layer(xp, p_prompt[i, 0], lw, final_norm_g, final)
        new_p.append(st)
        xs, st = _sample_layer(xs, p_sample[i, :, 0], lw, final_norm_g, final, i, cache_k, cache_v, page_table,
                               state_conv, state_delta, state_s5_re, state_s5_im)
        new_s.append(st)

    def stk(states, j):
        return jnp.stack([s[j] for s in states])

    return (xp.reshape(1, L, D_MODEL), xs.reshape(nbatch, 1, D_MODEL),
            stk(new_p, 0), stk(new_p, 1), stk(new_p, 2), stk(new_p, 3), stk(new_p, 4), stk(new_p, 5),
            stk(new_s, 0), stk(new_s, 1), stk(new_s, 2), stk(new_s, 3), stk(new_s, 4), stk(new_s, 5))
```

```python
import functools
import math

import jax
import jax.numpy as jnp
from jax import lax
from jax.experimental import pallas as pl
from jax.experimental.pallas import tpu as pltpu

f32 = jnp.float32
bf16 = jnp.bfloat16

D_MODEL = 2048
H_A, KVH_A, HD_A = 8, 4, 128
GROUP_A = H_A // KVH_A
MOBA_BLOCK, MOBA_TOPK = 256, 3
H_B, DK_B, DV_B, CONV_W = 8, 128, 128, 4
DN_CHUNK = 64
W_C, S5_GROUP_CH, S5_GROUPS, S5_STATE = 1024, 16, 64, 64
PLE_DIM = 256
PAGE_SIZE = 128
EPS = 1e-6
Q_A = H_A * HD_A
KV_A = KVH_A * HD_A
V_B = H_B * DV_B
CONV_CH = 2 * H_B * DK_B + V_B
N_GATE = 3 * D_MODEL

OFF_G = 0
OFF_QA = OFF_G + N_GATE
OFF_KA = OFF_QA + Q_A
OFF_VA = OFF_KA + KV_A
OFF_ZA = OFF_VA + KV_A
OFF_CONV = OFF_ZA + Q_A
OFF_ZB = OFF_CONV + CONV_CH
OFF_UC = OFF_ZB + V_B
OFF_ZC = OFF_UC + W_C
OFF_AB = OFF_ZC + W_C
AB_W = 512
PW = OFF_AB + AB_W

LANE = 128
SUBLANE = 8
TN_IN = 512
NEG = -1e30
ATT_SCALE = HD_A ** -0.5
LOG2E = math.log2(math.e)
ATT_C1 = ATT_SCALE * LOG2E
VMEM_BIG = 56 * 1024 * 1024
HIGHEST = lax.Precision.HIGHEST
NT = (((1,), (1,)), ((), ()))


def _cparams(n_axes, vmem=None):
    return pltpu.CompilerParams(dimension_semantics=("arbitrary",) * n_axes, vmem_limit_bytes=vmem)


def _sigmoid(x):
    return 1.0 / (1.0 + jnp.exp(-x))


def _silu(x):
    return x * _sigmoid(x)


def _softplus(x):
    return jnp.maximum(x, 0.0) + jnp.log1p(jnp.exp(-jnp.abs(x)))


def _rms(x, g):
    return x * lax.rsqrt(jnp.mean(x * x, axis=-1, keepdims=True) + EPS) * g


def _dot(a, b):
    return jnp.dot(a, b, preferred_element_type=f32)


def _inproj_kernel(x_ref, g_ref, w_ref, o_ref, h_scr):
    @pl.when(pl.program_id(1) == 0)
    def _():
        h_scr[...] = _rms(x_ref[...], g_ref[...]).astype(bf16)

    o_ref[...] = _dot(h_scr[...], w_ref[...])


def _inproj(x2d, g, w_pad, tm):
    m = x2d.shape[0]
    return pl.pallas_call(
        _inproj_kernel,
        out_shape=jax.ShapeDtypeStruct((m, PW), f32),
        grid=(m // tm, PW // TN_IN),
        in_specs=[pl.BlockSpec((tm, D_MODEL), lambda i, n: (i, 0)),
                  pl.BlockSpec((1, D_MODEL), lambda i, n: (0, 0)),
                  pl.BlockSpec((D_MODEL, TN_IN), lambda i, n: (0, n))],
        out_specs=pl.BlockSpec((tm, TN_IN), lambda i, n: (i, n)),
        scratch_shapes=[pltpu.VMEM((tm, D_MODEL), bf16)],
        compiler_params=_cparams(2, VMEM_BIG),
        name="inproj",
    )(x2d, g.reshape(1, D_MODEL), w_pad)


def _topk_select(scores, n_valid, lane_f, n_sel):
    sc = jnp.where(lane_f < n_valid, scores, -jnp.inf)
    sel = jnp.zeros_like(sc)
    for _ in range(n_sel):
        m = jnp.max(sc, axis=-1, keepdims=True)
        idx = jnp.min(jnp.where(sc == m, lane_f, float(LANE)), axis=-1, keepdims=True)
        hit = lane_f == idx
        sel = jnp.maximum(sel, jnp.where(hit, jnp.where(m > -jnp.inf, 1.0, 0.0), 0.0))
        sc = jnp.where(hit, -jnp.inf, sc)
    return sel


def _moba_prompt_kernel(q_ref, k_ref, v_ref, za_ref, o_ref,
                        kb_scr, vb_scr, kmean_scr, bias_scr, off_scr, acc_scr, m_scr, *, nb):
    g = pl.program_id(0)
    i = pl.program_id(1)
    rows = GROUP_A * MOBA_BLOCK
    pair = 2 * MOBA_BLOCK

    @pl.when(i == 0)
    def _():
        kb_scr[...] = k_ref[...].astype(bf16)
        vb_scr[:, 0:HD_A] = v_ref[...].astype(bf16)
        vb_scr[:, HD_A:] = jnp.ones((nb * MOBA_BLOCK, HD_A), bf16)
        kmean_scr[...] = jnp.zeros_like(kmean_scr)
        for n in range(nb):
            kmean_scr[n:n + 1, :] = jnp.mean(k_ref[n * MOBA_BLOCK:(n + 1) * MOBA_BLOCK, :], axis=0, keepdims=True)
        row = lax.broadcasted_iota(jnp.int32, (rows, pair), 0)
        col = lax.broadcasted_iota(jnp.int32, (rows, pair), 1)
        second = row >= MOBA_BLOCK
        gf = jnp.zeros((rows, pair), f32) + g.astype(f32)
        slope = jnp.exp2(-(2.0 * gf + 1.0) - jnp.where(second, 1.0, 0.0)) * LOG2E
        rin = row - jnp.where(second, MOBA_BLOCK, 0)
        bias_scr[...] = slope * (col - rin).astype(f32)
        off_scr[...] = slope[:, :LANE] * float(MOBA_BLOCK)

    q = q_ref[...]
    q2 = jnp.concatenate([q[:, :HD_A], q[:, HD_A:]], axis=0)
    lane_f = lax.broadcasted_iota(jnp.int32, (rows, LANE), 1).astype(f32)
    scores = lax.dot_general(q2, kmean_scr[...], NT, precision=HIGHEST, preferred_element_type=f32)
    i_f = jnp.zeros((rows, LANE), f32) + i.astype(f32)
    sel = _topk_select(scores, i_f, lane_f, min(MOBA_TOPK, nb))
    selneg = jnp.where(sel > 0.0, 0.0, NEG)
    lhs = jnp.concatenate([q2.astype(bf16), selneg.astype(bf16)], axis=1)

    acc_scr[...] = jnp.zeros_like(acc_scr)
    m_scr[...] = jnp.full_like(m_scr, -jnp.inf)

    def attend(n, width, aux, dist, causal):
        off = pl.multiple_of(n * MOBA_BLOCK, MOBA_BLOCK)
        rhs = jnp.concatenate([kb_scr[pl.ds(off, width), :], aux], axis=1)
        vblk = vb_scr[pl.ds(off, width), :]
        for half in range(GROUP_A):
            rs = slice(half * MOBA_BLOCK, (half + 1) * MOBA_BLOCK)
            t = lax.dot_general(lhs[rs], rhs, NT, preferred_element_type=f32) * ATT_C1 + bias_scr[rs, :width]
            if causal:
                r = lax.broadcasted_iota(jnp.int32, (MOBA_BLOCK, width), 0)
                c = lax.broadcasted_iota(jnp.int32, (MOBA_BLOCK, width), 1)
                t = jnp.where(c <= r, t, NEG)
            m_prev = m_scr[rs]
            m_cur = jnp.max(t, axis=1, keepdims=True)
            shift = None
            if dist is not None:
                shift = off_scr[rs] * dist
                m_cur = m_cur - shift
            m_new = jnp.maximum(m_prev, m_cur)
            alpha = jnp.exp2(m_prev - m_new)
            sub = m_new if shift is None else m_new + shift
            p = jnp.exp2(t - jnp.concatenate([sub] * (width // LANE), axis=1))
            pv = _dot(p.astype(bf16), vblk)
            acc_scr[rs] = acc_scr[rs] * jnp.concatenate([alpha, alpha], axis=1) + pv
            m_scr[rs] = m_new

    def dvec(d):
        return jnp.zeros((MOBA_BLOCK, LANE), f32) + d

    attend(i, MOBA_BLOCK, jnp.zeros((MOBA_BLOCK, LANE), bf16), None, True)

    lane_k = lax.broadcasted_iota(jnp.int32, (pair, LANE), 1)
    second_k = jnp.where(lax.broadcasted_iota(jnp.int32, (pair, LANE), 0) >= MOBA_BLOCK, 1, 0)

    def body(j, c):
        n = 2 * j
        aux = jnp.where(lane_k == n + second_k, 1.0, 0.0).astype(bf16)
        attend(n, pair, aux, dvec((i - n).astype(f32)), False)
        return c

    lax.fori_loop(0, lax.shift_right_logical(i, 1), body, 0)

    @pl.when(lax.rem(i, 2) == 1)
    def _():
        lane_1 = lax.broadcasted_iota(jnp.int32, (MOBA_BLOCK, LANE), 1)
        aux = jnp.where(lane_1 == i - 1, 1.0, 0.0).astype(bf16)
        attend(i - 1, MOBA_BLOCK, aux, dvec(1.0), False)

    acc = acc_scr[...]
    o = acc[:, :HD_A] / acc[:, HD_A:]
    o2 = jnp.concatenate([o[:MOBA_BLOCK], o[MOBA_BLOCK:]], axis=1)
    o_ref[...] = o2 * _silu(za_ref[...])


def _moba_prompt(proj):
    L = proj.shape[0]
    nb = L // MOBA_BLOCK
    rows = GROUP_A * MOBA_BLOCK
    wq = GROUP_A * HD_A
    return pl.pallas_call(
        functools.partial(_moba_prompt_kernel, nb=nb),
        out_shape=jax.ShapeDtypeStruct((L, Q_A), f32),
        grid=(KVH_A, nb),
        in_specs=[pl.BlockSpec((MOBA_BLOCK, wq), lambda g, i: (i, OFF_QA // wq + g)),
                  pl.BlockSpec((L, HD_A), lambda g, i: (0, OFF_KA // HD_A + g)),
                  pl.BlockSpec((L, HD_A), lambda g, i: (0, OFF_VA // HD_A + g)),
                  pl.BlockSpec((MOBA_BLOCK, wq), lambda g, i: (i, OFF_ZA // wq + g))],
        out_specs=pl.BlockSpec((MOBA_BLOCK, wq), lambda g, i: (i, g)),
        scratch_shapes=[pltpu.VMEM((L, HD_A), bf16),
                        pltpu.VMEM((L, 2 * HD_A), bf16),
                        pltpu.VMEM((LANE, HD_A), f32),
                        pltpu.VMEM((rows, 2 * MOBA_BLOCK), f32),
                        pltpu.VMEM((rows, LANE), f32),
                        pltpu.VMEM((rows, 2 * HD_A), f32),
                        pltpu.VMEM((rows, HD_A), f32)],
        compiler_params=_cparams(2, VMEM_BIG),
        name="moba_prompt",
    )(proj, proj, proj, proj)


def _moba_decode_kernel(pt_ref, q_ref, kn_ref, vn_ref, za_ref, ck_ref, cv_ref, o_ref,
                        kbuf, vbuf, sems, *, layer, n_pages, nbatch):
    b = pl.program_id(0)
    past = n_pages * PAGE_SIZE
    nbp = past // MOBA_BLOCK
    prow = PAGE_SIZE * KVH_A

    def copies(bb, slot):
        out = []
        for p in range(n_pages):
            page = pt_ref[bb * n_pages + p]
            out.append(pltpu.make_async_copy(ck_ref.at[layer, page], kbuf.at[slot, pl.ds(p * prow, prow), :],
                                             sems.at[0, slot]))
            out.append(pltpu.make_async_copy(cv_ref.at[layer, page], vbuf.at[slot, pl.ds(p * prow, prow), :],
                                             sems.at[1, slot]))
        return out

    slot = lax.rem(b, 2)

    @pl.when(b == 0)
    def _():
        for c in copies(0, 0):
            c.start()

    @pl.when(b + 1 < nbatch)
    def _():
        for c in copies(b + 1, 1 - slot):
            c.start()

    for c in copies(b, slot):
        c.wait()

    q = q_ref[0]
    kn = kn_ref[0].astype(bf16).astype(f32)
    vn = vn_ref[0].astype(bf16).astype(f32)
    lane_f = lax.broadcasted_iota(jnp.int32, (SUBLANE, LANE), 1).astype(f32)
    rrow = lax.broadcasted_iota(jnp.int32, (SUBLANE, past), 0).astype(f32)
    pos = lax.broadcasted_iota(jnp.int32, (SUBLANE, past), 1)
    dist = (past - pos).astype(f32)
    zpad = jnp.zeros((SUBLANE - GROUP_A, HD_A), f32)
    outs = []
    for g in range(KVH_A):
        hs = slice(g * HD_A, (g + 1) * HD_A)
        kf = kbuf[slot, pl.ds(g, past, stride=KVH_A), :]
        vf = vbuf[slot, pl.ds(g, past, stride=KVH_A), :]
        qg = jnp.concatenate([q[:, (GROUP_A * g + r) * HD_A:(GROUP_A * g + r + 1) * HD_A] for r in range(GROUP_A)]
                             + [zpad], axis=0)
        kmean = jnp.concatenate(
            [jnp.mean(kf[n * MOBA_BLOCK:(n + 1) * MOBA_BLOCK, :], axis=0, keepdims=True) for n in range(nbp)]
            + [jnp.zeros((LANE - nbp, HD_A), f32)], axis=0)
        scores = lax.dot_general(qg, kmean, NT, precision=HIGHEST, preferred_element_type=f32)
        sel = _topk_select(scores, float(nbp), lane_f, min(MOBA_TOPK, nbp + 1))
        qb = qg.astype(bf16)
        s = lax.dot_general(qb, kf.astype(bf16), NT, preferred_element_type=f32)
        slope = jnp.exp2(-(rrow + float(GROUP_A * g + 1)))
        s = s * ATT_SCALE - slope * dist
        selx = jnp.concatenate([jnp.broadcast_to(sel[:, n:n + 1], (SUBLANE, MOBA_BLOCK)) for n in range(nbp)], axis=1)
        s = jnp.where(selx > 0.0, s, NEG)
        s_own = jnp.sum(qb.astype(f32) * kn[:, hs], axis=1, keepdims=True) * ATT_SCALE
        m = jnp.maximum(jnp.max(s, axis=1, keepdims=True), s_own)
        p = jnp.exp(s - m)
        p_own = jnp.exp(s_own - m).astype(bf16).astype(f32)
        pb = p.astype(bf16)
        l = jnp.sum(pb.astype(f32), axis=1, keepdims=True) + p_own
        o = (_dot(pb, vf.astype(bf16)) + p_own * vn[:, hs]) / l
        outs.extend(o[r:r + 1, :] for r in range(GROUP_A))
    o_ref[0] = jnp.concatenate(outs, axis=1) * _silu(za_ref[0])


def _moba_decode(proj_s, cache_k, cache_v, page_table, layer):
    nbatch, n_pages = page_table.shape
    past = n_pages * PAGE_SIZE
    n_phys = cache_k.shape[1]
    ck = cache_k.reshape(cache_k.shape[0], n_phys, PAGE_SIZE * KVH_A, HD_A)
    cv = cache_v.reshape(cache_v.shape[0], n_phys, PAGE_SIZE * KVH_A, HD_A)
    p3 = proj_s.reshape(nbatch, 1, PW)
    grid_spec = pltpu.PrefetchScalarGridSpec(
        num_scalar_prefetch=1,
        grid=(nbatch,),
        in_specs=[pl.BlockSpec((1, 1, Q_A), lambda b, pt: (b, 0, OFF_QA // Q_A)),
                  pl.BlockSpec((1, 1, KV_A), lambda b, pt: (b, 0, OFF_KA // KV_A)),
                  pl.BlockSpec((1, 1, KV_A), lambda b, pt: (b, 0, OFF_VA // KV_A)),
                  pl.BlockSpec((1, 1, Q_A), lambda b, pt: (b, 0, OFF_ZA // Q_A)),
                  pl.BlockSpec(memory_space=pl.ANY),
                  pl.BlockSpec(memory_space=pl.ANY)],
        out_specs=pl.BlockSpec((1, 1, Q_A), lambda b, pt: (b, 0, 0)),
        scratch_shapes=[pltpu.VMEM((2, past * KVH_A, HD_A), f32),
                        pltpu.VMEM((2, past * KVH_A, HD_A), f32),
                        pltpu.SemaphoreType.DMA((2, 2))],
    )
    out = pl.pallas_call(
        functools.partial(_moba_decode_kernel, layer=layer, n_pages=n_pages, nbatch=nbatch),
        out_shape=jax.ShapeDtypeStruct((nbatch, 1, Q_A), f32),
        grid_spec=grid_spec,
        compiler_params=_cparams(1, VMEM_BIG),
        name="moba_decode",
    )(page_table.reshape(-1), p3, p3, p3, p3, ck, cv)
    return out.reshape(nbatch, Q_A)


GDN_T = 256
GDN_HB = 2


def _gdn_prompt_kernel(alog_ref, dtb_ref, q_ref, k_ref, v_ref, ab_ref, zb_ref, wq_ref, wk_ref, wv_ref, ng_ref,
                       o_ref, s_out_ref, s_scr, prev_scr, vnew_scr, *, T):
    hp = pl.program_id(0)
    t = pl.program_id(1)
    C = min(DN_CHUNK, T)
    n_chunks = T // C
    shift = int(math.log2(C))

    @pl.when(t == 0)
    def _():
        s_scr[...] = jnp.zeros_like(s_scr)
        prev_scr[...] = jnp.zeros_like(prev_scr)
        vnew_scr[...] = jnp.zeros_like(vnew_scr)

    row = lax.broadcasted_iota(jnp.int32, (T, T), 0)
    col = lax.broadcasted_iota(jnp.int32, (T, T), 1)
    same = lax.shift_right_logical(row, shift) == lax.shift_right_logical(col, shift)
    tri_incl = jnp.logical_and(same, row >= col)
    tri_strict = jnp.logical_and(same, row > col)
    ltri = jnp.where(tri_incl, 1.0, 0.0)
    ab = ab_ref[...]
    lane = lax.broadcasted_iota(jnp.int32, (T, LANE), 1)
    r8 = lax.broadcasted_iota(jnp.int32, (SUBLANE, LANE), 0)
    colT = lax.broadcasted_iota(jnp.int32, (DK_B, T), 1)

    for hb in range(GDN_HB):
        h = hp * GDN_HB + hb
        sl = slice(hb * LANE, (hb + 1) * LANE)

        def conv(x_ref, w_ref, pidx):
            x = x_ref[:, sl]
            w = w_ref[:, sl]
            prev8 = prev_scr[pidx]
            acc = x * w[CONV_W - 1:CONV_W, :]
            for s in range(1, CONV_W):
                xs = pltpu.roll(x, s, axis=0)
                top = jnp.where(r8 < s, pltpu.roll(prev8, s, axis=0), xs[:SUBLANE])
                xs = jnp.concatenate([top, xs[SUBLANE:]], axis=0)
                acc = acc + xs * w[CONV_W - 1 - s:CONV_W - s, :]
            prev_scr[pidx] = x[T - SUBLANE:, :]
            return _silu(acc)

        qc = conv(q_ref, wq_ref, 3 * hb)
        kc = conv(k_ref, wk_ref, 3 * hb + 1)
        vc = conv(v_ref, wv_ref, 3 * hb + 2)
        qn = qc * lax.rsqrt(jnp.sum(qc * qc, axis=-1, keepdims=True) + EPS) * (DK_B ** -0.5)
        kn = kc * lax.rsqrt(jnp.sum(kc * kc, axis=-1, keepdims=True) + EPS)
        a = jnp.sum(jnp.where(lane == h, ab, 0.0), axis=-1, keepdims=True)
        b = jnp.sum(jnp.where(lane == H_B + h, ab, 0.0), axis=-1, keepdims=True)
        beta = _sigmoid(b)
        zcol = jnp.zeros((T, 1), f32)
        glog = -jnp.exp(zcol + alog_ref[h]) * _softplus(a + dtb_ref[h])
        g_rep = jnp.broadcast_to(glog, (T, LANE))
        gam = jnp.dot(ltri, g_rep, precision=HIGHEST, preferred_element_type=f32)
        gam_row = gam.T
        reps = T // LANE
        gam_i = jnp.concatenate([gam] * reps, axis=1) if reps > 1 else gam[:, :T]
        gam_j = jnp.concatenate([gam_row] * reps, axis=0) if reps > 1 else gam_row[:T]
        decay = jnp.exp(jnp.where(tri_incl, gam_i - gam_j, -jnp.inf))
        kb = kn.astype(bf16)
        kk = lax.dot_general(kb, kb, NT, preferred_element_type=f32)
        beta_i = jnp.broadcast_to(beta, (T, T))
        npow = jnp.where(tri_strict, -(beta_i * kk * decay), 0.0)
        x_acc = npow
        for _ in range(shift - 1):
            npb = npow.astype(bf16)
            npow = _dot(npb, npb)
            x_acc = x_acc + npow + _dot(npow.astype(bf16), x_acc.astype(bf16))
        egam = jnp.exp(gam)
        rhs = jnp.concatenate([vc * beta, kn * (beta * egam)], axis=1)
        sol = rhs + _dot(x_acc.astype(bf16), rhs.astype(bf16))
        u_val = sol[:, :DV_B]
        k_cum = sol[:, DV_B:]
        qk = lax.dot_general(qn.astype(bf16), kb, NT, preferred_element_type=f32) * decay
        qkb = qk.astype(bf16)
        q_dec = (qn * egam).astype(bf16)
        k_cum_b = k_cum.astype(bf16)
        knT = kn.T
        outs = []
        for c in range(n_chunks):
            rs = slice(c * C, (c + 1) * C)
            last = (c + 1) * C - 1
            s_old = s_scr[hb]
            s_b = s_old.astype(bf16)
            v_new = u_val[rs] - _dot(k_cum_b[rs], s_b)
            vnew_scr[hb, rs, :] = v_new
            vn_b = vnew_scr[hb].astype(bf16)
            outs.append(_dot(q_dec[rs], s_b) + _dot(qkb[rs, :], vn_b))
            in_chunk = jnp.logical_and(colT >= c * C, colT <= last)
            w_end = jnp.exp(jnp.where(in_chunk, gam_row[:, last:last + 1] - gam_row[:, :T], -jnp.inf))
            k_end_t = (knT * w_end).astype(bf16)
            s_scr[hb] = s_old * jnp.exp(gam[last:last + 1, :]) + _dot(k_end_t, vn_b)
        o = jnp.concatenate(outs, axis=0)
        o_ref[:, sl] = _rms(o, ng_ref[...]) * _silu(zb_ref[:, sl])
        s_out_ref[hb] = s_scr[hb]


def _gdn_prompt(proj, conv_w, a_log, dt_bias, norm_g):
    L = proj.shape[0]
    T = min(GDN_T, L)
    wb = GDN_HB * LANE
    nq = H_B * DK_B // wb
    return pl.pallas_call(
        functools.partial(_gdn_prompt_kernel, T=T),
        out_shape=(jax.ShapeDtypeStruct((L, V_B), f32), jax.ShapeDtypeStruct((H_B, DK_B, DV_B), f32)),
        grid=(H_B // GDN_HB, L // T),
        in_specs=[pl.BlockSpec(memory_space=pltpu.SMEM),
                  pl.BlockSpec(memory_space=pltpu.SMEM),
                  pl.BlockSpec((T, wb), lambda hp, t: (t, OFF_CONV // wb + hp)),
                  pl.BlockSpec((T, wb), lambda hp, t: (t, OFF_CONV // wb + nq + hp)),
                  pl.BlockSpec((T, wb), lambda hp, t: (t, OFF_CONV // wb + 2 * nq + hp)),
                  pl.BlockSpec((T, LANE), lambda hp, t: (t, OFF_AB // LANE)),
                  pl.BlockSpec((T, wb), lambda hp, t: (t, OFF_ZB // wb + hp)),
                  pl.BlockSpec((CONV_W, wb), lambda hp, t: (0, hp)),
                  pl.BlockSpec((CONV_W, wb), lambda hp, t: (0, nq + hp)),
                  pl.BlockSpec((CONV_W, wb), lambda hp, t: (0, 2 * nq + hp)),
                  pl.BlockSpec((1, DV_B), lambda hp, t: (0, 0))],
        out_specs=(pl.BlockSpec((T, wb), lambda hp, t: (t, hp)),
                   pl.BlockSpec((GDN_HB, DK_B, DV_B), lambda hp, t: (hp, 0, 0))),
        scratch_shapes=[pltpu.VMEM((GDN_HB, DK_B, DV_B), f32),
                        pltpu.VMEM((3 * GDN_HB, SUBLANE, LANE), f32),
                        pltpu.VMEM((GDN_HB, T, DV_B), f32)],
        compiler_params=_cparams(2, VMEM_BIG),
        name="gdn_prompt",
    )(a_log, dt_bias, proj, proj, proj, proj, proj, conv_w, conv_w, conv_w, norm_g.reshape(1, DV_B))


GDN_BB = 8


def _gdn_decode_kernel(x_ref, ab_ref, zb_ref, cs_ref, sd_ref, w_ref, alog_ref, dtb_ref, ng_ref,
                       o_ref, cs_out_ref, sd_out_ref):
    x = x_ref[...]
    w = w_ref[...]
    c0, c1, c2 = cs_ref[:, 0, :], cs_ref[:, 1, :], cs_ref[:, 2, :]
    y = c0 * w[0:1, :] + c1 * w[1:2, :] + c2 * w[2:3, :] + x * w[3:4, :]
    cs_out_ref[:, 0, :] = c1
    cs_out_ref[:, 1, :] = c2
    cs_out_ref[:, 2, :] = x
    y = _silu(y)
    ab = ab_ref[...]
    a = ab[:, 0:H_B]
    beta = _sigmoid(ab[:, H_B:2 * H_B])
    g = -jnp.exp(alog_ref[...]) * _softplus(a + dtb_ref[...])
    eg = jnp.exp(g)
    nq = H_B * DK_B
    outs = []
    for h in range(H_B):
        qc = y[:, h * DK_B:(h + 1) * DK_B]
        kc = y[:, nq + h * DK_B:nq + (h + 1) * DK_B]
        vc = y[:, 2 * nq + h * DV_B:2 * nq + (h + 1) * DV_B]
        qn = qc * lax.rsqrt(jnp.sum(qc * qc, axis=-1, keepdims=True) + EPS) * (DK_B ** -0.5)
        kn = kc * lax.rsqrt(jnp.sum(kc * kc, axis=-1, keepdims=True) + EPS)
        qk = jnp.sum(qn * kn, axis=-1, keepdims=True)
        knT = kn.T
        qnT = qn.T
        rows = []
        for bb in range(GDN_BB):
            s_old = sd_ref[bb, h]
            kcol = knT[:, bb:bb + 1]
            qcol = qnT[:, bb:bb + 1]
            be = beta[bb:bb + 1, h:h + 1]
            e = eg[bb:bb + 1, h:h + 1]
            ks = jnp.sum(kcol * s_old, axis=0, keepdims=True)
            qs = jnp.sum(qcol * s_old, axis=0, keepdims=True)
            v_new = be * vc[bb:bb + 1, :] - (be * e) * ks
            rows.append(e * qs + qk[bb:bb + 1, :] * v_new)
            sd_out_ref[bb, h] = s_old * e + kcol * v_new
        o = jnp.concatenate(rows, axis=0)
        outs.append(_rms(o, ng_ref[...]) * _silu(zb_ref[:, h * DV_B:(h + 1) * DV_B]))
    o_ref[...] = jnp.concatenate(outs, axis=1)


def _gdn_decode(proj_s, state_conv, state_delta, conv_w, a_log, dt_bias, norm_g, layer):
    nbatch = proj_s.shape[0]
    bb = GDN_BB
    return pl.pallas_call(
        _gdn_decode_kernel,
        out_shape=(jax.ShapeDtypeStruct((nbatch, V_B), f32),
                   jax.ShapeDtypeStruct((nbatch, CONV_W - 1, CONV_CH), f32),
                   jax.ShapeDtypeStruct((nbatch, H_B, DK_B, DV_B), f32)),
        grid=(nbatch // bb,),
        in_specs=[pl.BlockSpec((bb, CONV_CH), lambda i: (i, OFF_CONV // CONV_CH)),
                  pl.BlockSpec((bb, LANE), lambda i: (i, OFF_AB // LANE)),
                  pl.BlockSpec((bb, V_B), lambda i: (i, OFF_ZB // V_B)),
                  pl.BlockSpec((None, bb, CONV_W - 1, CONV_CH), lambda i: (layer, i, 0, 0)),
                  pl.BlockSpec((None, bb, H_B, DK_B, DV_B), lambda i: (layer, i, 0, 0, 0)),
                  pl.BlockSpec((CONV_W, CONV_CH), lambda i: (0, 0)),
                  pl.BlockSpec((1, H_B), lambda i: (0, 0)),
                  pl.BlockSpec((1, H_B), lambda i: (0, 0)),
                  pl.BlockSpec((1, DV_B), lambda i: (0, 0))],
        out_specs=(pl.BlockSpec((bb, V_B), lambda i: (i, 0)),
                   pl.BlockSpec((bb, CONV_W - 1, CONV_CH), lambda i: (i, 0, 0)),
                   pl.BlockSpec((bb, H_B, DK_B, DV_B), lambda i: (i, 0, 0, 0))),
        compiler_params=_cparams(1, VMEM_BIG),
        name="gdn_decode",
    )(proj_s, proj_s, proj_s, state_conv, state_delta, conv_w,
      a_log.reshape(1, H_B), dt_bias.reshape(1, H_B), norm_g.reshape(1, DV_B))


S5_JB = W_C // LANE
S5_GPB = LANE // S5_GROUP_CH
S5_SW = S5_GPB * S5_STATE
S5_TT = 512
S5_NSEG = SUBLANE


def _s5_param_kernel(lr_ref, li_ref, ldt_ref, bre_ref, bim_ref, are_ref, aim_ref, bbre_ref, bbim_ref):
    lr = lr_ref[...]
    li = li_ref[...]
    dt = jnp.exp(ldt_ref[...])
    mag = jnp.exp(lr * dt)
    ang = li * dt
    a_re = mag * jnp.cos(ang)
    a_im = mag * jnp.sin(ang)
    den = lr * lr + li * li
    f_re = ((a_re - 1.0) * lr + a_im * li) / den
    f_im = (a_im * lr - (a_re - 1.0) * li) / den
    are_ref[...] = a_re
    aim_ref[...] = a_im
    fr = f_re[:, None, :]
    fi = f_im[:, None, :]
    bre = bre_ref[...]
    bim = bim_ref[...]
    bbre_ref[...] = fr * bre - fi * bim
    bbim_ref[...] = fr * bim + fi * bre


def _s5_params(lam_re, lam_im, log_dt, b_re, b_im, c_re, c_im, d):
    G, P, CH = S5_GROUPS, S5_STATE, S5_GROUP_CH
    bt_re = jnp.swapaxes(b_re, 1, 2)
    bt_im = jnp.swapaxes(b_im, 1, 2)
    a_re, a_im, bb_re, bb_im = pl.pallas_call(
        _s5_param_kernel,
        out_shape=(jax.ShapeDtypeStruct((G, P), f32), jax.ShapeDtypeStruct((G, P), f32),
                   jax.ShapeDtypeStruct((G, CH, P), f32), jax.ShapeDtypeStruct((G, CH, P), f32)),
        name="s5_params",
    )(lam_re, lam_im, log_dt.reshape(G, 1), bt_re, bt_im)
    eye = jnp.eye(S5_GPB, dtype=f32)

    def bdiag_in(bb):
        x = bb.reshape(S5_JB, S5_GPB, CH, P)
        return jnp.einsum("jacp,ab->jacbp", x, eye).reshape(S5_JB, LANE, S5_SW)

    def bdiag_out(c):
        x = c.reshape(S5_JB, S5_GPB, CH, P)
        return jnp.einsum("jacp,ab->jbpac", x, eye).reshape(S5_JB, S5_SW, LANE)

    wb = jnp.concatenate([bdiag_in(bb_re), bdiag_in(bb_im)], axis=-1).astype(bf16)
    wc_re = bdiag_out(c_re).astype(bf16)
    wc_im = bdiag_out(c_im).astype(bf16)
    return dict(wb=wb, wc_re=wc_re, wc_im=wc_im,
                a_re=a_re.reshape(S5_JB, 1, S5_SW), a_im=a_im.reshape(S5_JB, 1, S5_SW),
                d=d.reshape(S5_JB, 1, LANE))


def _s5_prompt_kernel(u_ref, wb_ref, wcre_ref, wcim_ref, are_ref, aim_ref, d_ref,
                      y_ref, hre_ref, him_ref, a_scr, b_scr, carry_scr, *, TT):
    t = pl.program_id(1)
    nlb = S5_SW // LANE
    seg = TT // S5_NSEG
    pitch = seg + SUBLANE

    @pl.when(t == 0)
    def _():
        carry_scr[...] = jnp.zeros_like(carry_scr)

    u = u_ref[...]
    zpad = jnp.zeros((SUBLANE, LANE), f32)
    u_pad = jnp.concatenate([x for s in range(S5_NSEG) for x in (u[s * seg:(s + 1) * seg], zpad)], axis=0)
    bu = _dot(u_pad.astype(bf16), wb_ref[...])
    for c in range(2 * nlb):
        a_scr[c] = bu[:, c * LANE:(c + 1) * LANE]

    are = are_ref[...]
    aim = aim_ref[...]
    ar8 = [jnp.broadcast_to(are[:, c * LANE:(c + 1) * LANE], (S5_NSEG, LANE)) for c in range(nlb)]
    ai8 = [jnp.broadcast_to(aim[:, c * LANE:(c + 1) * LANE], (S5_NSEG, LANE)) for c in range(nlb)]

    def seg_rows(k):
        return pl.ds(k, S5_NSEG, stride=pitch)

    zero8 = jnp.zeros((S5_NSEG, LANE), f32)
    hr = [zero8] * nlb
    hi = [zero8] * nlb
    for k in range(seg):
        for c in range(nlb):
            nr = ar8[c] * hr[c] - ai8[c] * hi[c] + a_scr[c, seg_rows(k), :]
            ni = ar8[c] * hi[c] + ai8[c] * hr[c] + a_scr[nlb + c, seg_rows(k), :]
            b_scr[c, seg_rows(k), :] = nr
            b_scr[nlb + c, seg_rows(k), :] = ni
            hr[c], hi[c] = nr, ni
    ends = hr + hi

    pr, pi = are, aim
    for _ in range(int(math.log2(seg))):
        pr, pi = pr * pr - pi * pi, 2.0 * pr * pi

    carry = carry_scr[...]
    er, ei = carry[:, :S5_SW], carry[:, S5_SW:]
    rows_r, rows_i = [], []
    for s in range(S5_NSEG):
        rows_r.append(er)
        rows_i.append(ei)
        loc_r = jnp.concatenate([ends[c][s:s + 1, :] for c in range(nlb)], axis=1)
        loc_i = jnp.concatenate([ends[nlb + c][s:s + 1, :] for c in range(nlb)], axis=1)
        er, ei = pr * er - pi * ei + loc_r, pr * ei + pi * er + loc_i
    carry_scr[...] = jnp.concatenate([er, ei], axis=1)
    hre_ref[0] = er
    him_ref[0] = ei
    ein_r = jnp.concatenate(rows_r, axis=0)
    ein_i = jnp.concatenate(rows_i, axis=0)

    pw_r = list(ar8)
    pw_i = list(ai8)
    for k in range(seg):
        for c in range(nlb):
            e_r = ein_r[:, c * LANE:(c + 1) * LANE]
            e_i = ein_i[:, c * LANE:(c + 1) * LANE]
            a_scr[c, seg_rows(k), :] = b_scr[c, seg_rows(k), :] + (pw_r[c] * e_r - pw_i[c] * e_i)
            a_scr[nlb + c, seg_rows(k), :] = b_scr[nlb + c, seg_rows(k), :] + (pw_r[c] * e_i + pw_i[c] * e_r)
            if k + 1 < seg:
                pw_r[c], pw_i[c] = pw_r[c] * ar8[c] - pw_i[c] * ai8[c], pw_r[c] * ai8[c] + pw_i[c] * ar8[c]

    h_re = jnp.concatenate([a_scr[c] for c in range(nlb)], axis=1).astype(bf16)
    h_im = jnp.concatenate([a_scr[nlb + c] for c in range(nlb)], axis=1).astype(bf16)
    y_pad = _dot(h_re, wcre_ref[...]) - _dot(h_im, wcim_ref[...])
    y = jnp.concatenate([y_pad[s * pitch:s * pitch + seg] for s in range(S5_NSEG)], axis=0)
    y_ref[...] = y + d_ref[...] * u


def _s5_prompt(proj, sp):
    L = proj.shape[0]
    TT = min(S5_TT, L)
    jb_spec = lambda shape: pl.BlockSpec((None,) + shape, lambda j, t: (j, 0, 0))
    y, hre, him = pl.pallas_call(
        functools.partial(_s5_prompt_kernel, TT=TT),
        out_shape=(jax.ShapeDtypeStruct((L, W_C), f32),
                   jax.ShapeDtypeStruct((S5_JB, 1, S5_SW), f32),
                   jax.ShapeDtypeStruct((S5_JB, 1, S5_SW), f32)),
        grid=(S5_JB, L // TT),
        in_specs=[pl.BlockSpec((TT, LANE), lambda j, t: (t, OFF_UC // LANE + j)),
                  jb_spec((LANE, 2 * S5_SW)), jb_spec((S5_SW, LANE)), jb_spec((S5_SW, LANE)),
                  jb_spec((1, S5_SW)), jb_spec((1, S5_SW)), jb_spec((1, LANE))],
        out_specs=(pl.BlockSpec((TT, LANE), lambda j, t: (t, j)),
                   pl.BlockSpec((1, 1, S5_SW), lambda j, t: (j, 0, 0)),
                   pl.BlockSpec((1, 1, S5_SW), lambda j, t: (j, 0, 0))),
        scratch_shapes=[pltpu.VMEM((2 * S5_SW // LANE, TT + S5_NSEG * SUBLANE, LANE), f32),
                        pltpu.VMEM((2 * S5_SW // LANE, TT + S5_NSEG * SUBLANE, LANE), f32),
                        pltpu.VMEM((1, 2 * S5_SW), f32)],
        compiler_params=_cparams(2, VMEM_BIG),
        name="s5_prompt",
    )(proj, sp["wb"], sp["wc_re"], sp["wc_im"], sp["a_re"], sp["a_im"], sp["d"])
    return y, hre.reshape(S5_GROUPS, S5_STATE), him.reshape(S5_GROUPS, S5_STATE)


def _s5_decode_kernel(u_ref, h0r_ref, h0i_ref, wb_ref, wcre_ref, wcim_ref, are_ref, aim_ref, d_ref,
                      y_ref, hr_ref, hi_ref):
    u = u_ref[...]
    bu = _dot(u.astype(bf16), wb_ref[...])
    are, aim = are_ref[...], aim_ref[...]
    h0r, h0i = h0r_ref[...], h0i_ref[...]
    hr = bu[:, :S5_SW] + are * h0r - aim * h0i
    hi = bu[:, S5_SW:] + are * h0i + aim * h0r
    hr_ref[...] = hr
    hi_ref[...] = hi
    y_ref[...] = _dot(hr.astype(bf16), wcre_ref[...]) - _dot(hi.astype(bf16), wcim_ref[...]) + d_ref[...] * u


def _s5_decode(proj_s, st_re, st_im, sp, layer):
    nbatch = proj_s.shape[0]
    sre = st_re.reshape(st_re.shape[0], nbatch, S5_GROUPS * S5_STATE)
    sim = st_im.reshape(st_im.shape[0], nbatch, S5_GROUPS * S5_STATE)
    jb_spec = lambda shape: pl.BlockSpec((None,) + shape, lambda j: (j, 0, 0))
    st_spec = pl.BlockSpec((None, nbatch, S5_SW), lambda j: (layer, 0, j))
    y, hr, hi = pl.pallas_call(
        _s5_decode_kernel,
        out_shape=(jax.ShapeDtypeStruct((nbatch, W_C), f32),
                   jax.ShapeDtypeStruct((nbatch, S5_GROUPS * S5_STATE), f32),
                   jax.ShapeDtypeStruct((nbatch, S5_GROUPS * S5_STATE), f32)),
        grid=(S5_JB,),
        in_specs=[pl.BlockSpec((nbatch, LANE), lambda j: (0, OFF_UC // LANE + j)),
                  st_spec, st_spec,
                  jb_spec((LANE, 2 * S5_SW)), jb_spec((S5_SW, LANE)), jb_spec((S5_SW, LANE)),
                  jb_spec((1, S5_SW)), jb_spec((1, S5_SW)), jb_spec((1, LANE))],
        out_specs=(pl.BlockSpec((nbatch, LANE), lambda j: (0, j)),
                   pl.BlockSpec((nbatch, S5_SW), lambda j: (0, j)),
                   pl.BlockSpec((nbatch, S5_SW), lambda j: (0, j))),
        compiler_params=_cparams(1),
        name="s5_decode",
    )(proj_s, sre, sim, sp["wb"], sp["wc_re"], sp["wc_im"], sp["a_re"], sp["a_im"], sp["d"])
    return (y, hr.reshape(nbatch, S5_GROUPS, S5_STATE), hi.reshape(nbatch, S5_GROUPS, S5_STATE))


def _s5_glu_kernel(y_ref, zc_ref, w_ref, b_ref, o_ref):
    y = y_ref[...]
    ge = 0.5 * y * (1.0 + lax.erf(y * (0.5 ** 0.5)))
    gl = _dot(ge.astype(bf16), w_ref[...]) + b_ref[...]
    o_ref[...] = ge * _sigmoid(gl) * _silu(zc_ref[...])


def _s5_glu(y, proj, w_glu, b_glu, tm):
    m = y.shape[0]
    return pl.pallas_call(
        _s5_glu_kernel,
        out_shape=jax.ShapeDtypeStruct((m, W_C), f32),
        grid=(m // tm,),
        in_specs=[pl.BlockSpec((tm, W_C), lambda i: (i, 0)),
                  pl.BlockSpec((tm, W_C), lambda i: (i, OFF_ZC // W_C)),
                  pl.BlockSpec((W_C, W_C), lambda i: (0, 0)),
                  pl.BlockSpec((1, W_C), lambda i: (0, 0))],
        out_specs=pl.BlockSpec((tm, W_C), lambda i: (i, 0)),
        compiler_params=_cparams(1, VMEM_BIG),
        name="s5_glu",
    )(y, proj, w_glu, b_glu.reshape(1, W_C))


def _merge_kernel(oa_ref, ob_ref, oc_ref, ga_ref, gb_ref, gc_ref, wb_ref, o_ref):
    acc = _sigmoid(ga_ref[...]) * _dot(oa_ref[...].astype(bf16), wb_ref[0])
    acc = acc + _sigmoid(gb_ref[...]) * _dot(ob_ref[...].astype(bf16), wb_ref[1])
    acc = acc + _sigmoid(gc_ref[...]) * _dot(oc_ref[...].astype(bf16), wb_ref[2])
    o_ref[...] = acc.astype(bf16)


def _merge(out_a, out_b, out_c, proj, w_branch, tm):
    m = out_a.shape[0]
    act = pl.BlockSpec((tm, Q_A), lambda i: (i, 0))
    gate = lambda k: pl.BlockSpec((tm, D_MODEL), lambda i: (i, OFF_G // D_MODEL + k))
    return pl.pallas_call(
        _merge_kernel,
        out_shape=jax.ShapeDtypeStruct((m, D_MODEL), bf16),
        grid=(m // tm,),
        in_specs=[act, act, act, gate(0), gate(1), gate(2),
                  pl.BlockSpec((3, Q_A, D_MODEL), lambda i: (0, 0, 0), pipeline_mode=pl.Buffered(1))],
        out_specs=pl.BlockSpec((tm, D_MODEL), lambda i: (i, 0)),
        compiler_params=_cparams(1, VMEM_BIG),
        name="merge",
    )(out_a, out_b, out_c, proj, proj, proj, w_branch)


def _post_kernel(x_ref, mg_ref, pe_ref, wout_ref, plew_ref, pleg_ref, png_ref, fng_ref, o_ref, *, final):
    x2 = x_ref[...] + _dot(mg_ref[...], wout_ref[...])
    gate = _sigmoid(_dot(_rms(x2, png_ref[...]).astype(bf16), pleg_ref[...]))
    x3 = x2 + _dot(pe_ref[...].astype(bf16), plew_ref[...]) * gate
    if final:
        x3 = _rms(x3, fng_ref[...])
    o_ref[...] = x3


def _post(x2d, merged, pe, w_out, ple_w, ple_w_gate, ple_norm_g, final_norm_g, final, tm):
    m = x2d.shape[0]
    const = lambda shape: pl.BlockSpec(shape, lambda i: (0, 0), pipeline_mode=pl.Buffered(1))
    return pl.pallas_call(
        functools.partial(_post_kernel, final=final),
        out_shape=jax.ShapeDtypeStruct((m, D_MODEL), f32),
        grid=(m // tm,),
        in_specs=[pl.BlockSpec((tm, D_MODEL), lambda i: (i, 0)),
                  pl.BlockSpec((tm, D_MODEL), lambda i: (i, 0)),
                  pl.BlockSpec((tm, PLE_DIM), lambda i: (i, 0)),
                  const((D_MODEL, D_MODEL)), const((PLE_DIM, D_MODEL)), const((D_MODEL, D_MODEL)),
                  const((1, D_MODEL)), const((1, D_MODEL))],
        out_specs=pl.BlockSpec((tm, D_MODEL), lambda i: (i, 0)),
        compiler_params=_cparams(1, VMEM_BIG),
        name="post",
    )(x2d, merged, pe, w_out, ple_w, ple_w_gate, ple_norm_g.reshape(1, D_MODEL), final_norm_g.reshape(1, D_MODEL))


def _pad_w_in(w_in):
    o = [0, Q_A, Q_A + KV_A, Q_A + 2 * KV_A, 2 * Q_A + 2 * KV_A]
    c_end = o[4] + CONV_CH
    ab_end = c_end + 2 * H_B
    z_end = ab_end + V_B + 2 * W_C
    pad = jnp.zeros((w_in.shape[0], AB_W - 2 * H_B), w_in.dtype)
    return jnp.concatenate([w_in[:, z_end:], w_in[:, :c_end], w_in[:, ab_end:z_end], w_in[:, c_end:ab_end], pad],
                           axis=1).astype(bf16)


def _row_tile(m, pref):
    return pref if m % pref == 0 else m


def _layer_weights(i, norm_g, w_in, dn_conv_w, dn_a_log, dn_dt_bias, dn_norm_g, s5_lambda_re, s5_lambda_im,
                   s5_b_re, s5_b_im, s5_c_re, s5_c_im, s5_d, s5_log_dt, s5_w_glu, s5_b_glu, w_branch, w_out,
                   ple_w, ple_norm_g, ple_w_gate):
    return dict(
        norm_g=norm_g[i], w_pad=_pad_w_in(w_in[i]), conv_w=dn_conv_w[i], a_log=dn_a_log[i], dt_bias=dn_dt_bias[i],
        dn_norm_g=dn_norm_g[i],
        s5=_s5_params(s5_lambda_re[i], s5_lambda_im[i], s5_log_dt[i], s5_b_re[i], s5_b_im[i], s5_c_re[i],
                      s5_c_im[i], s5_d[i]),
        w_glu=s5_w_glu[i].astype(bf16), b_glu=s5_b_glu[i], w_branch=w_branch[i].astype(bf16),
        w_out=w_out[i].astype(bf16), ple_w=ple_w[i].astype(bf16), ple_norm_g=ple_norm_g[i],
        ple_w_gate=ple_w_gate[i].astype(bf16))


def _finish(x2d, proj, out_a, out_b, y_c, pe, lw, final_norm_g, final):
    m = x2d.shape[0]
    out_c = _s5_glu(y_c, proj, lw["w_glu"], lw["b_glu"], _row_tile(m, 512))
    merged = _merge(out_a, out_b, out_c, proj, lw["w_branch"], _row_tile(m, 256))
    return _post(x2d, merged, pe, lw["w_out"], lw["ple_w"], lw["ple_w_gate"], lw["ple_norm_g"], final_norm_g,
                 final, _row_tile(m, 256))


def _prompt_layer(x2d, pe, lw, final_norm_g, final):
    L = x2d.shape[0]
    proj = _inproj(x2d, lw["norm_g"], lw["w_pad"], _row_tile(L, 1024))
    out_a = _moba_prompt(proj)
    out_b, s_new = _gdn_prompt(proj, lw["conv_w"], lw["a_log"], lw["dt_bias"], lw["dn_norm_g"])
    y_c, hre, him = _s5_prompt(proj, lw["s5"])
    x_new = _finish(x2d, proj, out_a, out_b, y_c, pe, lw, final_norm_g, final)
    ka = proj[:, OFF_KA:OFF_KA + KV_A].reshape(1, L, KVH_A, HD_A)
    va = proj[:, OFF_VA:OFF_VA + KV_A].reshape(1, L, KVH_A, HD_A)
    conv_new = proj[L - (CONV_W - 1):, OFF_CONV:OFF_CONV + CONV_CH][None]
    return x_new, (ka, va, conv_new, s_new[None], hre[None], him[None])


def _sample_layer(x2d, pe, lw, final_norm_g, final, layer, cache_k, cache_v, page_table, state_conv, state_delta,
                  state_s5_re, state_s5_im):
    nbatch = x2d.shape[0]
    proj = _inproj(x2d, lw["norm_g"], lw["w_pad"], nbatch)
    out_a = _moba_decode(proj, cache_k, cache_v, page_table, layer)
    out_b, conv_new, s_new = _gdn_decode(proj, state_conv, state_delta, lw["conv_w"], lw["a_log"], lw["dt_bias"],
                                         lw["dn_norm_g"], layer)
    y_c, hr, hi = _s5_decode(proj, state_s5_re, state_s5_im, lw["s5"], layer)
    x_new = _finish(x2d, proj, out_a, out_b, y_c, pe, lw, final_norm_g, final)
    ka = proj[:, OFF_KA:OFF_KA + KV_A].reshape(nbatch, 1, KVH_A, HD_A)
    va = proj[:, OFF_VA:OFF_VA + KV_A].reshape(nbatch, 1, KVH_A, HD_A)
    return x_new, (ka, va, conv_new, s_new, hr, hi)


def kernel(x_prompt, x_sample, cache_k, cache_v, page_table, state_conv, state_delta, state_s5_re, state_s5_im,
           p_prompt, p_sample, norm_g, w_in, dn_conv_w, dn_a_log, dn_dt_bias, dn_norm_g, s5_lambda_re, s5_lambda_im,
           s5_b_re, s5_b_im, s5_c_re, s5_c_im, s5_d, s5_log_dt, s5_w_glu, s5_b_glu, w_branch, w_out, ple_w,
           ple_norm_g, ple_w_gate, final_norm_g):
    depth = w_in.shape[0]
    bp, L, _ = x_prompt.shape
    nbatch = x_sample.shape[0]
    assert bp == 1 and x_sample.shape[1] == 1
    xp = x_prompt.reshape(L, D_MODEL)
    xs = x_sample.reshape(nbatch, D_MODEL)
    new_p, new_s = [], []
    for i in range(depth):
        lw = _layer_weights(i, norm_g, w_in, dn_conv_w, dn_a_log, dn_dt_bias, dn_norm_g, s5_lambda_re, s5_lambda_im,
                            s5_b_re, s5_b_im, s5_c_re, s5_c_im, s5_d, s5_log_dt, s5_w_glu, s5_b_glu, w_branch,
                            w_out, ple_w, ple_norm_g, ple_w_gate)
        final = i == depth - 1
        xp, st = _prompt_layer(xp, p_prompt[i, 0], lw, final_norm_g, final)
        new_p.append(st)
        xs, st = _sample_layer(xs, p_sample[i, :, 0], lw, final_norm_g, final, i, cache_k, cache_v, page_table,
                               state_conv, state_delta, state_s5_re, state_s5_im)
        new_s.append(st)

    def stk(states, j):
        return jnp.stack([s[j] for s in states])

    return (xp.reshape(1, L, D_MODEL), xs.reshape(nbatch, 1, D_MODEL),
            stk(new_p, 0), stk(new_p, 1), stk(new_p, 2), stk(new_p, 3), stk(new_p, 4), stk(new_p, 5),
            stk(new_s, 0), stk(new_s, 1), stk(new_s, 2), stk(new_s, 3), stk(new_s, 4), stk(new_s, 5))
```

```python
import functools
import math

import jax
import jax.numpy as jnp
from jax import lax
from jax.experimental import pallas as pl
from jax.experimental.pallas import tpu as pltpu

f32 = jnp.float32
bf16 = jnp.bfloat16

D_MODEL = 2048
H_A, KVH_A, HD_A = 8, 4, 128
GROUP_A = H_A // KVH_A
MOBA_BLOCK, MOBA_TOPK = 256, 3
MOBA_RC = 256
H_B, DK_B, DV_B, CONV_W = 8, 128, 128, 4
DN_CHUNK = 64
W_C, S5_GROUP_CH, S5_GROUPS, S5_STATE = 1024, 16, 64, 64
PLE_DIM = 256
PAGE_SIZE = 128
EPS = 1e-6
Q_A = H_A * HD_A
KV_A = KVH_A * HD_A
V_B = H_B * DV_B
CONV_CH = 2 * H_B * DK_B + V_B
N_GATE = 3 * D_MODEL

OFF_G = 0
OFF_QA = OFF_G + N_GATE
OFF_KA = OFF_QA + Q_A
OFF_VA = OFF_KA + KV_A
OFF_ZA = OFF_VA + KV_A
OFF_CONV = OFF_ZA + Q_A
OFF_ZB = OFF_CONV + CONV_CH
OFF_UC = OFF_ZB + V_B
OFF_ZC = OFF_UC + W_C
OFF_AB = OFF_ZC + W_C
AB_W = 512
PW = OFF_AB + AB_W

LANE = 128
SUBLANE = 8
TN_IN = 512
NEG = -1e30
ATT_SCALE = HD_A ** -0.5
LOG2E = math.log2(math.e)
ATT_C1 = ATT_SCALE * LOG2E
VMEM_BIG = 56 * 1024 * 1024
HIGHEST = lax.Precision.HIGHEST
NT = (((1,), (1,)), ((), ()))


def _cparams(n_axes, vmem=None):
    return pltpu.CompilerParams(dimension_semantics=("arbitrary",) * n_axes, vmem_limit_bytes=vmem)


def _sigmoid(x):
    return 1.0 / (1.0 + jnp.exp(-x))


def _silu(x):
    return x * _sigmoid(x)


def _softplus(x):
    return jnp.maximum(x, 0.0) + jnp.log1p(jnp.exp(-jnp.abs(x)))


def _rms(x, g):
    return x * lax.rsqrt(jnp.mean(x * x, axis=-1, keepdims=True) + EPS) * g


def _dot(a, b):
    return jnp.dot(a, b, preferred_element_type=f32)


def _inproj_kernel(x_ref, g_ref, w_ref, o_ref, h_scr):
    @pl.when(pl.program_id(1) == 0)
    def _():
        h_scr[...] = _rms(x_ref[...], g_ref[...]).astype(bf16)

    o_ref[...] = _dot(h_scr[...], w_ref[...])


def _inproj(x2d, g, w_pad, tm):
    m = x2d.shape[0]
    return pl.pallas_call(
        _inproj_kernel,
        out_shape=jax.ShapeDtypeStruct((m, PW), f32),
        grid=(m // tm, PW // TN_IN),
        in_specs=[pl.BlockSpec((tm, D_MODEL), lambda i, n: (i, 0)),
                  pl.BlockSpec((1, D_MODEL), lambda i, n: (0, 0)),
                  pl.BlockSpec((D_MODEL, TN_IN), lambda i, n: (0, n))],
        out_specs=pl.BlockSpec((tm, TN_IN), lambda i, n: (i, n)),
        scratch_shapes=[pltpu.VMEM((tm, D_MODEL), bf16)],
        compiler_params=_cparams(2, VMEM_BIG),
        name="inproj",
    )(x2d, g.reshape(1, D_MODEL), w_pad)


def _topk_select(scores, n_valid, lane_f, n_sel):
    sc = jnp.where(lane_f < n_valid, scores, -jnp.inf)
    sel = jnp.zeros_like(sc)
    for _ in range(n_sel):
        m = jnp.max(sc, axis=-1, keepdims=True)
        idx = jnp.min(jnp.where(sc == m, lane_f, float(LANE)), axis=-1, keepdims=True)
        hit = lane_f == idx
        sel = jnp.maximum(sel, jnp.where(hit, jnp.where(m > -jnp.inf, 1.0, 0.0), 0.0))
        sc = jnp.where(hit, -jnp.inf, sc)
    return sel


def _moba_prompt_kernel(q_ref, k_ref, v_ref, za_ref, o_ref,
                        kb_scr, vb_scr, kmean_scr, bias_scr, off_scr, acc_scr, m_scr, *, nb):
    g = pl.program_id(0)
    i = pl.program_id(1)
    rows = GROUP_A * MOBA_BLOCK
    pair = 2 * MOBA_BLOCK

    @pl.when(i == 0)
    def _():
        kb_scr[...] = k_ref[...].astype(bf16)
        vb_scr[:, 0:HD_A] = v_ref[...].astype(bf16)
        vb_scr[:, HD_A:] = jnp.ones((nb * MOBA_BLOCK, HD_A), bf16)
        kmean_scr[...] = jnp.zeros_like(kmean_scr)
        for n in range(nb):
            kmean_scr[n:n + 1, :] = jnp.mean(k_ref[n * MOBA_BLOCK:(n + 1) * MOBA_BLOCK, :], axis=0, keepdims=True)
        row = lax.broadcasted_iota(jnp.int32, (rows, pair), 0)
        col = lax.broadcasted_iota(jnp.int32, (rows, pair), 1)
        second = row >= MOBA_BLOCK
        gf = jnp.zeros((rows, pair), f32) + g.astype(f32)
        slope = jnp.exp2(-(2.0 * gf + 1.0) - jnp.where(second, 1.0, 0.0)) * LOG2E
        rin = row - jnp.where(second, MOBA_BLOCK, 0)
        bias_scr[...] = slope * (col - rin).astype(f32)
        off_scr[...] = slope[:, :LANE] * float(MOBA_BLOCK)

    q = q_ref[...]
    q2 = jnp.concatenate([q[:, :HD_A], q[:, HD_A:]], axis=0)
    lane_f = lax.broadcasted_iota(jnp.int32, (rows, LANE), 1).astype(f32)
    scores = lax.dot_general(q2, kmean_scr[...], NT, precision=HIGHEST, preferred_element_type=f32)
    i_f = jnp.zeros((rows, LANE), f32) + i.astype(f32)
    sel = _topk_select(scores, i_f, lane_f, min(MOBA_TOPK, nb))
    selneg = jnp.where(sel > 0.0, 0.0, NEG)
    lhs = jnp.concatenate([q2.astype(bf16), selneg.astype(bf16)], axis=1)

    acc_scr[...] = jnp.zeros_like(acc_scr)
    m_scr[...] = jnp.full_like(m_scr, -jnp.inf)

    def attend(n, width, aux, dist, causal):
        off = pl.multiple_of(n * MOBA_BLOCK, MOBA_BLOCK)
        rhs = jnp.concatenate([kb_scr[pl.ds(off, width), :], aux], axis=1)
        vblk = vb_scr[pl.ds(off, width), :]
        rss = [slice(c * MOBA_RC, (c + 1) * MOBA_RC) for c in range(rows // MOBA_RC)]
        t = [lax.dot_general(lhs[rs], rhs, NT, preferred_element_type=f32) * ATT_C1 + bias_scr[rs, :width]
             for rs in rss]
        if causal:
            c_i = lax.broadcasted_iota(jnp.int32, (MOBA_RC, width), 1)
            r_i = lax.broadcasted_iota(jnp.int32, (MOBA_RC, width), 0)
            t = [jnp.where(c_i <= r_i + (k * MOBA_RC) % MOBA_BLOCK, t[k], NEG) for k in range(len(rss))]
        m_prev = [m_scr[rs] for rs in rss]
        m_cur = [jnp.max(x, axis=1, keepdims=True) for x in t]
        shift = None
        if dist is not None:
            shift = [off_scr[rs] * dist for rs in rss]
            m_cur = [mc - sh for mc, sh in zip(m_cur, shift)]
        m_new = [jnp.maximum(a, b) for a, b in zip(m_prev, m_cur)]
        alpha = [jnp.exp2(a - b) for a, b in zip(m_prev, m_new)]
        sub = m_new if shift is None else [a + b for a, b in zip(m_new, shift)]
        p = [jnp.exp2(x - jnp.concatenate([s] * (width // LANE), axis=1)).astype(bf16) for x, s in zip(t, sub)]
        pv = [_dot(x, vblk) for x in p]
        for k, rs in enumerate(rss):
            acc_scr[rs] = acc_scr[rs] * jnp.concatenate([alpha[k], alpha[k]], axis=1) + pv[k]
            m_scr[rs] = m_new[k]

    def dvec(d):
        return jnp.zeros((MOBA_RC, LANE), f32) + d

    attend(i, MOBA_BLOCK, jnp.zeros((MOBA_BLOCK, LANE), bf16), None, True)

    lane_k = lax.broadcasted_iota(jnp.int32, (pair, LANE), 1)
    second_k = jnp.where(lax.broadcasted_iota(jnp.int32, (pair, LANE), 0) >= MOBA_BLOCK, 1, 0)

    def body(j, c):
        n = 2 * j
        aux = jnp.where(lane_k == n + second_k, 1.0, 0.0).astype(bf16)
        attend(n, pair, aux, dvec((i - n).astype(f32)), False)
        return c

    lax.fori_loop(0, lax.shift_right_logical(i, 1), body, 0)

    @pl.when(lax.rem(i, 2) == 1)
    def _():
        lane_1 = lax.broadcasted_iota(jnp.int32, (MOBA_BLOCK, LANE), 1)
        aux = jnp.where(lane_1 == i - 1, 1.0, 0.0).astype(bf16)
        attend(i - 1, MOBA_BLOCK, aux, dvec(1.0), False)

    acc = acc_scr[...]
    o = acc[:, :HD_A] / acc[:, HD_A:]
    o2 = jnp.concatenate([o[:MOBA_BLOCK], o[MOBA_BLOCK:]], axis=1)
    o_ref[...] = o2 * _silu(za_ref[...])


def _moba_prompt(proj):
    L = proj.shape[0]
    nb = L // MOBA_BLOCK
    rows = GROUP_A * MOBA_BLOCK
    wq = GROUP_A * HD_A
    return pl.pallas_call(
        functools.partial(_moba_prompt_kernel, nb=nb),
        out_shape=jax.ShapeDtypeStruct((L, Q_A), f32),
        grid=(KVH_A, nb),
        in_specs=[pl.BlockSpec((MOBA_BLOCK, wq), lambda g, i: (i, OFF_QA // wq + g)),
                  pl.BlockSpec((L, HD_A), lambda g, i: (0, OFF_KA // HD_A + g)),
                  pl.BlockSpec((L, HD_A), lambda g, i: (0, OFF_VA // HD_A + g)),
                  pl.BlockSpec((MOBA_BLOCK, wq), lambda g, i: (i, OFF_ZA // wq + g))],
        out_specs=pl.BlockSpec((MOBA_BLOCK, wq), lambda g, i: (i, g)),
        scratch_shapes=[pltpu.VMEM((L, HD_A), bf16),
                        pltpu.VMEM((L, 2 * HD_A), bf16),
                        pltpu.VMEM((LANE, HD_A), f32),
                        pltpu.VMEM((rows, 2 * MOBA_BLOCK), f32),
                        pltpu.VMEM((rows, LANE), f32),
                        pltpu.VMEM((rows, 2 * HD_A), f32),
                        pltpu.VMEM((rows, HD_A), f32)],
        compiler_params=_cparams(2, VMEM_BIG),
        name="moba_prompt",
    )(proj, proj, proj, proj)


def _moba_decode_kernel(pt_ref, q_ref, kn_ref, vn_ref, za_ref, ck_ref, cv_ref, o_ref,
                        kbuf, vbuf, sems, *, layer, n_pages, nbatch):
    b = pl.program_id(0)
    past = n_pages * PAGE_SIZE
    nbp = past // MOBA_BLOCK
    prow = PAGE_SIZE * KVH_A

    def copies(bb, slot):
        out = []
        for p in range(n_pages):
            page = pt_ref[bb * n_pages + p]
            out.append(pltpu.make_async_copy(ck_ref.at[layer, page], kbuf.at[slot, pl.ds(p * prow, prow), :],
                                             sems.at[0, slot]))
            out.append(pltpu.make_async_copy(cv_ref.at[layer, page], vbuf.at[slot, pl.ds(p * prow, prow), :],
                                             sems.at[1, slot]))
        return out

    slot = lax.rem(b, 2)

    @pl.when(b == 0)
    def _():
        for c in copies(0, 0):
            c.start()

    @pl.when(b + 1 < nbatch)
    def _():
        for c in copies(b + 1, 1 - slot):
            c.start()

    for c in copies(b, slot):
        c.wait()

    q = q_ref[0]
    kn = kn_ref[0].astype(bf16).astype(f32)
    vn = vn_ref[0].astype(bf16).astype(f32)
    lane_f = lax.broadcasted_iota(jnp.int32, (SUBLANE, LANE), 1).astype(f32)
    rrow = lax.broadcasted_iota(jnp.int32, (SUBLANE, past), 0).astype(f32)
    pos = lax.broadcasted_iota(jnp.int32, (SUBLANE, past), 1)
    dist = (past - pos).astype(f32)
    zpad = jnp.zeros((SUBLANE - GROUP_A, HD_A), f32)
    groups = range(KVH_A)
    hs = [slice(g * HD_A, (g + 1) * HD_A) for g in groups]
    kf = [kbuf[slot, pl.ds(g, past, stride=KVH_A), :] for g in groups]
    qg = [jnp.concatenate([q[:, (GROUP_A * g + r) * HD_A:(GROUP_A * g + r + 1) * HD_A] for r in range(GROUP_A)]
                          + [zpad], axis=0) for g in groups]
    qb = [x.astype(bf16) for x in qg]
    s = [lax.dot_general(qb[g], kf[g].astype(bf16), NT, preferred_element_type=f32) for g in groups]
    kmean = [jnp.concatenate(
        [jnp.mean(kf[g][n * MOBA_BLOCK:(n + 1) * MOBA_BLOCK, :], axis=0, keepdims=True) for n in range(nbp)]
        + [jnp.zeros((LANE - nbp, HD_A), f32)], axis=0) for g in groups]
    scores = [lax.dot_general(qg[g], kmean[g], NT, precision=HIGHEST, preferred_element_type=f32) for g in groups]
    sel = [_topk_select(scores[g], float(nbp), lane_f, min(MOBA_TOPK, nbp + 1)) for g in groups]
    vb = [vbuf[slot, pl.ds(g, past, stride=KVH_A), :].astype(bf16) for g in groups]
    pb, p_own, l = [], [], []
    for g in groups:
        slope = jnp.exp2(-(rrow + float(GROUP_A * g + 1)))
        sg = s[g] * ATT_SCALE - slope * dist
        selx = jnp.concatenate([jnp.broadcast_to(sel[g][:, n:n + 1], (SUBLANE, MOBA_BLOCK)) for n in range(nbp)],
                               axis=1)
        sg = jnp.where(selx > 0.0, sg, NEG)
        s_own = jnp.sum(qb[g].astype(f32) * kn[:, hs[g]], axis=1, keepdims=True) * ATT_SCALE
        m = jnp.maximum(jnp.max(sg, axis=1, keepdims=True), s_own)
        pb.append(jnp.exp(sg - m).astype(bf16))
        p_own.append(jnp.exp(s_own - m).astype(bf16).astype(f32))
        l.append(jnp.sum(pb[g].astype(f32), axis=1, keepdims=True) + p_own[g])
    pv = [_dot(pb[g], vb[g]) for g in groups]
    outs = []
    for g in groups:
        o = (pv[g] + p_own[g] * vn[:, hs[g]]) / l[g]
        outs.extend(o[r:r + 1, :] for r in range(GROUP_A))
    o_ref[0] = jnp.concatenate(outs, axis=1) * _silu(za_ref[0])


def _moba_decode(proj_s, cache_k, cache_v, page_table, layer):
    nbatch, n_pages = page_table.shape
    past = n_pages * PAGE_SIZE
    n_phys = cache_k.shape[1]
    ck = cache_k.reshape(cache_k.shape[0], n_phys, PAGE_SIZE * KVH_A, HD_A)
    cv = cache_v.reshape(cache_v.shape[0], n_phys, PAGE_SIZE * KVH_A, HD_A)
    p3 = proj_s.reshape(nbatch, 1, PW)
    grid_spec = pltpu.PrefetchScalarGridSpec(
        num_scalar_prefetch=1,
        grid=(nbatch,),
        in_specs=[pl.BlockSpec((1, 1, Q_A), lambda b, pt: (b, 0, OFF_QA // Q_A)),
                  pl.BlockSpec((1, 1, KV_A), lambda b, pt: (b, 0, OFF_KA // KV_A)),
                  pl.BlockSpec((1, 1, KV_A), lambda b, pt: (b, 0, OFF_VA // KV_A)),
                  pl.BlockSpec((1, 1, Q_A), lambda b, pt: (b, 0, OFF_ZA // Q_A)),
                  pl.BlockSpec(memory_space=pl.ANY),
                  pl.BlockSpec(memory_space=pl.ANY)],
        out_specs=pl.BlockSpec((1, 1, Q_A), lambda b, pt: (b, 0, 0)),
        scratch_shapes=[pltpu.VMEM((2, past * KVH_A, HD_A), f32),
                        pltpu.VMEM((2, past * KVH_A, HD_A), f32),
                        pltpu.SemaphoreType.DMA((2, 2))],
    )
    out = pl.pallas_call(
        functools.partial(_moba_decode_kernel, layer=layer, n_pages=n_pages, nbatch=nbatch),
        out_shape=jax.ShapeDtypeStruct((nbatch, 1, Q_A), f32),
        grid_spec=grid_spec,
        compiler_params=_cparams(1, VMEM_BIG),
        name="moba_decode",
    )(page_table.reshape(-1), p3, p3, p3, p3, ck, cv)
    return out.reshape(nbatch, Q_A)


GDN_T = 256
GDN_HB = 8


def _gdn_prompt_kernel(alog_ref, dtb_ref, q_ref, k_ref, v_ref, ab_ref, zb_ref, wq_ref, wk_ref, wv_ref, ng_ref,
                       o_ref, s_out_ref, s_scr, prev_scr, vnew_scr, *, T):
    hp = pl.program_id(0)
    t = pl.program_id(1)
    C = min(DN_CHUNK, T)
    n_chunks = T // C
    shift = int(math.log2(C))
    heads = range(GDN_HB)

    @pl.when(t == 0)
    def _():
        s_scr[...] = jnp.zeros_like(s_scr)
        prev_scr[...] = jnp.zeros_like(prev_scr)
        vnew_scr[...] = jnp.zeros_like(vnew_scr)

    row = lax.broadcasted_iota(jnp.int32, (T, T), 0)
    col = lax.broadcasted_iota(jnp.int32, (T, T), 1)
    same = lax.shift_right_logical(row, shift) == lax.shift_right_logical(col, shift)
    tri_incl = jnp.logical_and(same, row >= col)
    tri_strict = jnp.logical_and(same, row > col)
    ltri = jnp.where(tri_incl, 1.0, 0.0)
    ab = ab_ref[...]
    lane = lax.broadcasted_iota(jnp.int32, (T, LANE), 1)
    r8 = lax.broadcasted_iota(jnp.int32, (SUBLANE, LANE), 0)
    colT = lax.broadcasted_iota(jnp.int32, (DK_B, T), 1)

    reps = T // LANE

    def conv(x_ref, w_ref, hb, which):
        sl = slice(hb * LANE, (hb + 1) * LANE)
        x = x_ref[:, sl]
        w = w_ref[:, sl]
        prev8 = prev_scr[3 * hb + which]
        acc = x * w[CONV_W - 1:CONV_W, :]
        for s in range(1, CONV_W):
            xs = pltpu.roll(x, s, axis=0)
            top = jnp.where(r8 < s, pltpu.roll(prev8, s, axis=0), xs[:SUBLANE])
            xs = jnp.concatenate([top, xs[SUBLANE:]], axis=0)
            acc = acc + xs * w[CONV_W - 1 - s:CONV_W - s, :]
        prev_scr[3 * hb + which] = x[T - SUBLANE:, :]
        return _silu(acc)

    qn, kn, vc, beta, g_rep = [], [], [], [], []
    for hb in heads:
        h = hp * GDN_HB + hb
        qc = conv(q_ref, wq_ref, hb, 0)
        kc = conv(k_ref, wk_ref, hb, 1)
        vc.append(conv(v_ref, wv_ref, hb, 2))
        qn.append(qc * lax.rsqrt(jnp.sum(qc * qc, axis=-1, keepdims=True) + EPS) * (DK_B ** -0.5))
        kn.append(kc * lax.rsqrt(jnp.sum(kc * kc, axis=-1, keepdims=True) + EPS))
        a = jnp.sum(jnp.where(lane == h, ab, 0.0), axis=-1, keepdims=True)
        b = jnp.sum(jnp.where(lane == H_B + h, ab, 0.0), axis=-1, keepdims=True)
        beta.append(_sigmoid(b))
        glog = -jnp.exp(jnp.zeros((T, 1), f32) + alog_ref[h]) * _softplus(a + dtb_ref[h])
        g_rep.append(jnp.broadcast_to(glog, (T, LANE)))

    gam = [jnp.dot(ltri, g_rep[hb], precision=HIGHEST, preferred_element_type=f32) for hb in heads]
    gam_row = [gam[hb].T for hb in heads]
    kb = [kn[hb].astype(bf16) for hb in heads]
    kk = [lax.dot_general(kb[hb], kb[hb], NT, preferred_element_type=f32) for hb in heads]
    qk_raw = [lax.dot_general(qn[hb].astype(bf16), kb[hb], NT, preferred_element_type=f32) for hb in heads]

    decay, pw = [], []
    for hb in heads:
        gam_i = jnp.concatenate([gam[hb]] * reps, axis=1)
        gam_j = jnp.concatenate([gam_row[hb]] * reps, axis=0)
        decay.append(jnp.exp(jnp.where(tri_incl, gam_i - gam_j, -jnp.inf)))
        pw.append(jnp.where(tri_strict, -(jnp.broadcast_to(beta[hb], (T, T)) * kk[hb] * decay[hb]), 0.0))

    x_acc = list(pw)
    pwb = [pw[hb].astype(bf16) for hb in heads]
    for r in range(1, shift):
        pw = [_dot(pwb[hb], pwb[hb]) for hb in heads]
        pwb = [pw[hb].astype(bf16) for hb in heads]
        x_acc = [x_acc[hb] + pw[hb] + _dot(pwb[hb], x_acc[hb].astype(bf16)) for hb in heads]

    egam = [jnp.exp(gam[hb]) for hb in heads]
    rhs = [jnp.concatenate([vc[hb] * beta[hb], kn[hb] * (beta[hb] * egam[hb])], axis=1) for hb in heads]
    sol = [rhs[hb] + _dot(x_acc[hb].astype(bf16), rhs[hb].astype(bf16)) for hb in heads]
    u_val = [sol[hb][:, :DV_B] for hb in heads]
    k_cum_b = [sol[hb][:, DV_B:].astype(bf16) for hb in heads]
    qkb = [(qk_raw[hb] * decay[hb]).astype(bf16) for hb in heads]
    q_dec = [(qn[hb] * egam[hb]).astype(bf16) for hb in heads]
    knT = [kn[hb].T for hb in heads]

    outs = [[] for _ in heads]
    for c in range(n_chunks):
        rs = slice(c * C, (c + 1) * C)
        last = (c + 1) * C - 1
        in_chunk = jnp.logical_and(colT >= c * C, colT <= last)
        s_old = [s_scr[hb] for hb in heads]
        s_b = [s_old[hb].astype(bf16) for hb in heads]
        v_new = [u_val[hb][rs] - _dot(k_cum_b[hb][rs], s_b[hb]) for hb in heads]
        o_state = [_dot(q_dec[hb][rs], s_b[hb]) for hb in heads]
        for hb in heads:
            vnew_scr[hb, rs, :] = v_new[hb]
        vn_b = [vnew_scr[hb].astype(bf16) for hb in heads]
        for hb in heads:
            outs[hb].append(o_state[hb] + _dot(qkb[hb][rs, :], vn_b[hb]))
        for hb in heads:
            w_end = jnp.exp(jnp.where(in_chunk, gam_row[hb][:, last:last + 1] - gam_row[hb], -jnp.inf))
            k_end_t = (knT[hb] * w_end).astype(bf16)
            s_scr[hb] = s_old[hb] * jnp.exp(gam[hb][last:last + 1, :]) + _dot(k_end_t, vn_b[hb])

    for hb in heads:
        sl = slice(hb * LANE, (hb + 1) * LANE)
        o = jnp.concatenate(outs[hb], axis=0)
        o_ref[:, sl] = _rms(o, ng_ref[...]) * _silu(zb_ref[:, sl])
        s_out_ref[hb] = s_scr[hb]


def _gdn_prompt(proj, conv_w, a_log, dt_bias, norm_g):
    L = proj.shape[0]
    T = min(GDN_T, L)
    wb = GDN_HB * LANE
    nq = H_B * DK_B // wb
    return pl.pallas_call(
        functools.partial(_gdn_prompt_kernel, T=T),
        out_shape=(jax.ShapeDtypeStruct((L, V_B), f32), jax.ShapeDtypeStruct((H_B, DK_B, DV_B), f32)),
        grid=(H_B // GDN_HB, L // T),
        in_specs=[pl.BlockSpec(memory_space=pltpu.SMEM),
                  pl.BlockSpec(memory_space=pltpu.SMEM),
                  pl.BlockSpec((T, wb), lambda hp, t: (t, OFF_CONV // wb + hp)),
                  pl.BlockSpec((T, wb), lambda hp, t: (t, OFF_CONV // wb + nq + hp)),
                  pl.BlockSpec((T, wb), lambda hp, t: (t, OFF_CONV // wb + 2 * nq + hp)),
                  pl.BlockSpec((T, LANE), lambda hp, t: (t, OFF_AB // LANE)),
                  pl.BlockSpec((T, wb), lambda hp, t: (t, OFF_ZB // wb + hp)),
                  pl.BlockSpec((CONV_W, wb), lambda hp, t: (0, hp)),
                  pl.BlockSpec((CONV_W, wb), lambda hp, t: (0, nq + hp)),
                  pl.BlockSpec((CONV_W, wb), lambda hp, t: (0, 2 * nq + hp)),
                  pl.BlockSpec((1, DV_B), lambda hp, t: (0, 0))],
        out_specs=(pl.BlockSpec((T, wb), lambda hp, t: (t, hp)),
                   pl.BlockSpec((GDN_HB, DK_B, DV_B), lambda hp, t: (hp, 0, 0))),
        scratch_shapes=[pltpu.VMEM((GDN_HB, DK_B, DV_B), f32),
                        pltpu.VMEM((3 * GDN_HB, SUBLANE, LANE), f32),
                        pltpu.VMEM((GDN_HB, T, DV_B), f32)],
        compiler_params=_cparams(2, VMEM_BIG),
        name="gdn_prompt",
    )(a_log, dt_bias, proj, proj, proj, proj, proj, conv_w, conv_w, conv_w, norm_g.reshape(1, DV_B))


GDN_BB = 8


def _gdn_decode_kernel(x_ref, ab_ref, zb_ref, cs_ref, sd_ref, w_ref, alog_ref, dtb_ref, ng_ref,
                       o_ref, cs_out_ref, sd_out_ref):
    x = x_ref[...]
    w = w_ref[...]
    c0, c1, c2 = cs_ref[:, 0, :], cs_ref[:, 1, :], cs_ref[:, 2, :]
    y = c0 * w[0:1, :] + c1 * w[1:2, :] + c2 * w[2:3, :] + x * w[3:4, :]
    cs_out_ref[:, 0, :] = c1
    cs_out_ref[:, 1, :] = c2
    cs_out_ref[:, 2, :] = x
    y = _silu(y)
    ab = ab_ref[...]
    a = ab[:, 0:H_B]
    beta = _sigmoid(ab[:, H_B:2 * H_B])
    g = -jnp.exp(alog_ref[...]) * _softplus(a + dtb_ref[...])
    eg = jnp.exp(g)
    nq = H_B * DK_B
    outs = []
    for h in range(H_B):
        qc = y[:, h * DK_B:(h + 1) * DK_B]
        kc = y[:, nq + h * DK_B:nq + (h + 1) * DK_B]
        vc = y[:, 2 * nq + h * DV_B:2 * nq + (h + 1) * DV_B]
        qn = qc * lax.rsqrt(jnp.sum(qc * qc, axis=-1, keepdims=True) + EPS) * (DK_B ** -0.5)
        kn = kc * lax.rsqrt(jnp.sum(kc * kc, axis=-1, keepdims=True) + EPS)
        qk = jnp.sum(qn * kn, axis=-1, keepdims=True)
        knT = kn.T
        qnT = qn.T
        rows = []
        for bb in range(GDN_BB):
            s_old = sd_ref[bb, h]
            kcol = knT[:, bb:bb + 1]
            qcol = qnT[:, bb:bb + 1]
            be = beta[bb:bb + 1, h:h + 1]
            e = eg[bb:bb + 1, h:h + 1]
            ks = jnp.sum(kcol * s_old, axis=0, keepdims=True)
            qs = jnp.sum(qcol * s_old, axis=0, keepdims=True)
            v_new = be * vc[bb:bb + 1, :] - (be * e) * ks
            rows.append(e * qs + qk[bb:bb + 1, :] * v_new)
            sd_out_ref[bb, h] = s_old * e + kcol * v_new
        o = jnp.concatenate(rows, axis=0)
        outs.append(_rms(o, ng_ref[...]) * _silu(zb_ref[:, h * DV_B:(h + 1) * DV_B]))
    o_ref[...] = jnp.concatenate(outs, axis=1)


def _gdn_decode(proj_s, state_conv, state_delta, conv_w, a_log, dt_bias, norm_g, layer):
    nbatch = proj_s.shape[0]
    bb = GDN_BB
    return pl.pallas_call(
        _gdn_decode_kernel,
        out_shape=(jax.ShapeDtypeStruct((nbatch, V_B), f32),
                   jax.ShapeDtypeStruct((nbatch, CONV_W - 1, CONV_CH), f32),
                   jax.ShapeDtypeStruct((nbatch, H_B, DK_B, DV_B), f32)),
        grid=(nbatch // bb,),
        in_specs=[pl.BlockSpec((bb, CONV_CH), lambda i: (i, OFF_CONV // CONV_CH)),
                  pl.BlockSpec((bb, LANE), lambda i: (i, OFF_AB // LANE)),
                  pl.BlockSpec((bb, V_B), lambda i: (i, OFF_ZB // V_B)),
                  pl.BlockSpec((None, bb, CONV_W - 1, CONV_CH), lambda i: (layer, i, 0, 0)),
                  pl.BlockSpec((None, bb, H_B, DK_B, DV_B), lambda i: (layer, i, 0, 0, 0)),
                  pl.BlockSpec((CONV_W, CONV_CH), lambda i: (0, 0)),
                  pl.BlockSpec((1, H_B), lambda i: (0, 0)),
                  pl.BlockSpec((1, H_B), lambda i: (0, 0)),
                  pl.BlockSpec((1, DV_B), lambda i: (0, 0))],
        out_specs=(pl.BlockSpec((bb, V_B), lambda i: (i, 0)),
                   pl.BlockSpec((bb, CONV_W - 1, CONV_CH), lambda i: (i, 0, 0)),
                   pl.BlockSpec((bb, H_B, DK_B, DV_B), lambda i: (i, 0, 0, 0))),
        compiler_params=_cparams(1, VMEM_BIG),
        name="gdn_decode",
    )(proj_s, proj_s, proj_s, state_conv, state_delta, conv_w,
      a_log.reshape(1, H_B), dt_bias.reshape(1, H_B), norm_g.reshape(1, DV_B))


S5_JB = W_C // LANE
S5_GPB = LANE // S5_GROUP_CH
S5_SW = S5_GPB * S5_STATE
S5_TT = 512
S5_NSEG = SUBLANE


def _s5_param_kernel(lr_ref, li_ref, ldt_ref, bre_ref, bim_ref, are_ref, aim_ref, bbre_ref, bbim_ref):
    lr = lr_ref[...]
    li = li_ref[...]
    dt = jnp.exp(ldt_ref[...])
    mag = jnp.exp(lr * dt)
    ang = li * dt
    a_re = mag * jnp.cos(ang)
    a_im = mag * jnp.sin(ang)
    den = lr * lr + li * li
    f_re = ((a_re - 1.0) * lr + a_im * li) / den
    f_im = (a_im * lr - (a_re - 1.0) * li) / den
    are_ref[...] = a_re
    aim_ref[...] = a_im
    fr = f_re[:, None, :]
    fi = f_im[:, None, :]
    bre = bre_ref[...]
    bim = bim_ref[...]
    bbre_ref[...] = fr * bre - fi * bim
    bbim_ref[...] = fr * bim + fi * bre


def _s5_params(lam_re, lam_im, log_dt, b_re, b_im, c_re, c_im, d):
    G, P, CH = S5_GROUPS, S5_STATE, S5_GROUP_CH
    bt_re = jnp.swapaxes(b_re, 1, 2)
    bt_im = jnp.swapaxes(b_im, 1, 2)
    a_re, a_im, bb_re, bb_im = pl.pallas_call(
        _s5_param_kernel,
        out_shape=(jax.ShapeDtypeStruct((G, P), f32), jax.ShapeDtypeStruct((G, P), f32),
                   jax.ShapeDtypeStruct((G, CH, P), f32), jax.ShapeDtypeStruct((G, CH, P), f32)),
        name="s5_params",
    )(lam_re, lam_im, log_dt.reshape(G, 1), bt_re, bt_im)
    eye = jnp.eye(S5_GPB, dtype=f32)

    def bdiag_in(bb):
        x = bb.reshape(S5_JB, S5_GPB, CH, P)
        return jnp.einsum("jacp,ab->jacbp", x, eye).reshape(S5_JB, LANE, S5_SW)

    def bdiag_out(c):
        x = c.reshape(S5_JB, S5_GPB, CH, P)
        return jnp.einsum("jacp,ab->jbpac", x, eye).reshape(S5_JB, S5_SW, LANE)

    wb = jnp.concatenate([bdiag_in(bb_re), bdiag_in(bb_im)], axis=-1).astype(bf16)
    wc_re = bdiag_out(c_re).astype(bf16)
    wc_im = bdiag_out(c_im).astype(bf16)
    return dict(wb=wb, wc_re=wc_re, wc_im=wc_im,
                a_re=a_re.reshape(S5_JB, 1, S5_SW), a_im=a_im.reshape(S5_JB, 1, S5_SW),
                d=d.reshape(S5_JB, 1, LANE))


def _s5_prompt_kernel(u_ref, wb_ref, wcre_ref, wcim_ref, are_ref, aim_ref, d_ref,
                      y_ref, hre_ref, him_ref, a_scr, b_scr, carry_scr, *, TT):
    t = pl.program_id(1)
    nlb = S5_SW // LANE
    seg = TT // S5_NSEG
    pitch = seg + SUBLANE

    @pl.when(t == 0)
    def _():
        carry_scr[...] = jnp.zeros_like(carry_scr)

    u = u_ref[...]
    zpad = jnp.zeros((SUBLANE, LANE), f32)
    u_pad = jnp.concatenate([x for s in range(S5_NSEG) for x in (u[s * seg:(s + 1) * seg], zpad)], axis=0)
    bu = _dot(u_pad.astype(bf16), wb_ref[...])
    for c in range(2 * nlb):
        a_scr[c] = bu[:, c * LANE:(c + 1) * LANE]

    are = are_ref[...]
    aim = aim_ref[...]
    ar8 = [jnp.broadcast_to(are[:, c * LANE:(c + 1) * LANE], (S5_NSEG, LANE)) for c in range(nlb)]
    ai8 = [jnp.broadcast_to(aim[:, c * LANE:(c + 1) * LANE], (S5_NSEG, LANE)) for c in range(nlb)]

    def seg_rows(k):
        return pl.ds(k, S5_NSEG, stride=pitch)

    zero8 = jnp.zeros((S5_NSEG, LANE), f32)
    hr = [zero8] * nlb
    hi = [zero8] * nlb
    for k in range(seg):
        for c in range(nlb):
            nr = ar8[c] * hr[c] - ai8[c] * hi[c] + a_scr[c, seg_rows(k), :]
            ni = ar8[c] * hi[c] + ai8[c] * hr[c] + a_scr[nlb + c, seg_rows(k), :]
            b_scr[c, seg_rows(k), :] = nr
            b_scr[nlb + c, seg_rows(k), :] = ni
            hr[c], hi[c] = nr, ni
    ends = hr + hi

    pr, pi = are, aim
    for _ in range(int(math.log2(seg))):
        pr, pi = pr * pr - pi * pi, 2.0 * pr * pi

    carry = carry_scr[...]
    er, ei = carry[:, :S5_SW], carry[:, S5_SW:]
    rows_r, rows_i = [], []
    for s in range(S5_NSEG):
        rows_r.append(er)
        rows_i.append(ei)
        loc_r = jnp.concatenate([ends[c][s:s + 1, :] for c in range(nlb)], axis=1)
        loc_i = jnp.concatenate([ends[nlb + c][s:s + 1, :] for c in range(nlb)], axis=1)
        er, ei = pr * er - pi * ei + loc_r, pr * ei + pi * er + loc_i
    carry_scr[...] = jnp.concatenate([er, ei], axis=1)
    hre_ref[0] = er
    him_ref[0] = ei
    ein_r = jnp.concatenate(rows_r, axis=0)
    ein_i = jnp.concatenate(rows_i, axis=0)

    pw_r = list(ar8)
    pw_i = list(ai8)
    for k in range(seg):
        for c in range(nlb):
            e_r = ein_r[:, c * LANE:(c + 1) * LANE]
            e_i = ein_i[:, c * LANE:(c + 1) * LANE]
            a_scr[c, seg_rows(k), :] = b_scr[c, seg_rows(k), :] + (pw_r[c] * e_r - pw_i[c] * e_i)
            a_scr[nlb + c, seg_rows(k), :] = b_scr[nlb + c, seg_rows(k), :] + (pw_r[c] * e_i + pw_i[c] * e_r)
            if k + 1 < seg:
                pw_r[c], pw_i[c] = pw_r[c] * ar8[c] - pw_i[c] * ai8[c], pw_r[c] * ai8[c] + pw_i[c] * ar8[c]

    h_re = jnp.concatenate([a_scr[c] for c in range(nlb)], axis=1).astype(bf16)
    h_im = jnp.concatenate([a_scr[nlb + c] for c in range(nlb)], axis=1).astype(bf16)
    y_pad = _dot(h_re, wcre_ref[...]) - _dot(h_im, wcim_ref[...])
    y = jnp.concatenate([y_pad[s * pitch:s * pitch + seg] for s in range(S5_NSEG)], axis=0)
    y_ref[...] = y + d_ref[...] * u


def _s5_prompt(proj, sp):
    L = proj.shape[0]
    TT = min(S5_TT, L)
    jb_spec = lambda shape: pl.BlockSpec((None,) + shape, lambda j, t: (j, 0, 0))
    y, hre, him = pl.pallas_call(
        functools.partial(_s5_prompt_kernel, TT=TT),
        out_shape=(jax.ShapeDtypeStruct((L, W_C), f32),
                   jax.ShapeDtypeStruct((S5_JB, 1, S5_SW), f32),
                   jax.ShapeDtypeStruct((S5_JB, 1, S5_SW), f32)),
        grid=(S5_JB, L // TT),
        in_specs=[pl.BlockSpec((TT, LANE), lambda j, t: (t, OFF_UC // LANE + j)),
                  jb_spec((LANE, 2 * S5_SW)), jb_spec((S5_SW, LANE)), jb_spec((S5_SW, LANE)),
                  jb_spec((1, S5_SW)), jb_spec((1, S5_SW)), jb_spec((1, LANE))],
        out_specs=(pl.BlockSpec((TT, LANE), lambda j, t: (t, j)),
                   pl.BlockSpec((1, 1, S5_SW), lambda j, t: (j, 0, 0)),
                   pl.BlockSpec((1, 1, S5_SW), lambda j, t: (j, 0, 0))),
        scratch_shapes=[pltpu.VMEM((2 * S5_SW // LANE, TT + S5_NSEG * SUBLANE, LANE), f32),
                        pltpu.VMEM((2 * S5_SW // LANE, TT + S5_NSEG * SUBLANE, LANE), f32),
                        pltpu.VMEM((1, 2 * S5_SW), f32)],
        compiler_params=_cparams(2, VMEM_BIG),
        name="s5_prompt",
    )(proj, sp["wb"], sp["wc_re"], sp["wc_im"], sp["a_re"], sp["a_im"], sp["d"])
    return y, hre.reshape(S5_GROUPS, S5_STATE), him.reshape(S5_GROUPS, S5_STATE)


def _s5_decode_kernel(u_ref, h0r_ref, h0i_ref, wb_ref, wcre_ref, wcim_ref, are_ref, aim_ref, d_ref,
                      y_ref, hr_ref, hi_ref):
    u = u_ref[...]
    bu = _dot(u.astype(bf16), wb_ref[...])
    are, aim = are_ref[...], aim_ref[...]
    h0r, h0i = h0r_ref[...], h0i_ref[...]
    hr = bu[:, :S5_SW] + are * h0r - aim * h0i
    hi = bu[:, S5_SW:] + are * h0i + aim * h0r
    hr_ref[...] = hr
    hi_ref[...] = hi
    y_ref[...] = _dot(hr.astype(bf16), wcre_ref[...]) - _dot(hi.astype(bf16), wcim_ref[...]) + d_ref[...] * u


def _s5_decode(proj_s, st_re, st_im, sp, layer):
    nbatch = proj_s.shape[0]
    sre = st_re.reshape(st_re.shape[0], nbatch, S5_GROUPS * S5_STATE)
    sim = st_im.reshape(st_im.shape[0], nbatch, S5_GROUPS * S5_STATE)
    jb_spec = lambda shape: pl.BlockSpec((None,) + shape, lambda j: (j, 0, 0))
    st_spec = pl.BlockSpec((None, nbatch, S5_SW), lambda j: (layer, 0, j))
    y, hr, hi = pl.pallas_call(
        _s5_decode_kernel,
        out_shape=(jax.ShapeDtypeStruct((nbatch, W_C), f32),
                   jax.ShapeDtypeStruct((nbatch, S5_GROUPS * S5_STATE), f32),
                   jax.ShapeDtypeStruct((nbatch, S5_GROUPS * S5_STATE), f32)),
        grid=(S5_JB,),
        in_specs=[pl.BlockSpec((nbatch, LANE), lambda j: (0, OFF_UC // LANE + j)),
                  st_spec, st_spec,
                  jb_spec((LANE, 2 * S5_SW)), jb_spec((S5_SW, LANE)), jb_spec((S5_SW, LANE)),
                  jb_spec((1, S5_SW)), jb_spec((1, S5_SW)), jb_spec((1, LANE))],
        out_specs=(pl.BlockSpec((nbatch, LANE), lambda j: (0, j)),
                   pl.BlockSpec((nbatch, S5_SW), lambda j: (0, j)),
                   pl.BlockSpec((nbatch, S5_SW), lambda j: (0, j))),
        compiler_params=_cparams(1),
        name="s5_decode",
    )(proj_s, sre, sim, sp["wb"], sp["wc_re"], sp["wc_im"], sp["a_re"], sp["a_im"], sp["d"])
    return (y, hr.reshape(nbatch, S5_GROUPS, S5_STATE), hi.reshape(nbatch, S5_GROUPS, S5_STATE))


def _s5_glu_kernel(y_ref, zc_ref, w_ref, b_ref, o_ref):
    y = y_ref[...]
    ge = 0.5 * y * (1.0 + lax.erf(y * (0.5 ** 0.5)))
    gl = _dot(ge.astype(bf16), w_ref[...]) + b_ref[...]
    o_ref[...] = ge * _sigmoid(gl) * _silu(zc_ref[...])


def _s5_glu(y, proj, w_glu, b_glu, tm):
    m = y.shape[0]
    return pl.pallas_call(
        _s5_glu_kernel,
        out_shape=jax.ShapeDtypeStruct((m, W_C), f32),
        grid=(m // tm,),
        in_specs=[pl.BlockSpec((tm, W_C), lambda i: (i, 0)),
                  pl.BlockSpec((tm, W_C), lambda i: (i, OFF_ZC // W_C)),
                  pl.BlockSpec((W_C, W_C), lambda i: (0, 0)),
                  pl.BlockSpec((1, W_C), lambda i: (0, 0))],
        out_specs=pl.BlockSpec((tm, W_C), lambda i: (i, 0)),
        compiler_params=_cparams(1, VMEM_BIG),
        name="s5_glu",
    )(y, proj, w_glu, b_glu.reshape(1, W_C))


def _merge_kernel(oa_ref, ob_ref, oc_ref, ga_ref, gb_ref, gc_ref, wb_ref, o_ref):
    acc = _sigmoid(ga_ref[...]) * _dot(oa_ref[...].astype(bf16), wb_ref[0])
    acc = acc + _sigmoid(gb_ref[...]) * _dot(ob_ref[...].astype(bf16), wb_ref[1])
    acc = acc + _sigmoid(gc_ref[...]) * _dot(oc_ref[...].astype(bf16), wb_ref[2])
    o_ref[...] = acc.astype(bf16)


def _merge(out_a, out_b, out_c, proj, w_branch, tm):
    m = out_a.shape[0]
    act = pl.BlockSpec((tm, Q_A), lambda i: (i, 0))
    gate = lambda k: pl.BlockSpec((tm, D_MODEL), lambda i: (i, OFF_G // D_MODEL + k))
    return pl.pallas_call(
        _merge_kernel,
        out_shape=jax.ShapeDtypeStruct((m, D_MODEL), bf16),
        grid=(m // tm,),
        in_specs=[act, act, act, gate(0), gate(1), gate(2),
                  pl.BlockSpec((3, Q_A, D_MODEL), lambda i: (0, 0, 0), pipeline_mode=pl.Buffered(1))],
        out_specs=pl.BlockSpec((tm, D_MODEL), lambda i: (i, 0)),
        compiler_params=_cparams(1, VMEM_BIG),
        name="merge",
    )(out_a, out_b, out_c, proj, proj, proj, w_branch)


def _post_kernel(x_ref, mg_ref, pe_ref, wout_ref, plew_ref, pleg_ref, png_ref, fng_ref, o_ref, *, final):
    x2 = x_ref[...] + _dot(mg_ref[...], wout_ref[...])
    gate = _sigmoid(_dot(_rms(x2, png_ref[...]).astype(bf16), pleg_ref[...]))
    x3 = x2 + _dot(pe_ref[...].astype(bf16), plew_ref[...]) * gate
    if final:
        x3 = _rms(x3, fng_ref[...])
    o_ref[...] = x3


def _post(x2d, merged, pe, w_out, ple_w, ple_w_gate, ple_norm_g, final_norm_g, final, tm):
    m = x2d.shape[0]
    const = lambda shape: pl.BlockSpec(shape, lambda i: (0, 0), pipeline_mode=pl.Buffered(1))
    return pl.pallas_call(
        functools.partial(_post_kernel, final=final),
        out_shape=jax.ShapeDtypeStruct((m, D_MODEL), f32),
        grid=(m // tm,),
        in_specs=[pl.BlockSpec((tm, D_MODEL), lambda i: (i, 0)),
                  pl.BlockSpec((tm, D_MODEL), lambda i: (i, 0)),
                  pl.BlockSpec((tm, PLE_DIM), lambda i: (i, 0)),
                  const((D_MODEL, D_MODEL)), const((PLE_DIM, D_MODEL)), const((D_MODEL, D_MODEL)),
                  const((1, D_MODEL)), const((1, D_MODEL))],
        out_specs=pl.BlockSpec((tm, D_MODEL), lambda i: (i, 0)),
        compiler_params=_cparams(1, VMEM_BIG),
        name="post",
    )(x2d, merged, pe, w_out, ple_w, ple_w_gate, ple_norm_g.reshape(1, D_MODEL), final_norm_g.reshape(1, D_MODEL))


def _pad_w_in(w_in):
    o = [0, Q_A, Q_A + KV_A, Q_A + 2 * KV_A, 2 * Q_A + 2 * KV_A]
    c_end = o[4] + CONV_CH
    ab_end = c_end + 2 * H_B
    z_end = ab_end + V_B + 2 * W_C
    w = w_in.astype(bf16)
    pad = jnp.zeros((w.shape[0], AB_W - 2 * H_B), bf16)
    return jnp.concatenate([w[:, z_end:], w[:, :c_end], w[:, ab_end:z_end], w[:, c_end:ab_end], pad], axis=1)


def _row_tile(m, pref):
    return pref if m % pref == 0 else m


def _layer_weights(i, norm_g, w_in, dn_conv_w, dn_a_log, dn_dt_bias, dn_norm_g, s5_lambda_re, s5_lambda_im,
                   s5_b_re, s5_b_im, s5_c_re, s5_c_im, s5_d, s5_log_dt, s5_w_glu, s5_b_glu, w_branch, w_out,
                   ple_w, ple_norm_g, ple_w_gate):
    return dict(
        norm_g=norm_g[i], w_pad=_pad_w_in(w_in[i]), conv_w=dn_conv_w[i], a_log=dn_a_log[i], dt_bias=dn_dt_bias[i],
        dn_norm_g=dn_norm_g[i],
        s5=_s5_params(s5_lambda_re[i], s5_lambda_im[i], s5_log_dt[i], s5_b_re[i], s5_b_im[i], s5_c_re[i],
                      s5_c_im[i], s5_d[i]),
        w_glu=s5_w_glu[i].astype(bf16), b_glu=s5_b_glu[i], w_branch=w_branch[i].astype(bf16),
        w_out=w_out[i].astype(bf16), ple_w=ple_w[i].astype(bf16), ple_norm_g=ple_norm_g[i],
        ple_w_gate=ple_w_gate[i].astype(bf16))


def _finish(x2d, proj, out_a, out_b, y_c, pe, lw, final_norm_g, final):
    m = x2d.shape[0]
    out_c = _s5_glu(y_c, proj, lw["w_glu"], lw["b_glu"], _row_tile(m, 512))
    merged = _merge(out_a, out_b, out_c, proj, lw["w_branch"], _row_tile(m, 256))
    return _post(x2d, merged, pe, lw["w_out"], lw["ple_w"], lw["ple_w_gate"], lw["ple_norm_g"], final_norm_g,
                 final, _row_tile(m, 256))


def _prompt_layer(x2d, pe, lw, final_norm_g, final):
    L = x2d.shape[0]
    proj = _inproj(x2d, lw["norm_g"], lw["w_pad"], _row_tile(L, 1024))
    out_a = _moba_prompt(proj)
    out_b, s_new = _gdn_prompt(proj, lw["conv_w"], lw["a_log"], lw["dt_bias"], lw["dn_norm_g"])
    y_c, hre, him = _s5_prompt(proj, lw["s5"])
    x_new = _finish(x2d, proj, out_a, out_b, y_c, pe, lw, final_norm_g, final)
    ka = proj[:, OFF_KA:OFF_KA + KV_A].reshape(1, L, KVH_A, HD_A)
    va = proj[:, OFF_VA:OFF_VA + KV_A].reshape(1, L, KVH_A, HD_A)
    conv_new = proj[L - (CONV_W - 1):, OFF_CONV:OFF_CONV + CONV_CH][None]
    return x_new, (ka, va, conv_new, s_new[None], hre[None], him[None])


def _sample_layer(x2d, pe, lw, final_norm_g, final, layer, cache_k, cache_v, page_table, state_conv, state_delta,
                  state_s5_re, state_s5_im):
    nbatch = x2d.shape[0]
    proj = _inproj(x2d, lw["norm_g"], lw["w_pad"], nbatch)
    out_a = _moba_decode(proj, cache_k, cache_v, page_table, layer)
    out_b, conv_new, s_new = _gdn_decode(proj, state_conv, state_delta, lw["conv_w"], lw["a_log"], lw["dt_bias"],
                                         lw["dn_norm_g"], layer)
    y_c, hr, hi = _s5_decode(proj, state_s5_re, state_s5_im, lw["s5"], layer)
    x_new = _finish(x2d, proj, out_a, out_b, y_c, pe, lw, final_norm_g, final)
    ka = proj[:, OFF_KA:OFF_KA + KV_A].reshape(nbatch, 1, KVH_A, HD_A)
    va = proj[:, OFF_VA:OFF_VA + KV_A].reshape(nbatch, 1, KVH_A, HD_A)
    return x_new, (ka, va, conv_new, s_new, hr, hi)


def kernel(x_prompt, x_sample, cache_k, cache_v, page_table, state_conv, state_delta, state_s5_re, state_s5_im,
           p_prompt, p_sample, norm_g, w_in, dn_conv_w, dn_a_log, dn_dt_bias, dn_norm_g, s5_lambda_re, s5_lambda_im,
           s5_b_re, s5_b_im, s5_c_re, s5_c_im, s5_d, s5_log_dt, s5_w_glu, s5_b_glu, w_branch, w_out, ple_w,
           ple_norm_g, ple_w_gate, final_norm_g):
    depth = w_in.shape[0]
    bp, L, _ = x_prompt.shape
    nbatch = x_sample.shape[0]
    assert bp == 1 and x_sample.shape[1] == 1
    xp = x_prompt.reshape(L, D_MODEL)
    xs = x_sample.reshape(nbatch, D_MODEL)
    new_p, new_s = [], []
    for i in range(depth):
        lw = _layer_weights(i, norm_g, w_in, dn_conv_w, dn_a_log, dn_dt_bias, dn_norm_g, s5_lambda_re, s5_lambda_im,
                            s5_b_re, s5_b_im, s5_c_re, s5_c_im, s5_d, s5_log_dt, s5_w_glu, s5_b_glu, w_branch,
                            w_out, ple_w, ple_norm_g, ple_w_gate)
        final = i == depth - 1
        xp, st = _prompt_layer(xp, p_prompt[i, 0], lw, final_norm_g, final)
        new_p.append(st)
        xs, st = _sample_layer(xs, p_sample[i, :, 0], lw, final_norm_g, final, i, cache_k, cache_v, page_table,
                               state_conv, state_delta, state_s5_re, state_s5_im)
        new_s.append(st)

    def stk(states, j):
        return jnp.stack([s[j] for s in states])

    return (xp.reshape(1, L, D_MODEL), xs.reshape(nbatch, 1, D_MODEL),
            stk(new_p, 0), stk(new_p, 1), stk(new_p, 2), stk(new_p, 3), stk(new_p, 4), stk(new_p, 5),
            stk(new_s, 0), stk(new_s, 1), stk(new_s, 2), stk(new_s, 3), stk(new_s, 4), stk(new_s, 5))
```

```python
import functools
import math

import jax
import jax.numpy as jnp
from jax import lax
from jax.experimental import pallas as pl
from jax.experimental.pallas import tpu as pltpu

f32 = jnp.float32
bf16 = jnp.bfloat16

D_MODEL = 2048
H_A, KVH_A, HD_A = 8, 4, 128
GROUP_A = H_A // KVH_A
MOBA_BLOCK, MOBA_TOPK = 256, 3
MOBA_RC = 256
H_B, DK_B, DV_B, CONV_W = 8, 128, 128, 4
DN_CHUNK = 64
W_C, S5_GROUP_CH, S5_GROUPS, S5_STATE = 1024, 16, 64, 64
PLE_DIM = 256
PAGE_SIZE = 128
EPS = 1e-6
Q_A = H_A * HD_A
KV_A = KVH_A * HD_A
V_B = H_B * DV_B
CONV_CH = 2 * H_B * DK_B + V_B
N_GATE = 3 * D_MODEL

OFF_QA = 0
OFF_KA = OFF_QA + Q_A
OFF_VA = OFF_KA + KV_A
OFF_ZA = OFF_VA + KV_A
OFF_CONV = OFF_ZA + Q_A
OFF_AB = OFF_CONV + CONV_CH
AB_W = 1024
OFF_ZB = OFF_AB + AB_W
OFF_UC = OFF_ZB + V_B
OFF_ZC = OFF_UC + W_C
OFF_G = OFF_ZC + W_C
PW = OFF_G + N_GATE

LANE = 128
SUBLANE = 8
TN_IN = 1024
NEG = -1e30
ATT_SCALE = HD_A ** -0.5
LOG2E = math.log2(math.e)
ATT_C1 = ATT_SCALE * LOG2E
VMEM_BIG = 56 * 1024 * 1024
HIGHEST = lax.Precision.HIGHEST
NT = (((1,), (1,)), ((), ()))


def _cparams(n_axes, vmem=None):
    return pltpu.CompilerParams(dimension_semantics=("arbitrary",) * n_axes, vmem_limit_bytes=vmem)


def _sigmoid(x):
    return 1.0 / (1.0 + jnp.exp(-x))


def _silu(x):
    return x * _sigmoid(x)


def _softplus(x):
    return jnp.maximum(x, 0.0) + jnp.log1p(jnp.exp(-jnp.abs(x)))


def _rms(x, g):
    return x * lax.rsqrt(jnp.mean(x * x, axis=-1, keepdims=True) + EPS) * g


def _dot(a, b):
    return jnp.dot(a, b, preferred_element_type=f32)


def _inproj_kernel(x_ref, g_ref, w_ref, o_ref, h_scr):
    @pl.when(pl.program_id(1) == 0)
    def _():
        h_scr[...] = _rms(x_ref[...], g_ref[...]).astype(bf16)

    o_ref[...] = _dot(h_scr[...], w_ref[...])


def _inproj(x2d, g, w_pad, tm):
    m = x2d.shape[0]
    return pl.pallas_call(
        _inproj_kernel,
        out_shape=jax.ShapeDtypeStruct((m, PW), f32),
        grid=(m // tm, PW // TN_IN),
        in_specs=[pl.BlockSpec((tm, D_MODEL), lambda i, n: (i, 0)),
                  pl.BlockSpec((1, D_MODEL), lambda i, n: (0, 0)),
                  pl.BlockSpec((D_MODEL, TN_IN), lambda i, n: (0, n))],
        out_specs=pl.BlockSpec((tm, TN_IN), lambda i, n: (i, n)),
        scratch_shapes=[pltpu.VMEM((tm, D_MODEL), bf16)],
        compiler_params=_cparams(2, VMEM_BIG),
        name="inproj",
    )(x2d, g.reshape(1, D_MODEL), w_pad)


def _topk_select(scores, n_valid, lane_f, n_sel):
    sc = jnp.where(lane_f < n_valid, scores, -jnp.inf)
    sel = jnp.zeros_like(sc)
    for _ in range(n_sel):
        m = jnp.max(sc, axis=-1, keepdims=True)
        idx = jnp.min(jnp.where(sc == m, lane_f, float(LANE)), axis=-1, keepdims=True)
        hit = lane_f == idx
        sel = jnp.maximum(sel, jnp.where(hit, jnp.where(m > -jnp.inf, 1.0, 0.0), 0.0))
        sc = jnp.where(hit, -jnp.inf, sc)
    return sel


def _moba_prompt_kernel(q_ref, k_ref, v_ref, za_ref, o_ref,
                        kb_scr, vb_scr, kmean_scr, bias_scr, off_scr, acc_scr, m_scr, *, nb):
    g = pl.program_id(0)
    i = pl.program_id(1)
    rows = GROUP_A * MOBA_BLOCK
    pair = 2 * MOBA_BLOCK

    @pl.when(i == 0)
    def _():
        kb_scr[...] = k_ref[...].astype(bf16)
        vb_scr[:, 0:HD_A] = v_ref[...].astype(bf16)
        vb_scr[:, HD_A:] = jnp.ones((nb * MOBA_BLOCK, HD_A), bf16)
        kmean_scr[...] = jnp.zeros_like(kmean_scr)
        for n in range(nb):
            kmean_scr[n:n + 1, :] = jnp.mean(k_ref[n * MOBA_BLOCK:(n + 1) * MOBA_BLOCK, :], axis=0, keepdims=True)
        row = lax.broadcasted_iota(jnp.int32, (rows, pair), 0)
        col = lax.broadcasted_iota(jnp.int32, (rows, pair), 1)
        second = row >= MOBA_BLOCK
        gf = jnp.zeros((rows, pair), f32) + g.astype(f32)
        slope = jnp.exp2(-(2.0 * gf + 1.0) - jnp.where(second, 1.0, 0.0)) * LOG2E
        rin = row - jnp.where(second, MOBA_BLOCK, 0)
        bias_scr[...] = slope * (col - rin).astype(f32)
        off_scr[...] = slope[:, :LANE] * float(MOBA_BLOCK)

    q = q_ref[...]
    q2 = jnp.concatenate([q[:, :HD_A], q[:, HD_A:]], axis=0)
    lane_f = lax.broadcasted_iota(jnp.int32, (rows, LANE), 1).astype(f32)
    scores = lax.dot_general(q2, kmean_scr[...], NT, precision=HIGHEST, preferred_element_type=f32)
    i_f = jnp.zeros((rows, LANE), f32) + i.astype(f32)
    sel = _topk_select(scores, i_f, lane_f, min(MOBA_TOPK, nb))
    selneg = jnp.where(sel > 0.0, 0.0, NEG)
    lhs = jnp.concatenate([q2.astype(bf16), selneg.astype(bf16)], axis=1)

    acc_scr[...] = jnp.zeros_like(acc_scr)
    m_scr[...] = jnp.full_like(m_scr, -jnp.inf)

    def attend(n, width, aux, dist, causal):
        off = pl.multiple_of(n * MOBA_BLOCK, MOBA_BLOCK)
        rhs = jnp.concatenate([kb_scr[pl.ds(off, width), :], aux], axis=1)
        vblk = vb_scr[pl.ds(off, width), :]
        rss = [slice(c * MOBA_RC, (c + 1) * MOBA_RC) for c in range(rows // MOBA_RC)]
        t = [lax.dot_general(lhs[rs], rhs, NT, preferred_element_type=f32) * ATT_C1 + bias_scr[rs, :width]
             for rs in rss]
        if causal:
            c_i = lax.broadcasted_iota(jnp.int32, (MOBA_RC, width), 1)
            r_i = lax.broadcasted_iota(jnp.int32, (MOBA_RC, width), 0)
            t = [jnp.where(c_i <= r_i + (k * MOBA_RC) % MOBA_BLOCK, t[k], NEG) for k in range(len(rss))]
        m_prev = [m_scr[rs] for rs in rss]
        m_cur = [jnp.max(x, axis=1, keepdims=True) for x in t]
        shift = None
        if dist is not None:
            shift = [off_scr[rs] * dist for rs in rss]
            m_cur = [mc - sh for mc, sh in zip(m_cur, shift)]
        m_new = [jnp.maximum(a, b) for a, b in zip(m_prev, m_cur)]
        alpha = [jnp.exp2(a - b) for a, b in zip(m_prev, m_new)]
        sub = m_new if shift is None else [a + b for a, b in zip(m_new, shift)]
        p = [jnp.exp2(x - jnp.concatenate([s] * (width // LANE), axis=1)).astype(bf16) for x, s in zip(t, sub)]
        pv = [_dot(x, vblk) for x in p]
        for k, rs in enumerate(rss):
            acc_scr[rs] = acc_scr[rs] * jnp.concatenate([alpha[k], alpha[k]], axis=1) + pv[k]
            m_scr[rs] = m_new[k]

    def dvec(d):
        return jnp.zeros((MOBA_RC, LANE), f32) + d

    attend(i, MOBA_BLOCK, jnp.zeros((MOBA_BLOCK, LANE), bf16), None, True)

    lane_k = lax.broadcasted_iota(jnp.int32, (pair, LANE), 1)
    second_k = jnp.where(lax.broadcasted_iota(jnp.int32, (pair, LANE), 0) >= MOBA_BLOCK, 1, 0)

    def body(j, c):
        n = 2 * j
        aux = jnp.where(lane_k == n + second_k, 1.0, 0.0).astype(bf16)
        attend(n, pair, aux, dvec((i - n).astype(f32)), False)
        return c

    lax.fori_loop(0, lax.shift_right_logical(i, 1), body, 0)

    @pl.when(lax.rem(i, 2) == 1)
    def _():
        lane_1 = lax.broadcasted_iota(jnp.int32, (MOBA_BLOCK, LANE), 1)
        aux = jnp.where(lane_1 == i - 1, 1.0, 0.0).astype(bf16)
        attend(i - 1, MOBA_BLOCK, aux, dvec(1.0), False)

    acc = acc_scr[...]
    o = acc[:, :HD_A] / acc[:, HD_A:]
    o2 = jnp.concatenate([o[:MOBA_BLOCK], o[MOBA_BLOCK:]], axis=1)
    o_ref[...] = o2 * _silu(za_ref[...])


def _moba_prompt(proj):
    L = proj.shape[0]
    nb = L // MOBA_BLOCK
    rows = GROUP_A * MOBA_BLOCK
    wq = GROUP_A * HD_A
    return pl.pallas_call(
        functools.partial(_moba_prompt_kernel, nb=nb),
        out_shape=jax.ShapeDtypeStruct((L, Q_A), f32),
        grid=(KVH_A, nb),
        in_specs=[pl.BlockSpec((MOBA_BLOCK, wq), lambda g, i: (i, OFF_QA // wq + g)),
                  pl.BlockSpec((L, HD_A), lambda g, i: (0, OFF_KA // HD_A + g)),
                  pl.BlockSpec((L, HD_A), lambda g, i: (0, OFF_VA // HD_A + g)),
                  pl.BlockSpec((MOBA_BLOCK, wq), lambda g, i: (i, OFF_ZA // wq + g))],
        out_specs=pl.BlockSpec((MOBA_BLOCK, wq), lambda g, i: (i, g)),
        scratch_shapes=[pltpu.VMEM((L, HD_A), bf16),
                        pltpu.VMEM((L, 2 * HD_A), bf16),
                        pltpu.VMEM((LANE, HD_A), f32),
                        pltpu.VMEM((rows, 2 * MOBA_BLOCK), f32),
                        pltpu.VMEM((rows, LANE), f32),
                        pltpu.VMEM((rows, 2 * HD_A), f32),
                        pltpu.VMEM((rows, HD_A), f32)],
        compiler_params=_cparams(2, VMEM_BIG),
        name="moba_prompt",
    )(proj, proj, proj, proj)


def _moba_decode_kernel(pt_ref, q_ref, kn_ref, vn_ref, za_ref, ck_ref, cv_ref, o_ref,
                        kbuf, vbuf, sems, *, layer, n_pages, nbatch):
    b = pl.program_id(0)
    past = n_pages * PAGE_SIZE
    nbp = past // MOBA_BLOCK
    prow = PAGE_SIZE * KVH_A

    def copies(bb, slot):
        out = []
        for p in range(n_pages):
            page = pt_ref[bb * n_pages + p]
            out.append(pltpu.make_async_copy(ck_ref.at[layer, page], kbuf.at[slot, pl.ds(p * prow, prow), :],
                                             sems.at[0, slot]))
            out.append(pltpu.make_async_copy(cv_ref.at[layer, page], vbuf.at[slot, pl.ds(p * prow, prow), :],
                                             sems.at[1, slot]))
        return out

    slot = lax.rem(b, 2)

    @pl.when(b == 0)
    def _():
        for c in copies(0, 0):
            c.start()

    @pl.when(b + 1 < nbatch)
    def _():
        for c in copies(b + 1, 1 - slot):
            c.start()

    for c in copies(b, slot):
        c.wait()

    q = q_ref[0]
    kn = kn_ref[0].astype(bf16).astype(f32)
    vn = vn_ref[0].astype(bf16).astype(f32)
    lane_f = lax.broadcasted_iota(jnp.int32, (SUBLANE, LANE), 1).astype(f32)
    rrow = lax.broadcasted_iota(jnp.int32, (SUBLANE, past), 0).astype(f32)
    pos = lax.broadcasted_iota(jnp.int32, (SUBLANE, past), 1)
    dist = (past - pos).astype(f32)
    zpad = jnp.zeros((SUBLANE - GROUP_A, HD_A), f32)
    groups = range(KVH_A)
    hs = [slice(g * HD_A, (g + 1) * HD_A) for g in groups]
    kf = [kbuf[slot, pl.ds(g, past, stride=KVH_A), :] for g in groups]
    qg = [jnp.concatenate([q[:, (GROUP_A * g + r) * HD_A:(GROUP_A * g + r + 1) * HD_A] for r in range(GROUP_A)]
                          + [zpad], axis=0) for g in groups]
    qb = [x.astype(bf16) for x in qg]
    s = [lax.dot_general(qb[g], kf[g].astype(bf16), NT, preferred_element_type=f32) for g in groups]
    kmean = [jnp.concatenate(
        [jnp.mean(kf[g][n * MOBA_BLOCK:(n + 1) * MOBA_BLOCK, :], axis=0, keepdims=True) for n in range(nbp)]
        + [jnp.zeros((LANE - nbp, HD_A), f32)], axis=0) for g in groups]
    scores = [lax.dot_general(qg[g], kmean[g], NT, precision=HIGHEST, preferred_element_type=f32) for g in groups]
    sel = [_topk_select(scores[g], float(nbp), lane_f, min(MOBA_TOPK, nbp + 1)) for g in groups]
    vb = [vbuf[slot, pl.ds(g, past, stride=KVH_A), :].astype(bf16) for g in groups]
    pb, p_own, l = [], [], []
    for g in groups:
        slope = jnp.exp2(-(rrow + float(GROUP_A * g + 1)))
        sg = s[g] * ATT_SCALE - slope * dist
        selx = jnp.concatenate([jnp.broadcast_to(sel[g][:, n:n + 1], (SUBLANE, MOBA_BLOCK)) for n in range(nbp)],
                               axis=1)
        sg = jnp.where(selx > 0.0, sg, NEG)
        s_own = jnp.sum(qb[g].astype(f32) * kn[:, hs[g]], axis=1, keepdims=True) * ATT_SCALE
        m = jnp.maximum(jnp.max(sg, axis=1, keepdims=True), s_own)
        pb.append(jnp.exp(sg - m).astype(bf16))
        p_own.append(jnp.exp(s_own - m).astype(bf16).astype(f32))
        l.append(jnp.sum(pb[g].astype(f32), axis=1, keepdims=True) + p_own[g])
    pv = [_dot(pb[g], vb[g]) for g in groups]
    outs = []
    for g in groups:
        o = (pv[g] + p_own[g] * vn[:, hs[g]]) / l[g]
        outs.extend(o[r:r + 1, :] for r in range(GROUP_A))
    o_ref[0] = jnp.concatenate(outs, axis=1) * _silu(za_ref[0])


def _moba_decode(proj_s, cache_k, cache_v, page_table, layer):
    nbatch, n_pages = page_table.shape
    past = n_pages * PAGE_SIZE
    n_phys = cache_k.shape[1]
    ck = cache_k.reshape(cache_k.shape[0], n_phys, PAGE_SIZE * KVH_A, HD_A)
    cv = cache_v.reshape(cache_v.shape[0], n_phys, PAGE_SIZE * KVH_A, HD_A)
    p3 = proj_s.reshape(nbatch, 1, PW)
    grid_spec = pltpu.PrefetchScalarGridSpec(
        num_scalar_prefetch=1,
        grid=(nbatch,),
        in_specs=[pl.BlockSpec((1, 1, Q_A), lambda b, pt: (b, 0, OFF_QA // Q_A)),
                  pl.BlockSpec((1, 1, KV_A), lambda b, pt: (b, 0, OFF_KA // KV_A)),
                  pl.BlockSpec((1, 1, KV_A), lambda b, pt: (b, 0, OFF_VA // KV_A)),
                  pl.BlockSpec((1, 1, Q_A), lambda b, pt: (b, 0, OFF_ZA // Q_A)),
                  pl.BlockSpec(memory_space=pl.ANY),
                  pl.BlockSpec(memory_space=pl.ANY)],
        out_specs=pl.BlockSpec((1, 1, Q_A), lambda b, pt: (b, 0, 0)),
        scratch_shapes=[pltpu.VMEM((2, past * KVH_A, HD_A), f32),
                        pltpu.VMEM((2, past * KVH_A, HD_A), f32),
                        pltpu.SemaphoreType.DMA((2, 2))],
    )
    out = pl.pallas_call(
        functools.partial(_moba_decode_kernel, layer=layer, n_pages=n_pages, nbatch=nbatch),
        out_shape=jax.ShapeDtypeStruct((nbatch, 1, Q_A), f32),
        grid_spec=grid_spec,
        compiler_params=_cparams(1, VMEM_BIG),
        name="moba_decode",
    )(page_table.reshape(-1), p3, p3, p3, p3, ck, cv)
    return out.reshape(nbatch, Q_A)


GDN_T = 256
GDN_HB = 8


def _gdn_prompt_kernel(alog_ref, dtb_ref, q_ref, k_ref, v_ref, ab_ref, zb_ref, wq_ref, wk_ref, wv_ref, ng_ref,
                       o_ref, s_out_ref, s_scr, prev_scr, vnew_scr, *, T):
    hp = pl.program_id(0)
    t = pl.program_id(1)
    C = min(DN_CHUNK, T)
    n_chunks = T // C
    shift = int(math.log2(C))
    heads = range(GDN_HB)

    @pl.when(t == 0)
    def _():
        s_scr[...] = jnp.zeros_like(s_scr)
        prev_scr[...] = jnp.zeros_like(prev_scr)
        vnew_scr[...] = jnp.zeros_like(vnew_scr)

    row = lax.broadcasted_iota(jnp.int32, (T, T), 0)
    col = lax.broadcasted_iota(jnp.int32, (T, T), 1)
    same = lax.shift_right_logical(row, shift) == lax.shift_right_logical(col, shift)
    tri_incl = jnp.logical_and(same, row >= col)
    tri_strict = jnp.logical_and(same, row > col)
    ltri = jnp.where(tri_incl, 1.0, 0.0)
    ab = ab_ref[...]
    lane = lax.broadcasted_iota(jnp.int32, (T, LANE), 1)
    r8 = lax.broadcasted_iota(jnp.int32, (SUBLANE, LANE), 0)
    colT = lax.broadcasted_iota(jnp.int32, (DK_B, T), 1)

    reps = T // LANE

    def conv(x_ref, w_ref, hb, which):
        sl = slice(hb * LANE, (hb + 1) * LANE)
        x = x_ref[:, sl]
        w = w_ref[:, sl]
        prev8 = prev_scr[3 * hb + which]
        acc = x * w[CONV_W - 1:CONV_W, :]
        for s in range(1, CONV_W):
            xs = pltpu.roll(x, s, axis=0)
            top = jnp.where(r8 < s, pltpu.roll(prev8, s, axis=0), xs[:SUBLANE])
            xs = jnp.concatenate([top, xs[SUBLANE:]], axis=0)
            acc = acc + xs * w[CONV_W - 1 - s:CONV_W - s, :]
        prev_scr[3 * hb + which] = x[T - SUBLANE:, :]
        return _silu(acc)

    qn, kn, vc, beta, g_rep = [], [], [], [], []
    for hb in heads:
        h = hp * GDN_HB + hb
        qc = conv(q_ref, wq_ref, hb, 0)
        kc = conv(k_ref, wk_ref, hb, 1)
        vc.append(conv(v_ref, wv_ref, hb, 2))
        qn.append(qc * lax.rsqrt(jnp.sum(qc * qc, axis=-1, keepdims=True) + EPS) * (DK_B ** -0.5))
        kn.append(kc * lax.rsqrt(jnp.sum(kc * kc, axis=-1, keepdims=True) + EPS))
        a = jnp.sum(jnp.where(lane == h, ab, 0.0), axis=-1, keepdims=True)
        b = jnp.sum(jnp.where(lane == H_B + h, ab, 0.0), axis=-1, keepdims=True)
        beta.append(_sigmoid(b))
        glog = -jnp.exp(jnp.zeros((T, 1), f32) + alog_ref[h]) * _softplus(a + dtb_ref[h])
        g_rep.append(jnp.broadcast_to(glog, (T, LANE)))

    gam = [jnp.dot(ltri, g_rep[hb], precision=HIGHEST, preferred_element_type=f32) for hb in heads]
    gam_row = [gam[hb].T for hb in heads]
    kb = [kn[hb].astype(bf16) for hb in heads]
    kk = [lax.dot_general(kb[hb], kb[hb], NT, preferred_element_type=f32) for hb in heads]
    qk_raw = [lax.dot_general(qn[hb].astype(bf16), kb[hb], NT, preferred_element_type=f32) for hb in heads]

    decay, pw = [], []
    for hb in heads:
        gam_i = jnp.concatenate([gam[hb]] * reps, axis=1)
        gam_j = jnp.concatenate([gam_row[hb]] * reps, axis=0)
        decay.append(jnp.exp(jnp.where(tri_incl, gam_i - gam_j, -jnp.inf)))
        pw.append(jnp.where(tri_strict, -(jnp.broadcast_to(beta[hb], (T, T)) * kk[hb] * decay[hb]), 0.0))

    x_acc = list(pw)
    pwb = [pw[hb].astype(bf16) for hb in heads]
    for r in range(1, shift):
        pw = [_dot(pwb[hb], pwb[hb]) for hb in heads]
        pwb = [pw[hb].astype(bf16) for hb in heads]
        x_acc = [x_acc[hb] + pw[hb] + _dot(pwb[hb], x_acc[hb].astype(bf16)) for hb in heads]

    egam = [jnp.exp(gam[hb]) for hb in heads]
    rhs = [jnp.concatenate([vc[hb] * beta[hb], kn[hb] * (beta[hb] * egam[hb])], axis=1) for hb in heads]
    sol = [rhs[hb] + _dot(x_acc[hb].astype(bf16), rhs[hb].astype(bf16)) for hb in heads]
    u_val = [sol[hb][:, :DV_B] for hb in heads]
    k_cum_b = [sol[hb][:, DV_B:].astype(bf16) for hb in heads]
    qkb = [(qk_raw[hb] * decay[hb]).astype(bf16) for hb in heads]
    q_dec = [(qn[hb] * egam[hb]).astype(bf16) for hb in heads]
    knT = [kn[hb].T for hb in heads]

    outs = [[] for _ in heads]
    for c in range(n_chunks):
        rs = slice(c * C, (c + 1) * C)
        last = (c + 1) * C - 1
        in_chunk = jnp.logical_and(colT >= c * C, colT <= last)
        s_old = [s_scr[hb] for hb in heads]
        s_b = [s_old[hb].astype(bf16) for hb in heads]
        v_new = [u_val[hb][rs] - _dot(k_cum_b[hb][rs], s_b[hb]) for hb in heads]
        o_state = [_dot(q_dec[hb][rs], s_b[hb]) for hb in heads]
        for hb in heads:
            vnew_scr[hb, rs, :] = v_new[hb]
        vn_b = [vnew_scr[hb].astype(bf16) for hb in heads]
        for hb in heads:
            outs[hb].append(o_state[hb] + _dot(qkb[hb][rs, :], vn_b[hb]))
        for hb in heads:
            w_end = jnp.exp(jnp.where(in_chunk, gam_row[hb][:, last:last + 1] - gam_row[hb], -jnp.inf))
            k_end_t = (knT[hb] * w_end).astype(bf16)
            s_scr[hb] = s_old[hb] * jnp.exp(gam[hb][last:last + 1, :]) + _dot(k_end_t, vn_b[hb])

    for hb in heads:
        sl = slice(hb * LANE, (hb + 1) * LANE)
        o = jnp.concatenate(outs[hb], axis=0)
        o_ref[:, sl] = _rms(o, ng_ref[...]) * _silu(zb_ref[:, sl])
        s_out_ref[hb] = s_scr[hb]


def _gdn_prompt(proj, conv_w, a_log, dt_bias, norm_g):
    L = proj.shape[0]
    T = min(GDN_T, L)
    wb = GDN_HB * LANE
    nq = H_B * DK_B // wb
    return pl.pallas_call(
        functools.partial(_gdn_prompt_kernel, T=T),
        out_shape=(jax.ShapeDtypeStruct((L, V_B), f32), jax.ShapeDtypeStruct((H_B, DK_B, DV_B), f32)),
        grid=(H_B // GDN_HB, L // T),
        in_specs=[pl.BlockSpec(memory_space=pltpu.SMEM),
                  pl.BlockSpec(memory_space=pltpu.SMEM),
                  pl.BlockSpec((T, wb), lambda hp, t: (t, OFF_CONV // wb + hp)),
                  pl.BlockSpec((T, wb), lambda hp, t: (t, OFF_CONV // wb + nq + hp)),
                  pl.BlockSpec((T, wb), lambda hp, t: (t, OFF_CONV // wb + 2 * nq + hp)),
                  pl.BlockSpec((T, LANE), lambda hp, t: (t, OFF_AB // LANE)),
                  pl.BlockSpec((T, wb), lambda hp, t: (t, OFF_ZB // wb + hp)),
                  pl.BlockSpec((CONV_W, wb), lambda hp, t: (0, hp)),
                  pl.BlockSpec((CONV_W, wb), lambda hp, t: (0, nq + hp)),
                  pl.BlockSpec((CONV_W, wb), lambda hp, t: (0, 2 * nq + hp)),
                  pl.BlockSpec((1, DV_B), lambda hp, t: (0, 0))],
        out_specs=(pl.BlockSpec((T, wb), lambda hp, t: (t, hp)),
                   pl.BlockSpec((GDN_HB, DK_B, DV_B), lambda hp, t: (hp, 0, 0))),
        scratch_shapes=[pltpu.VMEM((GDN_HB, DK_B, DV_B), f32),
                        pltpu.VMEM((3 * GDN_HB, SUBLANE, LANE), f32),
                        pltpu.VMEM((GDN_HB, T, DV_B), f32)],
        compiler_params=_cparams(2, VMEM_BIG),
        name="gdn_prompt",
    )(a_log, dt_bias, proj, proj, proj, proj, proj, conv_w, conv_w, conv_w, norm_g.reshape(1, DV_B))


GDN_BB = 8


def _gdn_decode_kernel(x_ref, ab_ref, zb_ref, cs_ref, sd_ref, w_ref, alog_ref, dtb_ref, ng_ref,
                       o_ref, cs_out_ref, sd_out_ref):
    x = x_ref[...]
    w = w_ref[...]
    c0, c1, c2 = cs_ref[:, 0, :], cs_ref[:, 1, :], cs_ref[:, 2, :]
    y = c0 * w[0:1, :] + c1 * w[1:2, :] + c2 * w[2:3, :] + x * w[3:4, :]
    cs_out_ref[:, 0, :] = c1
    cs_out_ref[:, 1, :] = c2
    cs_out_ref[:, 2, :] = x
    y = _silu(y)
    ab = ab_ref[...]
    a = ab[:, 0:H_B]
    beta = _sigmoid(ab[:, H_B:2 * H_B])
    g = -jnp.exp(alog_ref[...]) * _softplus(a + dtb_ref[...])
    eg = jnp.exp(g)
    nq = H_B * DK_B
    outs = []
    for h in range(H_B):
        qc = y[:, h * DK_B:(h + 1) * DK_B]
        kc = y[:, nq + h * DK_B:nq + (h + 1) * DK_B]
        vc = y[:, 2 * nq + h * DV_B:2 * nq + (h + 1) * DV_B]
        qn = qc * lax.rsqrt(jnp.sum(qc * qc, axis=-1, keepdims=True) + EPS) * (DK_B ** -0.5)
        kn = kc * lax.rsqrt(jnp.sum(kc * kc, axis=-1, keepdims=True) + EPS)
        qk = jnp.sum(qn * kn, axis=-1, keepdims=True)
        knT = kn.T
        qnT = qn.T
        rows = []
        for bb in range(GDN_BB):
            s_old = sd_ref[bb, h]
            kcol = knT[:, bb:bb + 1]
            qcol = qnT[:, bb:bb + 1]
            be = beta[bb:bb + 1, h:h + 1]
            e = eg[bb:bb + 1, h:h + 1]
            ks = jnp.sum(kcol * s_old, axis=0, keepdims=True)
            qs = jnp.sum(qcol * s_old, axis=0, keepdims=True)
            v_new = be * vc[bb:bb + 1, :] - (be * e) * ks
            rows.append(e * qs + qk[bb:bb + 1, :] * v_new)
            sd_out_ref[bb, h] = s_old * e + kcol * v_new
        o = jnp.concatenate(rows, axis=0)
        outs.append(_rms(o, ng_ref[...]) * _silu(zb_ref[:, h * DV_B:(h + 1) * DV_B]))
    o_ref[...] = jnp.concatenate(outs, axis=1)


def _gdn_decode(proj_s, state_conv, state_delta, conv_w, a_log, dt_bias, norm_g, layer):
    nbatch = proj_s.shape[0]
    bb = GDN_BB
    return pl.pallas_call(
        _gdn_decode_kernel,
        out_shape=(jax.ShapeDtypeStruct((nbatch, V_B), f32),
                   jax.ShapeDtypeStruct((nbatch, CONV_W - 1, CONV_CH), f32),
                   jax.ShapeDtypeStruct((nbatch, H_B, DK_B, DV_B), f32)),
        grid=(nbatch // bb,),
        in_specs=[pl.BlockSpec((bb, CONV_CH), lambda i: (i, OFF_CONV // CONV_CH)),
                  pl.BlockSpec((bb, LANE), lambda i: (i, OFF_AB // LANE)),
                  pl.BlockSpec((bb, V_B), lambda i: (i, OFF_ZB // V_B)),
                  pl.BlockSpec((None, bb, CONV_W - 1, CONV_CH), lambda i: (layer, i, 0, 0)),
                  pl.BlockSpec((None, bb, H_B, DK_B, DV_B), lambda i: (layer, i, 0, 0, 0)),
                  pl.BlockSpec((CONV_W, CONV_CH), lambda i: (0, 0)),
                  pl.BlockSpec((1, H_B), lambda i: (0, 0)),
                  pl.BlockSpec((1, H_B), lambda i: (0, 0)),
                  pl.BlockSpec((1, DV_B), lambda i: (0, 0))],
        out_specs=(pl.BlockSpec((bb, V_B), lambda i: (i, 0)),
                   pl.BlockSpec((bb, CONV_W - 1, CONV_CH), lambda i: (i, 0, 0)),
                   pl.BlockSpec((bb, H_B, DK_B, DV_B), lambda i: (i, 0, 0, 0))),
        compiler_params=_cparams(1, VMEM_BIG),
        name="gdn_decode",
    )(proj_s, proj_s, proj_s, state_conv, state_delta, conv_w,
      a_log.reshape(1, H_B), dt_bias.reshape(1, H_B), norm_g.reshape(1, DV_B))


S5_JB = W_C // LANE
S5_GPB = LANE // S5_GROUP_CH
S5_SW = S5_GPB * S5_STATE
S5_TT = 512
S5_NSEG = SUBLANE
S5_PAD = 4


def _s5_param_kernel(lr_ref, li_ref, ldt_ref, bre_ref, bim_ref, are_ref, aim_ref, bbre_ref, bbim_ref):
    lr = lr_ref[...]
    li = li_ref[...]
    dt = jnp.exp(ldt_ref[...])
    mag = jnp.exp(lr * dt)
    ang = li * dt
    a_re = mag * jnp.cos(ang)
    a_im = mag * jnp.sin(ang)
    den = lr * lr + li * li
    f_re = ((a_re - 1.0) * lr + a_im * li) / den
    f_im = (a_im * lr - (a_re - 1.0) * li) / den
    are_ref[...] = a_re
    aim_ref[...] = a_im
    fr = f_re[:, None, :]
    fi = f_im[:, None, :]
    bre = bre_ref[...]
    bim = bim_ref[...]
    bbre_ref[...] = fr * bre - fi * bim
    bbim_ref[...] = fr * bim + fi * bre


def _s5_params(lam_re, lam_im, log_dt, b_re, b_im, c_re, c_im, d):
    G, P, CH = S5_GROUPS, S5_STATE, S5_GROUP_CH
    bt_re = jnp.swapaxes(b_re, 1, 2)
    bt_im = jnp.swapaxes(b_im, 1, 2)
    a_re, a_im, bb_re, bb_im = pl.pallas_call(
        _s5_param_kernel,
        out_shape=(jax.ShapeDtypeStruct((G, P), f32), jax.ShapeDtypeStruct((G, P), f32),
                   jax.ShapeDtypeStruct((G, CH, P), f32), jax.ShapeDtypeStruct((G, CH, P), f32)),
        name="s5_params",
    )(lam_re, lam_im, log_dt.reshape(G, 1), bt_re, bt_im)
    eye = jnp.eye(S5_GPB, dtype=f32)

    def bdiag_in(bb):
        x = bb.reshape(S5_JB, S5_GPB, CH, P)
        return jnp.einsum("jacp,ab->jacbp", x, eye).reshape(S5_JB, LANE, S5_SW)

    def bdiag_out(c):
        x = c.reshape(S5_JB, S5_GPB, CH, P)
        return jnp.einsum("jacp,ab->jbpac", x, eye).reshape(S5_JB, S5_SW, LANE)

    wb = jnp.concatenate([bdiag_in(bb_re), bdiag_in(bb_im)], axis=-1).astype(bf16)
    wc_re = bdiag_out(c_re).astype(bf16)
    wc_im = bdiag_out(c_im).astype(bf16)
    return dict(wb=wb, wc_re=wc_re, wc_im=wc_im,
                a_re=a_re.reshape(S5_JB, 1, S5_SW), a_im=a_im.reshape(S5_JB, 1, S5_SW),
                d=d.reshape(S5_JB, 1, LANE))


def _s5_prompt_kernel(u_ref, wb_ref, wcre_ref, wcim_ref, are_ref, aim_ref, d_ref,
                      y_ref, hre_ref, him_ref, a_scr, b_scr, carry_scr, *, TT):
    t = pl.program_id(1)
    nlb = S5_SW // LANE
    seg = TT // S5_NSEG
    pitch = seg + S5_PAD

    @pl.when(t == 0)
    def _():
        carry_scr[...] = jnp.zeros_like(carry_scr)

    u = u_ref[...]
    zpad = jnp.zeros((S5_PAD, LANE), f32)
    u_pad = jnp.concatenate([x for s in range(S5_NSEG) for x in (u[s * seg:(s + 1) * seg], zpad)], axis=0)
    bu = _dot(u_pad.astype(bf16), wb_ref[...])
    for c in range(2 * nlb):
        a_scr[c] = bu[:, c * LANE:(c + 1) * LANE]

    are = are_ref[...]
    aim = aim_ref[...]
    ar8 = [jnp.broadcast_to(are[:, c * LANE:(c + 1) * LANE], (S5_NSEG, LANE)) for c in range(nlb)]
    ai8 = [jnp.broadcast_to(aim[:, c * LANE:(c + 1) * LANE], (S5_NSEG, LANE)) for c in range(nlb)]

    def seg_rows(k):
        return pl.ds(k, S5_NSEG, stride=pitch)

    zero8 = jnp.zeros((S5_NSEG, LANE), f32)
    hr = [zero8] * nlb
    hi = [zero8] * nlb
    for k in range(seg):
        for c in range(nlb):
            nr = ar8[c] * hr[c] - ai8[c] * hi[c] + a_scr[c, seg_rows(k), :]
            ni = ar8[c] * hi[c] + ai8[c] * hr[c] + a_scr[nlb + c, seg_rows(k), :]
            b_scr[c, seg_rows(k), :] = nr
            b_scr[nlb + c, seg_rows(k), :] = ni
            hr[c], hi[c] = nr, ni
    ends = hr + hi

    pr, pi = are, aim
    for _ in range(int(math.log2(seg))):
        pr, pi = pr * pr - pi * pi, 2.0 * pr * pi

    carry = carry_scr[...]
    er, ei = carry[:, :S5_SW], carry[:, S5_SW:]
    rows_r, rows_i = [], []
    for s in range(S5_NSEG):
        rows_r.append(er)
        rows_i.append(ei)
        loc_r = jnp.concatenate([ends[c][s:s + 1, :] for c in range(nlb)], axis=1)
        loc_i = jnp.concatenate([ends[nlb + c][s:s + 1, :] for c in range(nlb)], axis=1)
        er, ei = pr * er - pi * ei + loc_r, pr * ei + pi * er + loc_i
    carry_scr[...] = jnp.concatenate([er, ei], axis=1)
    hre_ref[0] = er
    him_ref[0] = ei
    ein_r = jnp.concatenate(rows_r, axis=0)
    ein_i = jnp.concatenate(rows_i, axis=0)

    pw_r = list(ar8)
    pw_i = list(ai8)
    for k in range(seg):
        for c in range(nlb):
            e_r = ein_r[:, c * LANE:(c + 1) * LANE]
            e_i = ein_i[:, c * LANE:(c + 1) * LANE]
            a_scr[c, seg_rows(k), :] = b_scr[c, seg_rows(k), :] + (pw_r[c] * e_r - pw_i[c] * e_i)
            a_scr[nlb + c, seg_rows(k), :] = b_scr[nlb + c, seg_rows(k), :] + (pw_r[c] * e_i + pw_i[c] * e_r)
            if k + 1 < seg:
                pw_r[c], pw_i[c] = pw_r[c] * ar8[c] - pw_i[c] * ai8[c], pw_r[c] * ai8[c] + pw_i[c] * ar8[c]

    h_re = jnp.concatenate([a_scr[c] for c in range(nlb)], axis=1).astype(bf16)
    h_im = jnp.concatenate([a_scr[nlb + c] for c in range(nlb)], axis=1).astype(bf16)
    y_pad = _dot(h_re, wcre_ref[...]) - _dot(h_im, wcim_ref[...])
    y = jnp.concatenate([y_pad[s * pitch:s * pitch + seg] for s in range(S5_NSEG)], axis=0)
    y_ref[...] = y + d_ref[...] * u


def _s5_prompt(proj, sp):
    L = proj.shape[0]
    TT = min(S5_TT, L)
    jb_spec = lambda shape: pl.BlockSpec((None,) + shape, lambda j, t: (j, 0, 0))
    y, hre, him = pl.pallas_call(
        functools.partial(_s5_prompt_kernel, TT=TT),
        out_shape=(jax.ShapeDtypeStruct((L, W_C), f32),
                   jax.ShapeDtypeStruct((S5_JB, 1, S5_SW), f32),
                   jax.ShapeDtypeStruct((S5_JB, 1, S5_SW), f32)),
        grid=(S5_JB, L // TT),
        in_specs=[pl.BlockSpec((TT, LANE), lambda j, t: (t, OFF_UC // LANE + j)),
                  jb_spec((LANE, 2 * S5_SW)), jb_spec((S5_SW, LANE)), jb_spec((S5_SW, LANE)),
                  jb_spec((1, S5_SW)), jb_spec((1, S5_SW)), jb_spec((1, LANE))],
        out_specs=(pl.BlockSpec((TT, LANE), lambda j, t: (t, j)),
                   pl.BlockSpec((1, 1, S5_SW), lambda j, t: (j, 0, 0)),
                   pl.BlockSpec((1, 1, S5_SW), lambda j, t: (j, 0, 0))),
        scratch_shapes=[pltpu.VMEM((2 * S5_SW // LANE, TT + S5_NSEG * S5_PAD, LANE), f32),
                        pltpu.VMEM((2 * S5_SW // LANE, TT + S5_NSEG * S5_PAD, LANE), f32),
                        pltpu.VMEM((1, 2 * S5_SW), f32)],
        compiler_params=_cparams(2, VMEM_BIG),
        name="s5_prompt",
    )(proj, sp["wb"], sp["wc_re"], sp["wc_im"], sp["a_re"], sp["a_im"], sp["d"])
    return y, hre.reshape(S5_GROUPS, S5_STATE), him.reshape(S5_GROUPS, S5_STATE)


def _s5_decode_kernel(u_ref, h0r_ref, h0i_ref, wb_ref, wcre_ref, wcim_ref, are_ref, aim_ref, d_ref,
                      y_ref, hr_ref, hi_ref):
    u = u_ref[...]
    bu = _dot(u.astype(bf16), wb_ref[...])
    are, aim = are_ref[...], aim_ref[...]
    h0r, h0i = h0r_ref[...], h0i_ref[...]
    hr = bu[:, :S5_SW] + are * h0r - aim * h0i
    hi = bu[:, S5_SW:] + are * h0i + aim * h0r
    hr_ref[...] = hr
    hi_ref[...] = hi
    y_ref[...] = _dot(hr.astype(bf16), wcre_ref[...]) - _dot(hi.astype(bf16), wcim_ref[...]) + d_ref[...] * u


def _s5_decode(proj_s, st_re, st_im, sp, layer):
    nbatch = proj_s.shape[0]
    sre = st_re.reshape(st_re.shape[0], nbatch, S5_GROUPS * S5_STATE)
    sim = st_im.reshape(st_im.shape[0], nbatch, S5_GROUPS * S5_STATE)
    jb_spec = lambda shape: pl.BlockSpec((None,) + shape, lambda j: (j, 0, 0))
    st_spec = pl.BlockSpec((None, nbatch, S5_SW), lambda j: (layer, 0, j))
    y, hr, hi = pl.pallas_call(
        _s5_decode_kernel,
        out_shape=(jax.ShapeDtypeStruct((nbatch, W_C), f32),
                   jax.ShapeDtypeStruct((nbatch, S5_GROUPS * S5_STATE), f32),
                   jax.ShapeDtypeStruct((nbatch, S5_GROUPS * S5_STATE), f32)),
        grid=(S5_JB,),
        in_specs=[pl.BlockSpec((nbatch, LANE), lambda j: (0, OFF_UC // LANE + j)),
                  st_spec, st_spec,
                  jb_spec((LANE, 2 * S5_SW)), jb_spec((S5_SW, LANE)), jb_spec((S5_SW, LANE)),
                  jb_spec((1, S5_SW)), jb_spec((1, S5_SW)), jb_spec((1, LANE))],
        out_specs=(pl.BlockSpec((nbatch, LANE), lambda j: (0, j)),
                   pl.BlockSpec((nbatch, S5_SW), lambda j: (0, j)),
                   pl.BlockSpec((nbatch, S5_SW), lambda j: (0, j))),
        compiler_params=_cparams(1),
        name="s5_decode",
    )(proj_s, sre, sim, sp["wb"], sp["wc_re"], sp["wc_im"], sp["a_re"], sp["a_im"], sp["d"])
    return (y, hr.reshape(nbatch, S5_GROUPS, S5_STATE), hi.reshape(nbatch, S5_GROUPS, S5_STATE))


def _s5_glu_kernel(y_ref, zc_ref, w_ref, b_ref, o_ref):
    y = y_ref[...]
    ge = 0.5 * y * (1.0 + lax.erf(y * (0.5 ** 0.5)))
    gl = _dot(ge.astype(bf16), w_ref[...]) + b_ref[...]
    o_ref[...] = ge * _sigmoid(gl) * _silu(zc_ref[...])


def _s5_glu(y, proj, w_glu, b_glu, tm):
    m = y.shape[0]
    return pl.pallas_call(
        _s5_glu_kernel,
        out_shape=jax.ShapeDtypeStruct((m, W_C), f32),
        grid=(m // tm,),
        in_specs=[pl.BlockSpec((tm, W_C), lambda i: (i, 0)),
                  pl.BlockSpec((tm, W_C), lambda i: (i, OFF_ZC // W_C)),
                  pl.BlockSpec((W_C, W_C), lambda i: (0, 0)),
                  pl.BlockSpec((1, W_C), lambda i: (0, 0))],
        out_specs=pl.BlockSpec((tm, W_C), lambda i: (i, 0)),
        compiler_params=_cparams(1, VMEM_BIG),
        name="s5_glu",
    )(y, proj, w_glu, b_glu.reshape(1, W_C))


def _merge_kernel(oa_ref, ob_ref, oc_ref, ga_ref, gb_ref, gc_ref, wb_ref, o_ref):
    acc = _sigmoid(ga_ref[...]) * _dot(oa_ref[...].astype(bf16), wb_ref[0])
    acc = acc + _sigmoid(gb_ref[...]) * _dot(ob_ref[...].astype(bf16), wb_ref[1])
    acc = acc + _sigmoid(gc_ref[...]) * _dot(oc_ref[...].astype(bf16), wb_ref[2])
    o_ref[...] = acc.astype(bf16)


def _merge(out_a, out_b, out_c, proj, w_branch, tm):
    m = out_a.shape[0]
    act = pl.BlockSpec((tm, Q_A), lambda i: (i, 0))
    gate = lambda k: pl.BlockSpec((tm, D_MODEL), lambda i: (i, OFF_G // D_MODEL + k))
    return pl.pallas_call(
        _merge_kernel,
        out_shape=jax.ShapeDtypeStruct((m, D_MODEL), bf16),
        grid=(m // tm,),
        in_specs=[act, act, act, gate(0), gate(1), gate(2),
                  pl.BlockSpec((3, Q_A, D_MODEL), lambda i: (0, 0, 0), pipeline_mode=pl.Buffered(1))],
        out_specs=pl.BlockSpec((tm, D_MODEL), lambda i: (i, 0)),
        compiler_params=_cparams(1, VMEM_BIG),
        name="merge",
    )(out_a, out_b, out_c, proj, proj, proj, w_branch)


def _post_kernel(x_ref, mg_ref, pe_ref, wout_ref, plew_ref, pleg_ref, png_ref, fng_ref, o_ref, *, final):
    x2 = x_ref[...] + _dot(mg_ref[...], wout_ref[...])
    gate = _sigmoid(_dot(_rms(x2, png_ref[...]).astype(bf16), pleg_ref[...]))
    x3 = x2 + _dot(pe_ref[...].astype(bf16), plew_ref[...]) * gate
    if final:
        x3 = _rms(x3, fng_ref[...])
    o_ref[...] = x3


def _post(x2d, merged, pe, w_out, ple_w, ple_w_gate, ple_norm_g, final_norm_g, final, tm):
    m = x2d.shape[0]
    const = lambda shape: pl.BlockSpec(shape, lambda i: (0, 0), pipeline_mode=pl.Buffered(1))
    return pl.pallas_call(
        functools.partial(_post_kernel, final=final),
        out_shape=jax.ShapeDtypeStruct((m, D_MODEL), f32),
        grid=(m // tm,),
        in_specs=[pl.BlockSpec((tm, D_MODEL), lambda i: (i, 0)),
                  pl.BlockSpec((tm, D_MODEL), lambda i: (i, 0)),
                  pl.BlockSpec((tm, PLE_DIM), lambda i: (i, 0)),
                  const((D_MODEL, D_MODEL)), const((PLE_DIM, D_MODEL)), const((D_MODEL, D_MODEL)),
                  const((1, D_MODEL)), const((1, D_MODEL))],
        out_specs=pl.BlockSpec((tm, D_MODEL), lambda i: (i, 0)),
        compiler_params=_cparams(1, VMEM_BIG),
        name="post",
    )(x2d, merged, pe, w_out, ple_w, ple_w_gate, ple_norm_g.reshape(1, D_MODEL), final_norm_g.reshape(1, D_MODEL))


def _pad_w_in(w_in):
    ab_end = OFF_AB + 2 * H_B
    pad = jnp.zeros((w_in.shape[0], AB_W - 2 * H_B), bf16)
    return jnp.concatenate([w_in[:, :ab_end].astype(bf16), pad, w_in[:, ab_end:].astype(bf16)], axis=1)


def _row_tile(m, pref):
    return pref if m % pref == 0 else m


def _layer_weights(i, norm_g, w_in, dn_conv_w, dn_a_log, dn_dt_bias, dn_norm_g, s5_lambda_re, s5_lambda_im,
                   s5_b_re, s5_b_im, s5_c_re, s5_c_im, s5_d, s5_log_dt, s5_w_glu, s5_b_glu, w_branch, w_out,
                   ple_w, ple_norm_g, ple_w_gate):
    return dict(
        norm_g=norm_g[i], w_pad=_pad_w_in(w_in[i]), conv_w=dn_conv_w[i], a_log=dn_a_log[i], dt_bias=dn_dt_bias[i],
        dn_norm_g=dn_norm_g[i],
        s5=_s5_params(s5_lambda_re[i], s5_lambda_im[i], s5_log_dt[i], s5_b_re[i], s5_b_im[i], s5_c_re[i],
                      s5_c_im[i], s5_d[i]),
        w_glu=s5_w_glu[i].astype(bf16), b_glu=s5_b_glu[i], w_branch=w_branch[i].astype(bf16),
        w_out=w_out[i].astype(bf16), ple_w=ple_w[i].astype(bf16), ple_norm_g=ple_norm_g[i],
        ple_w_gate=ple_w_gate[i].astype(bf16))


def _finish(x2d, proj, out_a, out_b, y_c, pe, lw, final_norm_g, final):
    m = x2d.shape[0]
    out_c = _s5_glu(y_c, proj, lw["w_glu"], lw["b_glu"], _row_tile(m, 512))
    merged = _merge(out_a, out_b, out_c, proj, lw["w_branch"], _row_tile(m, 256))
    return _post(x2d, merged, pe, lw["w_out"], lw["ple_w"], lw["ple_w_gate"], lw["ple_norm_g"], final_norm_g,
                 final, _row_tile(m, 256))


def _prompt_layer(x2d, pe, lw, final_norm_g, final):
    L = x2d.shape[0]
    proj = _inproj(x2d, lw["norm_g"], lw["w_pad"], _row_tile(L, 1024))
    out_a = _moba_prompt(proj)
    out_b, s_new = _gdn_prompt(proj, lw["conv_w"], lw["a_log"], lw["dt_bias"], lw["dn_norm_g"])
    y_c, hre, him = _s5_prompt(proj, lw["s5"])
    x_new = _finish(x2d, proj, out_a, out_b, y_c, pe, lw, final_norm_g, final)
    ka = proj[:, OFF_KA:OFF_KA + KV_A].reshape(1, L, KVH_A, HD_A)
    va = proj[:, OFF_VA:OFF_VA + KV_A].reshape(1, L, KVH_A, HD_A)
    conv_new = proj[L - (CONV_W - 1):, OFF_CONV:OFF_CONV + CONV_CH][None]
    return x_new, (ka, va, conv_new, s_new[None], hre[None], him[None])


def _sample_layer(x2d, pe, lw, final_norm_g, final, layer, cache_k, cache_v, page_table, state_conv, state_delta,
                  state_s5_re, state_s5_im):
    nbatch = x2d.shape[0]
    proj = _inproj(x2d, lw["norm_g"], lw["w_pad"], nbatch)
    out_a = _moba_decode(proj, cache_k, cache_v, page_table, layer)
    out_b, conv_new, s_new = _gdn_decode(proj, state_conv, state_delta, lw["conv_w"], lw["a_log"], lw["dt_bias"],
                                         lw["dn_norm_g"], layer)
    y_c, hr, hi = _s5_decode(proj, state_s5_re, state_s5_im, lw["s5"], layer)
    x_new = _finish(x2d, proj, out_a, out_b, y_c, pe, lw, final_norm_g, final)
    ka = proj[:, OFF_KA:OFF_KA + KV_A].reshape(nbatch, 1, KVH_A, HD_A)
    va = proj[:, OFF_VA:OFF_VA + KV_A].reshape(nbatch, 1, KVH_A, HD_A)
    return x_new, (ka, va, conv_new, s_new, hr, hi)


def kernel(x_prompt, x_sample, cache_k, cache_v, page_table, state_conv, state_delta, state_s5_re, state_s5_im,
           p_prompt, p_sample, norm_g, w_in, dn_conv_w, dn_a_log, dn_dt_bias, dn_norm_g, s5_lambda_re, s5_lambda_im,
           s5_b_re, s5_b_im, s5_c_re, s5_c_im, s5_d, s5_log_dt, s5_w_glu, s5_b_glu, w_branch, w_out, ple_w,
           ple_norm_g, ple_w_gate, final_norm_g):
    depth = w_in.shape[0]
    bp, L, _ = x_prompt.shape
    nbatch = x_sample.shape[0]
    assert bp == 1 and x_sample.shape[1] == 1
    xp = x_prompt.reshape(L, D_MODEL)
    xs = x_sample.reshape(nbatch, D_MODEL)
    new_p, new_s = [], []
    for i in range(depth):
        lw = _layer_weights(i, norm_g, w_in, dn_conv_w, dn_a_log, dn_dt_bias, dn_norm_g, s5_lambda_re, s5_lambda_im,
                            s5_b_re, s5_b_im, s5_c_re, s5_c_im, s5_d, s5_log_dt, s5_w_glu, s5_b_glu, w_branch,
                            w_out, ple_w, ple_norm_g, ple_w_gate)
        final = i == depth - 1
        xp, st = _prompt_layer(xp, p_prompt[i, 0], lw, final_norm_g, final)
        new_p.append(st)
        xs, st = _sample_layer(xs, p_sample[i, :, 0], lw, final_norm_g, final, i, cache_k, cache_v, page_table,
                               state_conv, state_delta, state_s5_re, state_s5_im)
        new_s.append(st)

    def stk(states, j):
        return jnp.stack([s[j] for s in states])

    return (xp.reshape(1, L, D_MODEL), xs.reshape(nbatch, 1, D_MODEL),
            stk(new_p, 0), stk(new_p, 1), stk(new_p, 2), stk(new_p, 3), stk(new_p, 4), stk(new_p, 5),
            stk(new_s, 0), stk(new_s, 1), stk(new_s, 2), stk(new_s, 3), stk(new_s, 4), stk(new_s, 5))
```

```python
import functools
import math

import jax
import jax.numpy as jnp
from jax import lax
from jax.experimental import pallas as pl
from jax.experimental.pallas import tpu as pltpu

f32 = jnp.float32
bf16 = jnp.bfloat16

D_MODEL = 2048
H_A, KVH_A, HD_A = 8, 4, 128
GROUP_A = H_A // KVH_A
MOBA_BLOCK, MOBA_TOPK = 256, 3
MOBA_RC = 256
MOBA_GB = 8
H_B, DK_B, DV_B, CONV_W = 8, 128, 128, 4
DN_CHUNK = 64
W_C, S5_GROUP_CH, S5_GROUPS, S5_STATE = 1024, 16, 64, 64
PLE_DIM = 256
PAGE_SIZE = 128
EPS = 1e-6
Q_A = H_A * HD_A
KV_A = KVH_A * HD_A
V_B = H_B * DV_B
CONV_CH = 2 * H_B * DK_B + V_B
N_GATE = 3 * D_MODEL

OFF_QA = 0
OFF_KA = OFF_QA + Q_A
OFF_VA = OFF_KA + KV_A
OFF_ZA = OFF_VA + KV_A
OFF_CONV = OFF_ZA + Q_A
OFF_AB = OFF_CONV + CONV_CH
AB_W = 1024
OFF_ZB = OFF_AB + AB_W
OFF_UC = OFF_ZB + V_B
OFF_ZC = OFF_UC + W_C
OFF_G = OFF_ZC + W_C
PW = OFF_G + N_GATE

LANE = 128
SUBLANE = 8
TN_IN = 1024
NEG = -1e30
ATT_SCALE = HD_A ** -0.5
LOG2E = math.log2(math.e)
ATT_C1 = ATT_SCALE * LOG2E
VMEM_BIG = 56 * 1024 * 1024
HIGHEST = lax.Precision.HIGHEST
NT = (((1,), (1,)), ((), ()))


def _cparams(n_axes, vmem=None):
    return pltpu.CompilerParams(dimension_semantics=("arbitrary",) * n_axes, vmem_limit_bytes=vmem)


def _sigmoid(x):
    return 1.0 / (1.0 + jnp.exp(-x))


def _silu(x):
    return x * _sigmoid(x)


def _softplus(x):
    return jnp.maximum(x, 0.0) + jnp.log1p(jnp.exp(-jnp.abs(x)))


def _rms(x, g):
    return x * lax.rsqrt(jnp.mean(x * x, axis=-1, keepdims=True) + EPS) * g


def _dot(a, b):
    return jnp.dot(a, b, preferred_element_type=f32)


def _inproj_kernel(x_ref, g_ref, w_ref, o_ref, h_scr):
    @pl.when(pl.program_id(1) == 0)
    def _():
        h_scr[...] = _rms(x_ref[...], g_ref[...]).astype(bf16)

    o_ref[...] = _dot(h_scr[...], w_ref[...])


def _inproj(x2d, g, w_pad, tm):
    m = x2d.shape[0]
    return pl.pallas_call(
        _inproj_kernel,
        out_shape=jax.ShapeDtypeStruct((m, PW), f32),
        grid=(m // tm, PW // TN_IN),
        in_specs=[pl.BlockSpec((tm, D_MODEL), lambda i, n: (i, 0)),
                  pl.BlockSpec((1, D_MODEL), lambda i, n: (0, 0)),
                  pl.BlockSpec((D_MODEL, TN_IN), lambda i, n: (0, n))],
        out_specs=pl.BlockSpec((tm, TN_IN), lambda i, n: (i, n)),
        scratch_shapes=[pltpu.VMEM((tm, D_MODEL), bf16)],
        compiler_params=_cparams(2, VMEM_BIG),
        name="inproj",
    )(x2d, g.reshape(1, D_MODEL), w_pad)


def _topk_select(scores, n_valid, lane_f, n_sel):
    sc = jnp.where(lane_f < n_valid, scores, -jnp.inf)
    sel = jnp.zeros_like(sc)
    for _ in range(n_sel):
        m = jnp.max(sc, axis=-1, keepdims=True)
        idx = jnp.min(jnp.where(sc == m, lane_f, float(LANE)), axis=-1, keepdims=True)
        hit = lane_f == idx
        sel = jnp.maximum(sel, jnp.where(hit, jnp.where(m > -jnp.inf, 1.0, 0.0), 0.0))
        sc = jnp.where(hit, -jnp.inf, sc)
    return sel


def _moba_prompt_kernel(q_ref, k_ref, v_ref, za_ref, o_ref,
                        kb_scr, vb_scr, kmean_scr, bias_scr, off_scr, acc_scr, m_scr, *, nb):
    g = pl.program_id(0)
    i = pl.program_id(1)
    rows = GROUP_A * MOBA_BLOCK
    wide = MOBA_GB * MOBA_BLOCK

    @pl.when(i == 0)
    def _():
        kb_scr[...] = k_ref[...].astype(bf16)
        vb_scr[:, 0:HD_A] = v_ref[...].astype(bf16)
        vb_scr[:, HD_A:] = jnp.ones((nb * MOBA_BLOCK, HD_A), bf16)
        kmean_scr[...] = jnp.zeros_like(kmean_scr)
        for n in range(nb):
            kmean_scr[n:n + 1, :] = jnp.mean(k_ref[n * MOBA_BLOCK:(n + 1) * MOBA_BLOCK, :], axis=0, keepdims=True)
        row = lax.broadcasted_iota(jnp.int32, (rows, wide), 0)
        col = lax.broadcasted_iota(jnp.int32, (rows, wide), 1)
        second = row >= MOBA_BLOCK
        gf = jnp.zeros((rows, wide), f32) + g.astype(f32)
        slope = jnp.exp2(-(2.0 * gf + 1.0) - jnp.where(second, 1.0, 0.0)) * LOG2E
        rin = row - jnp.where(second, MOBA_BLOCK, 0)
        bias_scr[...] = slope * (col - rin).astype(f32)
        off_scr[...] = slope[:, :LANE] * float(MOBA_BLOCK)

    q = q_ref[...]
    q2 = jnp.concatenate([q[:, :HD_A], q[:, HD_A:]], axis=0)
    lane_f = lax.broadcasted_iota(jnp.int32, (rows, LANE), 1).astype(f32)
    scores = lax.dot_general(q2, kmean_scr[...], NT, precision=HIGHEST, preferred_element_type=f32)
    i_f = jnp.zeros((rows, LANE), f32) + i.astype(f32)
    sel = _topk_select(scores, i_f, lane_f, min(MOBA_TOPK, nb))
    selneg = jnp.where(sel > 0.0, 0.0, NEG)
    lhs = jnp.concatenate([q2.astype(bf16), selneg.astype(bf16)], axis=1)

    acc_scr[...] = jnp.zeros_like(acc_scr)
    m_scr[...] = jnp.full_like(m_scr, -jnp.inf)

    def attend(n, width, aux, dist, causal):
        off = pl.multiple_of(n * MOBA_BLOCK, MOBA_BLOCK)
        rhs = jnp.concatenate([kb_scr[pl.ds(off, width), :], aux], axis=1)
        vblk = vb_scr[pl.ds(off, width), :]
        rss = [slice(c * MOBA_RC, (c + 1) * MOBA_RC) for c in range(rows // MOBA_RC)]
        t = [lax.dot_general(lhs[rs], rhs, NT, preferred_element_type=f32) * ATT_C1 + bias_scr[rs, :width]
             for rs in rss]
        if causal:
            c_i = lax.broadcasted_iota(jnp.int32, (MOBA_RC, width), 1)
            r_i = lax.broadcasted_iota(jnp.int32, (MOBA_RC, width), 0)
            t = [jnp.where(c_i <= r_i + (k * MOBA_RC) % MOBA_BLOCK, t[k], NEG) for k in range(len(rss))]
        m_prev = [m_scr[rs] for rs in rss]
        m_cur = [jnp.max(x, axis=1, keepdims=True) for x in t]
        shift = None
        if dist is not None:
            shift = [off_scr[rs] * dist for rs in rss]
            m_cur = [mc - sh for mc, sh in zip(m_cur, shift)]
        m_new = [jnp.maximum(a, b) for a, b in zip(m_prev, m_cur)]
        alpha = [jnp.exp2(a - b) for a, b in zip(m_prev, m_new)]
        sub = m_new if shift is None else [a + b for a, b in zip(m_new, shift)]
        p = [jnp.exp2(x - jnp.concatenate([s] * (width // LANE), axis=1)).astype(bf16) for x, s in zip(t, sub)]
        pv = [_dot(x, vblk) for x in p]
        for k, rs in enumerate(rss):
            acc_scr[rs] = acc_scr[rs] * jnp.concatenate([alpha[k], alpha[k]], axis=1) + pv[k]
            m_scr[rs] = m_new[k]

    def dvec(d):
        return jnp.zeros((MOBA_RC, LANE), f32) + d

    attend(i, MOBA_BLOCK, jnp.zeros((MOBA_BLOCK, LANE), bf16), None, True)

    def past(n, nblk):
        w = nblk * MOBA_BLOCK
        lane_k = lax.broadcasted_iota(jnp.int32, (w, LANE), 1)
        blk_k = lax.shift_right_logical(lax.broadcasted_iota(jnp.int32, (w, LANE), 0), int(math.log2(MOBA_BLOCK)))
        aux = jnp.where(lane_k == n + blk_k, 1.0, 0.0).astype(bf16)
        attend(n, w, aux, dvec((i - n).astype(f32)), False)

    def body(j, c):
        past(MOBA_GB * j, MOBA_GB)
        return c

    ngroups = lax.div(i, MOBA_GB)
    lax.fori_loop(0, ngroups, body, 0)
    done = ngroups * MOBA_GB
    nblk = MOBA_GB // 2
    while nblk >= 1:
        take = lax.rem(lax.div(i, nblk), 2) == 1

        @pl.when(take)
        def _(done=done, nblk=nblk):
            past(done, nblk)

        done = done + jnp.where(take, nblk, 0)
        nblk //= 2

    acc = acc_scr[...]
    o = acc[:, :HD_A] / acc[:, HD_A:]
    o2 = jnp.concatenate([o[:MOBA_BLOCK], o[MOBA_BLOCK:]], axis=1)
    o_ref[...] = o2 * _silu(za_ref[...])


def _moba_prompt(proj):
    L = proj.shape[0]
    nb = L // MOBA_BLOCK
    rows = GROUP_A * MOBA_BLOCK
    wq = GROUP_A * HD_A
    return pl.pallas_call(
        functools.partial(_moba_prompt_kernel, nb=nb),
        out_shape=jax.ShapeDtypeStruct((L, Q_A), f32),
        grid=(KVH_A, nb),
        in_specs=[pl.BlockSpec((MOBA_BLOCK, wq), lambda g, i: (i, OFF_QA // wq + g)),
                  pl.BlockSpec((L, HD_A), lambda g, i: (0, OFF_KA // HD_A + g)),
                  pl.BlockSpec((L, HD_A), lambda g, i: (0, OFF_VA // HD_A + g)),
                  pl.BlockSpec((MOBA_BLOCK, wq), lambda g, i: (i, OFF_ZA // wq + g))],
        out_specs=pl.BlockSpec((MOBA_BLOCK, wq), lambda g, i: (i, g)),
        scratch_shapes=[pltpu.VMEM((L, HD_A), bf16),
                        pltpu.VMEM((L, 2 * HD_A), bf16),
                        pltpu.VMEM((LANE, HD_A), f32),
                        pltpu.VMEM((rows, MOBA_GB * MOBA_BLOCK), f32),
                        pltpu.VMEM((rows, LANE), f32),
                        pltpu.VMEM((rows, 2 * HD_A), f32),
                        pltpu.VMEM((rows, HD_A), f32)],
        compiler_params=_cparams(2, VMEM_BIG),
        name="moba_prompt",
    )(proj, proj, proj, proj)


def _moba_decode_kernel(pt_ref, q_ref, kn_ref, vn_ref, za_ref, ck_ref, cv_ref, o_ref,
                        kbuf, vbuf, sems, *, layer, n_pages, nbatch):
    b = pl.program_id(0)
    past = n_pages * PAGE_SIZE
    nbp = past // MOBA_BLOCK
    prow = PAGE_SIZE * KVH_A

    def copies(bb, slot):
        out = []
        for p in range(n_pages):
            page = pt_ref[bb * n_pages + p]
            out.append(pltpu.make_async_copy(ck_ref.at[layer, page], kbuf.at[slot, pl.ds(p * prow, prow), :],
                                             sems.at[0, slot]))
            out.append(pltpu.make_async_copy(cv_ref.at[layer, page], vbuf.at[slot, pl.ds(p * prow, prow), :],
                                             sems.at[1, slot]))
        return out

    slot = lax.rem(b, 2)

    @pl.when(b == 0)
    def _():
        for c in copies(0, 0):
            c.start()

    @pl.when(b + 1 < nbatch)
    def _():
        for c in copies(b + 1, 1 - slot):
            c.start()

    for c in copies(b, slot):
        c.wait()

    q = q_ref[0]
    kn = kn_ref[0].astype(bf16).astype(f32)
    vn = vn_ref[0].astype(bf16).astype(f32)
    lane_f = lax.broadcasted_iota(jnp.int32, (SUBLANE, LANE), 1).astype(f32)
    rrow = lax.broadcasted_iota(jnp.int32, (SUBLANE, past), 0).astype(f32)
    pos = lax.broadcasted_iota(jnp.int32, (SUBLANE, past), 1)
    dist = (past - pos).astype(f32)
    zpad = jnp.zeros((SUBLANE - GROUP_A, HD_A), f32)
    groups = range(KVH_A)
    hs = [slice(g * HD_A, (g + 1) * HD_A) for g in groups]
    kf = [kbuf[slot, pl.ds(g, past, stride=KVH_A), :] for g in groups]
    qg = [jnp.concatenate([q[:, (GROUP_A * g + r) * HD_A:(GROUP_A * g + r + 1) * HD_A] for r in range(GROUP_A)]
                          + [zpad], axis=0) for g in groups]
    qb = [x.astype(bf16) for x in qg]
    s = [lax.dot_general(qb[g], kf[g].astype(bf16), NT, preferred_element_type=f32) for g in groups]
    kmean = [jnp.concatenate(
        [jnp.mean(kf[g][n * MOBA_BLOCK:(n + 1) * MOBA_BLOCK, :], axis=0, keepdims=True) for n in range(nbp)]
        + [jnp.zeros((LANE - nbp, HD_A), f32)], axis=0) for g in groups]
    scores = [lax.dot_general(qg[g], kmean[g], NT, precision=HIGHEST, preferred_element_type=f32) for g in groups]
    sel = [_topk_select(scores[g], float(nbp), lane_f, min(MOBA_TOPK, nbp + 1)) for g in groups]
    vb = [vbuf[slot, pl.ds(g, past, stride=KVH_A), :].astype(bf16) for g in groups]
    pb, p_own, l = [], [], []
    for g in groups:
        slope = jnp.exp2(-(rrow + float(GROUP_A * g + 1)))
        sg = s[g] * ATT_SCALE - slope * dist
        selx = jnp.concatenate([jnp.broadcast_to(sel[g][:, n:n + 1], (SUBLANE, MOBA_BLOCK)) for n in range(nbp)],
                               axis=1)
        sg = jnp.where(selx > 0.0, sg, NEG)
        s_own = jnp.sum(qb[g].astype(f32) * kn[:, hs[g]], axis=1, keepdims=True) * ATT_SCALE
        m = jnp.maximum(jnp.max(sg, axis=1, keepdims=True), s_own)
        pb.append(jnp.exp(sg - m).astype(bf16))
        p_own.append(jnp.exp(s_own - m).astype(bf16).astype(f32))
        l.append(jnp.sum(pb[g].astype(f32), axis=1, keepdims=True) + p_own[g])
    pv = [_dot(pb[g], vb[g]) for g in groups]
    outs = []
    for g in groups:
        o = (pv[g] + p_own[g] * vn[:, hs[g]]) / l[g]
        outs.extend(o[r:r + 1, :] for r in range(GROUP_A))
    o_ref[0] = jnp.concatenate(outs, axis=1) * _silu(za_ref[0])


def _moba_decode(proj_s, cache_k, cache_v, page_table, layer):
    nbatch, n_pages = page_table.shape
    past = n_pages * PAGE_SIZE
    n_phys = cache_k.shape[1]
    ck = cache_k.reshape(cache_k.shape[0], n_phys, PAGE_SIZE * KVH_A, HD_A)
    cv = cache_v.reshape(cache_v.shape[0], n_phys, PAGE_SIZE * KVH_A, HD_A)
    p3 = proj_s.reshape(nbatch, 1, PW)
    grid_spec = pltpu.PrefetchScalarGridSpec(
        num_scalar_prefetch=1,
        grid=(nbatch,),
        in_specs=[pl.BlockSpec((1, 1, Q_A), lambda b, pt: (b, 0, OFF_QA // Q_A)),
                  pl.BlockSpec((1, 1, KV_A), lambda b, pt: (b, 0, OFF_KA // KV_A)),
                  pl.BlockSpec((1, 1, KV_A), lambda b, pt: (b, 0, OFF_VA // KV_A)),
                  pl.BlockSpec((1, 1, Q_A), lambda b, pt: (b, 0, OFF_ZA // Q_A)),
                  pl.BlockSpec(memory_space=pl.ANY),
                  pl.BlockSpec(memory_space=pl.ANY)],
        out_specs=pl.BlockSpec((1, 1, Q_A), lambda b, pt: (b, 0, 0)),
        scratch_shapes=[pltpu.VMEM((2, past * KVH_A, HD_A), f32),
                        pltpu.VMEM((2, past * KVH_A, HD_A), f32),
                        pltpu.SemaphoreType.DMA((2, 2))],
    )
    out = pl.pallas_call(
        functools.partial(_moba_decode_kernel, layer=layer, n_pages=n_pages, nbatch=nbatch),
        out_shape=jax.ShapeDtypeStruct((nbatch, 1, Q_A), f32),
        grid_spec=grid_spec,
        compiler_params=_cparams(1, VMEM_BIG),
        name="moba_decode",
    )(page_table.reshape(-1), p3, p3, p3, p3, ck, cv)
    return out.reshape(nbatch, Q_A)


GDN_T = 256
GDN_HB = 8


def _gdn_prompt_kernel(alog_ref, dtb_ref, q_ref, k_ref, v_ref, ab_ref, zb_ref, wq_ref, wk_ref, wv_ref, ng_ref,
                       o_ref, s_out_ref, s_scr, prev_scr, vnew_scr, *, T):
    hp = pl.program_id(0)
    t = pl.program_id(1)
    C = min(DN_CHUNK, T)
    n_chunks = T // C
    shift = int(math.log2(C))
    heads = range(GDN_HB)

    @pl.when(t == 0)
    def _():
        s_scr[...] = jnp.zeros_like(s_scr)
        prev_scr[...] = jnp.zeros_like(prev_scr)
        vnew_scr[...] = jnp.zeros_like(vnew_scr)

    row = lax.broadcasted_iota(jnp.int32, (T, T), 0)
    col = lax.broadcasted_iota(jnp.int32, (T, T), 1)
    same = lax.shift_right_logical(row, shift) == lax.shift_right_logical(col, shift)
    tri_incl = jnp.logical_and(same, row >= col)
    tri_strict = jnp.logical_and(same, row > col)
    ltri = jnp.where(tri_incl, 1.0, 0.0)
    ab = ab_ref[...]
    lane = lax.broadcasted_iota(jnp.int32, (T, LANE), 1)
    r8 = lax.broadcasted_iota(jnp.int32, (SUBLANE, LANE), 0)
    colT = lax.broadcasted_iota(jnp.int32, (DK_B, T), 1)

    reps = T // LANE

    def conv(x_ref, w_ref, hb, which):
        sl = slice(hb * LANE, (hb + 1) * LANE)
        x = x_ref[:, sl]
        w = w_ref[:, sl]
        prev8 = prev_scr[3 * hb + which]
        acc = x * w[CONV_W - 1:CONV_W, :]
        for s in range(1, CONV_W):
            xs = pltpu.roll(x, s, axis=0)
            top = jnp.where(r8 < s, pltpu.roll(prev8, s, axis=0), xs[:SUBLANE])
            xs = jnp.concatenate([top, xs[SUBLANE:]], axis=0)
            acc = acc + xs * w[CONV_W - 1 - s:CONV_W - s, :]
        prev_scr[3 * hb + which] = x[T - SUBLANE:, :]
        return _silu(acc)

    qn, kn, vc, beta, g_rep = [], [], [], [], []
    for hb in heads:
        h = hp * GDN_HB + hb
        qc = conv(q_ref, wq_ref, hb, 0)
        kc = conv(k_ref, wk_ref, hb, 1)
        vc.append(conv(v_ref, wv_ref, hb, 2))
        qn.append(qc * lax.rsqrt(jnp.sum(qc * qc, axis=-1, keepdims=True) + EPS) * (DK_B ** -0.5))
        kn.append(kc * lax.rsqrt(jnp.sum(kc * kc, axis=-1, keepdims=True) + EPS))
        a = jnp.sum(jnp.where(lane == h, ab, 0.0), axis=-1, keepdims=True)
        b = jnp.sum(jnp.where(lane == H_B + h, ab, 0.0), axis=-1, keepdims=True)
        beta.append(_sigmoid(b))
        glog = -jnp.exp(jnp.zeros((T, 1), f32) + alog_ref[h]) * _softplus(a + dtb_ref[h])
        g_rep.append(jnp.broadcast_to(glog, (T, LANE)))

    gam = [jnp.dot(ltri, g_rep[hb], precision=HIGHEST, preferred_element_type=f32) for hb in heads]
    gam_row = [gam[hb].T for hb in heads]
    kb = [kn[hb].astype(bf16) for hb in heads]
    kk = [lax.dot_general(kb[hb], kb[hb], NT, preferred_element_type=f32) for hb in heads]
    qk_raw = [lax.dot_general(qn[hb].astype(bf16), kb[hb], NT, preferred_element_type=f32) for hb in heads]

    decay, pw = [], []
    for hb in heads:
        gam_i = jnp.concatenate([gam[hb]] * reps, axis=1)
        gam_j = jnp.concatenate([gam_row[hb]] * reps, axis=0)
        decay.append(jnp.exp(jnp.where(tri_incl, gam_i - gam_j, -jnp.inf)))
        pw.append(jnp.where(tri_strict, -(jnp.broadcast_to(beta[hb], (T, T)) * kk[hb] * decay[hb]), 0.0))

    x_acc = list(pw)
    pwb = [pw[hb].astype(bf16) for hb in heads]
    for r in range(1, shift):
        pw = [_dot(pwb[hb], pwb[hb]) for hb in heads]
        pwb = [pw[hb].astype(bf16) for hb in heads]
        x_acc = [x_acc[hb] + pw[hb] + _dot(pwb[hb], x_acc[hb].astype(bf16)) for hb in heads]

    egam = [jnp.exp(gam[hb]) for hb in heads]
    rhs = [jnp.concatenate([vc[hb] * beta[hb], kn[hb] * (beta[hb] * egam[hb])], axis=1) for hb in heads]
    sol = [rhs[hb] + _dot(x_acc[hb].astype(bf16), rhs[hb].astype(bf16)) for hb in heads]
    u_val = [sol[hb][:, :DV_B] for hb in heads]
    k_cum_b = [sol[hb][:, DV_B:].astype(bf16) for hb in heads]
    qkb = [(qk_raw[hb] * decay[hb]).astype(bf16) for hb in heads]
    q_dec = [(qn[hb] * egam[hb]).astype(bf16) for hb in heads]
    knT = [kn[hb].T for hb in heads]

    outs = [[] for _ in heads]
    for c in range(n_chunks):
        rs = slice(c * C, (c + 1) * C)
        last = (c + 1) * C - 1
        in_chunk = jnp.logical_and(colT >= c * C, colT <= last)
        s_old = [s_scr[hb] for hb in heads]
        s_b = [s_old[hb].astype(bf16) for hb in heads]
        v_new = [u_val[hb][rs] - _dot(k_cum_b[hb][rs], s_b[hb]) for hb in heads]
        o_state = [_dot(q_dec[hb][rs], s_b[hb]) for hb in heads]
        for hb in heads:
            vnew_scr[hb, rs, :] = v_new[hb]
        vn_b = [vnew_scr[hb].astype(bf16) for hb in heads]
        for hb in heads:
            outs[hb].append(o_state[hb] + _dot(qkb[hb][rs, :], vn_b[hb]))
        for hb in heads:
            w_end = jnp.exp(jnp.where(in_chunk, gam_row[hb][:, last:last + 1] - gam_row[hb], -jnp.inf))
            k_end_t = (knT[hb] * w_end).astype(bf16)
            s_scr[hb] = s_old[hb] * jnp.exp(gam[hb][last:last + 1, :]) + _dot(k_end_t, vn_b[hb])

    for hb in heads:
        sl = slice(hb * LANE, (hb + 1) * LANE)
        o = jnp.concatenate(outs[hb], axis=0)
        o_ref[:, sl] = _rms(o, ng_ref[...]) * _silu(zb_ref[:, sl])
        s_out_ref[hb] = s_scr[hb]


def _gdn_prompt(proj, conv_w, a_log, dt_bias, norm_g):
    L = proj.shape[0]
    T = min(GDN_T, L)
    wb = GDN_HB * LANE
    nq = H_B * DK_B // wb
    return pl.pallas_call(
        functools.partial(_gdn_prompt_kernel, T=T),
        out_shape=(jax.ShapeDtypeStruct((L, V_B), f32), jax.ShapeDtypeStruct((H_B, DK_B, DV_B), f32)),
        grid=(H_B // GDN_HB, L // T),
        in_specs=[pl.BlockSpec(memory_space=pltpu.SMEM),
                  pl.BlockSpec(memory_space=pltpu.SMEM),
                  pl.BlockSpec((T, wb), lambda hp, t: (t, OFF_CONV // wb + hp)),
                  pl.BlockSpec((T, wb), lambda hp, t: (t, OFF_CONV // wb + nq + hp)),
                  pl.BlockSpec((T, wb), lambda hp, t: (t, OFF_CONV // wb + 2 * nq + hp)),
                  pl.BlockSpec((T, LANE), lambda hp, t: (t, OFF_AB // LANE)),
                  pl.BlockSpec((T, wb), lambda hp, t: (t, OFF_ZB // wb + hp)),
                  pl.BlockSpec((CONV_W, wb), lambda hp, t: (0, hp)),
                  pl.BlockSpec((CONV_W, wb), lambda hp, t: (0, nq + hp)),
                  pl.BlockSpec((CONV_W, wb), lambda hp, t: (0, 2 * nq + hp)),
                  pl.BlockSpec((1, DV_B), lambda hp, t: (0, 0))],
        out_specs=(pl.BlockSpec((T, wb), lambda hp, t: (t, hp)),
                   pl.BlockSpec((GDN_HB, DK_B, DV_B), lambda hp, t: (hp, 0, 0))),
        scratch_shapes=[pltpu.VMEM((GDN_HB, DK_B, DV_B), f32),
                        pltpu.VMEM((3 * GDN_HB, SUBLANE, LANE), f32),
                        pltpu.VMEM((GDN_HB, T, DV_B), f32)],
        compiler_params=_cparams(2, VMEM_BIG),
        name="gdn_prompt",
    )(a_log, dt_bias, proj, proj, proj, proj, proj, conv_w, conv_w, conv_w, norm_g.reshape(1, DV_B))


GDN_BB = 8


def _gdn_decode_kernel(x_ref, ab_ref, zb_ref, cs_ref, sd_ref, w_ref, alog_ref, dtb_ref, ng_ref,
                       o_ref, cs_out_ref, sd_out_ref):
    x = x_ref[...]
    w = w_ref[...]
    c0, c1, c2 = cs_ref[:, 0, :], cs_ref[:, 1, :], cs_ref[:, 2, :]
    y = c0 * w[0:1, :] + c1 * w[1:2, :] + c2 * w[2:3, :] + x * w[3:4, :]
    cs_out_ref[:, 0, :] = c1
    cs_out_ref[:, 1, :] = c2
    cs_out_ref[:, 2, :] = x
    y = _silu(y)
    ab = ab_ref[...]
    a = ab[:, 0:H_B]
    beta = _sigmoid(ab[:, H_B:2 * H_B])
    g = -jnp.exp(alog_ref[...]) * _softplus(a + dtb_ref[...])
    eg = jnp.exp(g)
    nq = H_B * DK_B
    outs = []
    for h in range(H_B):
        qc = y[:, h * DK_B:(h + 1) * DK_B]
        kc = y[:, nq + h * DK_B:nq + (h + 1) * DK_B]
        vc = y[:, 2 * nq + h * DV_B:2 * nq + (h + 1) * DV_B]
        qn = qc * lax.rsqrt(jnp.sum(qc * qc, axis=-1, keepdims=True) + EPS) * (DK_B ** -0.5)
        kn = kc * lax.rsqrt(jnp.sum(kc * kc, axis=-1, keepdims=True) + EPS)
        qk = jnp.sum(qn * kn, axis=-1, keepdims=True)
        knT = kn.T
        qnT = qn.T
        rows = []
        for bb in range(GDN_BB):
            s_old = sd_ref[bb, h]
            kcol = knT[:, bb:bb + 1]
            qcol = qnT[:, bb:bb + 1]
            be = beta[bb:bb + 1, h:h + 1]
            e = eg[bb:bb + 1, h:h + 1]
            ks = jnp.sum(kcol * s_old, axis=0, keepdims=True)
            qs = jnp.sum(qcol * s_old, axis=0, keepdims=True)
            v_new = be * vc[bb:bb + 1, :] - (be * e) * ks
            rows.append(e * qs + qk[bb:bb + 1, :] * v_new)
            sd_out_ref[bb, h] = s_old * e + kcol * v_new
        o = jnp.concatenate(rows, axis=0)
        outs.append(_rms(o, ng_ref[...]) * _silu(zb_ref[:, h * DV_B:(h + 1) * DV_B]))
    o_ref[...] = jnp.concatenate(outs, axis=1)


def _gdn_decode(proj_s, state_conv, state_delta, conv_w, a_log, dt_bias, norm_g, layer):
    nbatch = proj_s.shape[0]
    bb = GDN_BB
    return pl.pallas_call(
        _gdn_decode_kernel,
        out_shape=(jax.ShapeDtypeStruct((nbatch, V_B), f32),
                   jax.ShapeDtypeStruct((nbatch, CONV_W - 1, CONV_CH), f32),
                   jax.ShapeDtypeStruct((nbatch, H_B, DK_B, DV_B), f32)),
        grid=(nbatch // bb,),
        in_specs=[pl.BlockSpec((bb, CONV_CH), lambda i: (i, OFF_CONV // CONV_CH)),
                  pl.BlockSpec((bb, LANE), lambda i: (i, OFF_AB // LANE)),
                  pl.BlockSpec((bb, V_B), lambda i: (i, OFF_ZB // V_B)),
                  pl.BlockSpec((None, bb, CONV_W - 1, CONV_CH), lambda i: (layer, i, 0, 0)),
                  pl.BlockSpec((None, bb, H_B, DK_B, DV_B), lambda i: (layer, i, 0, 0, 0)),
                  pl.BlockSpec((CONV_W, CONV_CH), lambda i: (0, 0)),
                  pl.BlockSpec((1, H_B), lambda i: (0, 0)),
                  pl.BlockSpec((1, H_B), lambda i: (0, 0)),
                  pl.BlockSpec((1, DV_B), lambda i: (0, 0))],
        out_specs=(pl.BlockSpec((bb, V_B), lambda i: (i, 0)),
                   pl.BlockSpec((bb, CONV_W - 1, CONV_CH), lambda i: (i, 0, 0)),
                   pl.BlockSpec((bb, H_B, DK_B, DV_B), lambda i: (i, 0, 0, 0))),
        compiler_params=_cparams(1, VMEM_BIG),
        name="gdn_decode",
    )(proj_s, proj_s, proj_s, state_conv, state_delta, conv_w,
      a_log.reshape(1, H_B), dt_bias.reshape(1, H_B), norm_g.reshape(1, DV_B))


S5_JB = W_C // LANE
S5_GPB = LANE // S5_GROUP_CH
S5_SW = S5_GPB * S5_STATE
S5_TT = 512
S5_NSEG = SUBLANE
S5_PAD = 4


def _s5_param_kernel(lr_ref, li_ref, ldt_ref, bre_ref, bim_ref, are_ref, aim_ref, bbre_ref, bbim_ref):
    lr = lr_ref[...]
    li = li_ref[...]
    dt = jnp.exp(ldt_ref[...])
    mag = jnp.exp(lr * dt)
    ang = li * dt
    a_re = mag * jnp.cos(ang)
    a_im = mag * jnp.sin(ang)
    den = lr * lr + li * li
    f_re = ((a_re - 1.0) * lr + a_im * li) / den
    f_im = (a_im * lr - (a_re - 1.0) * li) / den
    are_ref[...] = a_re
    aim_ref[...] = a_im
    fr = f_re[:, None, :]
    fi = f_im[:, None, :]
    bre = bre_ref[...]
    bim = bim_ref[...]
    bbre_ref[...] = fr * bre - fi * bim
    bbim_ref[...] = fr * bim + fi * bre


def _s5_params(lam_re, lam_im, log_dt, b_re, b_im, c_re, c_im, d):
    G, P, CH = S5_GROUPS, S5_STATE, S5_GROUP_CH
    bt_re = jnp.swapaxes(b_re, 1, 2)
    bt_im = jnp.swapaxes(b_im, 1, 2)
    a_re, a_im, bb_re, bb_im = pl.pallas_call(
        _s5_param_kernel,
        out_shape=(jax.ShapeDtypeStruct((G, P), f32), jax.ShapeDtypeStruct((G, P), f32),
                   jax.ShapeDtypeStruct((G, CH, P), f32), jax.ShapeDtypeStruct((G, CH, P), f32)),
        name="s5_params",
    )(lam_re, lam_im, log_dt.reshape(G, 1), bt_re, bt_im)
    eye = jnp.eye(S5_GPB, dtype=f32)

    def bdiag_in(bb):
        x = bb.reshape(S5_JB, S5_GPB, CH, P)
        return jnp.einsum("jacp,ab->jacbp", x, eye).reshape(S5_JB, LANE, S5_SW)

    def bdiag_out(c):
        x = c.reshape(S5_JB, S5_GPB, CH, P)
        return jnp.einsum("jacp,ab->jbpac", x, eye).reshape(S5_JB, S5_SW, LANE)

    wb = jnp.concatenate([bdiag_in(bb_re), bdiag_in(bb_im)], axis=-1).astype(bf16)
    wc_re = bdiag_out(c_re).astype(bf16)
    wc_im = bdiag_out(c_im).astype(bf16)
    return dict(wb=wb, wc_re=wc_re, wc_im=wc_im,
                a_re=a_re.reshape(S5_JB, 1, S5_SW), a_im=a_im.reshape(S5_JB, 1, S5_SW),
                d=d.reshape(S5_JB, 1, LANE))


def _s5_prompt_kernel(u_ref, wb_ref, wcre_ref, wcim_ref, are_ref, aim_ref, d_ref,
                      y_ref, hre_ref, him_ref, a_scr, b_scr, carry_scr, *, TT):
    t = pl.program_id(1)
    nlb = S5_SW // LANE
    seg = TT // S5_NSEG
    pitch = seg + S5_PAD

    @pl.when(t == 0)
    def _():
        carry_scr[...] = jnp.zeros_like(carry_scr)

    u = u_ref[...]
    zpad = jnp.zeros((S5_PAD, LANE), f32)
    u_pad = jnp.concatenate([x for s in range(S5_NSEG) for x in (u[s * seg:(s + 1) * seg], zpad)], axis=0)
    bu = _dot(u_pad.astype(bf16), wb_ref[...])
    for c in range(2 * nlb):
        a_scr[c] = bu[:, c * LANE:(c + 1) * LANE]

    are = are_ref[...]
    aim = aim_ref[...]
    ar8 = [jnp.broadcast_to(are[:, c * LANE:(c + 1) * LANE], (S5_NSEG, LANE)) for c in range(nlb)]
    ai8 = [jnp.broadcast_to(aim[:, c * LANE:(c + 1) * LANE], (S5_NSEG, LANE)) for c in range(nlb)]

    def seg_rows(k):
        return pl.ds(k, S5_NSEG, stride=pitch)

    zero8 = jnp.zeros((S5_NSEG, LANE), f32)
    hr = [zero8] * nlb
    hi = [zero8] * nlb
    for k in range(seg):
        for c in range(nlb):
            nr = ar8[c] * hr[c] - ai8[c] * hi[c] + a_scr[c, seg_rows(k), :]
            ni = ar8[c] * hi[c] + ai8[c] * hr[c] + a_scr[nlb + c, seg_rows(k), :]
            b_scr[c, seg_rows(k), :] = nr
            b_scr[nlb + c, seg_rows(k), :] = ni
            hr[c], hi[c] = nr, ni
    ends = hr + hi

    pr, pi = are, aim
    for _ in range(int(math.log2(seg))):
        pr, pi = pr * pr - pi * pi, 2.0 * pr * pi

    carry = carry_scr[...]
    er, ei = carry[:, :S5_SW], carry[:, S5_SW:]
    rows_r, rows_i = [], []
    for s in range(S5_NSEG):
        rows_r.append(er)
        rows_i.append(ei)
        loc_r = jnp.concatenate([ends[c][s:s + 1, :] for c in range(nlb)], axis=1)
        loc_i = jnp.concatenate([ends[nlb + c][s:s + 1, :] for c in range(nlb)], axis=1)
        er, ei = pr * er - pi * ei + loc_r, pr * ei + pi * er + loc_i
    carry_scr[...] = jnp.concatenate([er, ei], axis=1)
    hre_ref[0] = er
    him_ref[0] = ei
    ein_r = jnp.concatenate(rows_r, axis=0)
    ein_i = jnp.concatenate(rows_i, axis=0)

    pw_r = list(ar8)
    pw_i = list(ai8)
    for k in range(seg):
        for c in range(nlb):
            e_r = ein_r[:, c * LANE:(c + 1) * LANE]
            e_i = ein_i[:, c * LANE:(c + 1) * LANE]
            a_scr[c, seg_rows(k), :] = b_scr[c, seg_rows(k), :] + (pw_r[c] * e_r - pw_i[c] * e_i)
            a_scr[nlb + c, seg_rows(k), :] = b_scr[nlb + c, seg_rows(k), :] + (pw_r[c] * e_i + pw_i[c] * e_r)
            if k + 1 < seg:
                pw_r[c], pw_i[c] = pw_r[c] * ar8[c] - pw_i[c] * ai8[c], pw_r[c] * ai8[c] + pw_i[c] * ar8[c]

    h_re = jnp.concatenate([a_scr[c] for c in range(nlb)], axis=1).astype(bf16)
    h_im = jnp.concatenate([a_scr[nlb + c] for c in range(nlb)], axis=1).astype(bf16)
    y_pad = _dot(h_re, wcre_ref[...]) - _dot(h_im, wcim_ref[...])
    y = jnp.concatenate([y_pad[s * pitch:s * pitch + seg] for s in range(S5_NSEG)], axis=0)
    y_ref[...] = y + d_ref[...] * u


def _s5_prompt(proj, sp):
    L = proj.shape[0]
    TT = min(S5_TT, L)
    jb_spec = lambda shape: pl.BlockSpec((None,) + shape, lambda j, t: (j, 0, 0))
    y, hre, him = pl.pallas_call(
        functools.partial(_s5_prompt_kernel, TT=TT),
        out_shape=(jax.ShapeDtypeStruct((L, W_C), f32),
                   jax.ShapeDtypeStruct((S5_JB, 1, S5_SW), f32),
                   jax.ShapeDtypeStruct((S5_JB, 1, S5_SW), f32)),
        grid=(S5_JB, L // TT),
        in_specs=[pl.BlockSpec((TT, LANE), lambda j, t: (t, OFF_UC // LANE + j)),
                  jb_spec((LANE, 2 * S5_SW)), jb_spec((S5_SW, LANE)), jb_spec((S5_SW, LANE)),
                  jb_spec((1, S5_SW)), jb_spec((1, S5_SW)), jb_spec((1, LANE))],
        out_specs=(pl.BlockSpec((TT, LANE), lambda j, t: (t, j)),
                   pl.BlockSpec((1, 1, S5_SW), lambda j, t: (j, 0, 0)),
                   pl.BlockSpec((1, 1, S5_SW), lambda j, t: (j, 0, 0))),
        scratch_shapes=[pltpu.VMEM((2 * S5_SW // LANE, TT + S5_NSEG * S5_PAD, LANE), f32),
                        pltpu.VMEM((2 * S5_SW // LANE, TT + S5_NSEG * S5_PAD, LANE), f32),
                        pltpu.VMEM((1, 2 * S5_SW), f32)],
        compiler_params=_cparams(2, VMEM_BIG),
        name="s5_prompt",
    )(proj, sp["wb"], sp["wc_re"], sp["wc_im"], sp["a_re"], sp["a_im"], sp["d"])
    return y, hre.reshape(S5_GROUPS, S5_STATE), him.reshape(S5_GROUPS, S5_STATE)


def _s5_decode_kernel(u_ref, h0r_ref, h0i_ref, wb_ref, wcre_ref, wcim_ref, are_ref, aim_ref, d_ref,
                      y_ref, hr_ref, hi_ref):
    u = u_ref[...]
    bu = _dot(u.astype(bf16), wb_ref[...])
    are, aim = are_ref[...], aim_ref[...]
    h0r, h0i = h0r_ref[...], h0i_ref[...]
    hr = bu[:, :S5_SW] + are * h0r - aim * h0i
    hi = bu[:, S5_SW:] + are * h0i + aim * h0r
    hr_ref[...] = hr
    hi_ref[...] = hi
    y_ref[...] = _dot(hr.astype(bf16), wcre_ref[...]) - _dot(hi.astype(bf16), wcim_ref[...]) + d_ref[...] * u


def _s5_decode(proj_s, st_re, st_im, sp, layer):
    nbatch = proj_s.shape[0]
    sre = st_re.reshape(st_re.shape[0], nbatch, S5_GROUPS * S5_STATE)
    sim = st_im.reshape(st_im.shape[0], nbatch, S5_GROUPS * S5_STATE)
    jb_spec = lambda shape: pl.BlockSpec((None,) + shape, lambda j: (j, 0, 0))
    st_spec = pl.BlockSpec((None, nbatch, S5_SW), lambda j: (layer, 0, j))
    y, hr, hi = pl.pallas_call(
        _s5_decode_kernel,
        out_shape=(jax.ShapeDtypeStruct((nbatch, W_C), f32),
                   jax.ShapeDtypeStruct((nbatch, S5_GROUPS * S5_STATE), f32),
                   jax.ShapeDtypeStruct((nbatch, S5_GROUPS * S5_STATE), f32)),
        grid=(S5_JB,),
        in_specs=[pl.BlockSpec((nbatch, LANE), lambda j: (0, OFF_UC // LANE + j)),
                  st_spec, st_spec,
                  jb_spec((LANE, 2 * S5_SW)), jb_spec((S5_SW, LANE)), jb_spec((S5_SW, LANE)),
                  jb_spec((1, S5_SW)), jb_spec((1, S5_SW)), jb_spec((1, LANE))],
        out_specs=(pl.BlockSpec((nbatch, LANE), lambda j: (0, j)),
                   pl.BlockSpec((nbatch, S5_SW), lambda j: (0, j)),
                   pl.BlockSpec((nbatch, S5_SW), lambda j: (0, j))),
        compiler_params=_cparams(1),
        name="s5_decode",
    )(proj_s, sre, sim, sp["wb"], sp["wc_re"], sp["wc_im"], sp["a_re"], sp["a_im"], sp["d"])
    return (y, hr.reshape(nbatch, S5_GROUPS, S5_STATE), hi.reshape(nbatch, S5_GROUPS, S5_STATE))


def _s5_glu_kernel(y_ref, zc_ref, w_ref, b_ref, o_ref):
    y = y_ref[...]
    ge = 0.5 * y * (1.0 + lax.erf(y * (0.5 ** 0.5)))
    gl = _dot(ge.astype(bf16), w_ref[...]) + b_ref[...]
    o_ref[...] = ge * _sigmoid(gl) * _silu(zc_ref[...])


def _s5_glu(y, proj, w_glu, b_glu, tm):
    m = y.shape[0]
    return pl.pallas_call(
        _s5_glu_kernel,
        out_shape=jax.ShapeDtypeStruct((m, W_C), f32),
        grid=(m // tm,),
        in_specs=[pl.BlockSpec((tm, W_C), lambda i: (i, 0)),
                  pl.BlockSpec((tm, W_C), lambda i: (i, OFF_ZC // W_C)),
                  pl.BlockSpec((W_C, W_C), lambda i: (0, 0)),
                  pl.BlockSpec((1, W_C), lambda i: (0, 0))],
        out_specs=pl.BlockSpec((tm, W_C), lambda i: (i, 0)),
        compiler_params=_cparams(1, VMEM_BIG),
        name="s5_glu",
    )(y, proj, w_glu, b_glu.reshape(1, W_C))


def _merge_kernel(oa_ref, ob_ref, oc_ref, ga_ref, gb_ref, gc_ref, wb_ref, o_ref):
    acc = _sigmoid(ga_ref[...]) * _dot(oa_ref[...].astype(bf16), wb_ref[0])
    acc = acc + _sigmoid(gb_ref[...]) * _dot(ob_ref[...].astype(bf16), wb_ref[1])
    acc = acc + _sigmoid(gc_ref[...]) * _dot(oc_ref[...].astype(bf16), wb_ref[2])
    o_ref[...] = acc.astype(bf16)


def _merge(out_a, out_b, out_c, proj, w_branch, tm):
    m = out_a.shape[0]
    act = pl.BlockSpec((tm, Q_A), lambda i: (i, 0))
    gate = lambda k: pl.BlockSpec((tm, D_MODEL), lambda i: (i, OFF_G // D_MODEL + k))
    return pl.pallas_call(
        _merge_kernel,
        out_shape=jax.ShapeDtypeStruct((m, D_MODEL), bf16),
        grid=(m // tm,),
        in_specs=[act, act, act, gate(0), gate(1), gate(2),
                  pl.BlockSpec((3, Q_A, D_MODEL), lambda i: (0, 0, 0), pipeline_mode=pl.Buffered(1))],
        out_specs=pl.BlockSpec((tm, D_MODEL), lambda i: (i, 0)),
        compiler_params=_cparams(1, VMEM_BIG),
        name="merge",
    )(out_a, out_b, out_c, proj, proj, proj, w_branch)


def _post_kernel(x_ref, mg_ref, pe_ref, wout_ref, plew_ref, pleg_ref, png_ref, fng_ref, o_ref, *, final):
    x2 = x_ref[...] + _dot(mg_ref[...], wout_ref[...])
    gate = _sigmoid(_dot(_rms(x2, png_ref[...]).astype(bf16), pleg_ref[...]))
    x3 = x2 + _dot(pe_ref[...].astype(bf16), plew_ref[...]) * gate
    if final:
        x3 = _rms(x3, fng_ref[...])
    o_ref[...] = x3


def _post(x2d, merged, pe, w_out, ple_w, ple_w_gate, ple_norm_g, final_norm_g, final, tm):
    m = x2d.shape[0]
    const = lambda shape: pl.BlockSpec(shape, lambda i: (0, 0), pipeline_mode=pl.Buffered(1))
    return pl.pallas_call(
        functools.partial(_post_kernel, final=final),
        out_shape=jax.ShapeDtypeStruct((m, D_MODEL), f32),
        grid=(m // tm,),
        in_specs=[pl.BlockSpec((tm, D_MODEL), lambda i: (i, 0)),
                  pl.BlockSpec((tm, D_MODEL), lambda i: (i, 0)),
                  pl.BlockSpec((tm, PLE_DIM), lambda i: (i, 0)),
                  const((D_MODEL, D_MODEL)), const((PLE_DIM, D_MODEL)), const((D_MODEL, D_MODEL)),
                  const((1, D_MODEL)), const((1, D_MODEL))],
        out_specs=pl.BlockSpec((tm, D_MODEL), lambda i: (i, 0)),
        compiler_params=_cparams(1, VMEM_BIG),
        name="post",
    )(x2d, merged, pe, w_out, ple_w, ple_w_gate, ple_norm_g.reshape(1, D_MODEL), final_norm_g.reshape(1, D_MODEL))


def _pad_w_in(w_in):
    ab_end = OFF_AB + 2 * H_B
    pad = jnp.zeros((w_in.shape[0], AB_W - 2 * H_B), bf16)
    return jnp.concatenate([w_in[:, :ab_end].astype(bf16), pad, w_in[:, ab_end:].astype(bf16)], axis=1)


def _row_tile(m, pref):
    return pref if m % pref == 0 else m


def _layer_weights(i, norm_g, w_in, dn_conv_w, dn_a_log, dn_dt_bias, dn_norm_g, s5_lambda_re, s5_lambda_im,
                   s5_b_re, s5_b_im, s5_c_re, s5_c_im, s5_d, s5_log_dt, s5_w_glu, s5_b_glu, w_branch, w_out,
                   ple_w, ple_norm_g, ple_w_gate):
    return dict(
        norm_g=norm_g[i], w_pad=_pad_w_in(w_in[i]), conv_w=dn_conv_w[i], a_log=dn_a_log[i], dt_bias=dn_dt_bias[i],
        dn_norm_g=dn_norm_g[i],
        s5=_s5_params(s5_lambda_re[i], s5_lambda_im[i], s5_log_dt[i], s5_b_re[i], s5_b_im[i], s5_c_re[i],
                      s5_c_im[i], s5_d[i]),
        w_glu=s5_w_glu[i].astype(bf16), b_glu=s5_b_glu[i], w_branch=w_branch[i].astype(bf16),
        w_out=w_out[i].astype(bf16), ple_w=ple_w[i].astype(bf16), ple_norm_g=ple_norm_g[i],
        ple_w_gate=ple_w_gate[i].astype(bf16))


def _finish(x2d, proj, out_a, out_b, y_c, pe, lw, final_norm_g, final):
    m = x2d.shape[0]
    out_c = _s5_glu(y_c, proj, lw["w_glu"], lw["b_glu"], _row_tile(m, 512))
    merged = _merge(out_a, out_b, out_c, proj, lw["w_branch"], _row_tile(m, 256))
    return _post(x2d, merged, pe, lw["w_out"], lw["ple_w"], lw["ple_w_gate"], lw["ple_norm_g"], final_norm_g,
                 final, _row_tile(m, 256))


def _prompt_layer(x2d, pe, lw, final_norm_g, final):
    L = x2d.shape[0]
    proj = _inproj(x2d, lw["norm_g"], lw["w_pad"], _row_tile(L, 1024))
    out_a = _moba_prompt(proj)
    out_b, s_new = _gdn_prompt(proj, lw["conv_w"], lw["a_log"], lw["dt_bias"], lw["dn_norm_g"])
    y_c, hre, him = _s5_prompt(proj, lw["s5"])
    x_new = _finish(x2d, proj, out_a, out_b, y_c, pe, lw, final_norm_g, final)
    ka = proj[:, OFF_KA:OFF_KA + KV_A].reshape(1, L, KVH_A, HD_A)
    va = proj[:, OFF_VA:OFF_VA + KV_A].reshape(1, L, KVH_A, HD_A)
    conv_new = proj[L - (CONV_W - 1):, OFF_CONV:OFF_CONV + CONV_CH][None]
    return x_new, (ka, va, conv_new, s_new[None], hre[None], him[None])


def _sample_layer(x2d, pe, lw, final_norm_g, final, layer, cache_k, cache_v, page_table, state_conv, state_delta,
                  state_s5_re, state_s5_im):
    nbatch = x2d.shape[0]
    proj = _inproj(x2d, lw["norm_g"], lw["w_pad"], nbatch)
    out_a = _moba_decode(proj, cache_k, cache_v, page_table, layer)
    out_b, conv_new, s_new = _gdn_decode(proj, state_conv, state_delta, lw["conv_w"], lw["a_log"], lw["dt_bias"],
                                         lw["dn_norm_g"], layer)
    y_c, hr, hi = _s5_decode(proj, state_s5_re, state_s5_im, lw["s5"], layer)
    x_new = _finish(x2d, proj, out_a, out_b, y_c, pe, lw, final_norm_g, final)
    ka = proj[:, OFF_KA:OFF_KA + KV_A].reshape(nbatch, 1, KVH_A, HD_A)
    va = proj[:, OFF_VA:OFF_VA + KV_A].reshape(nbatch, 1, KVH_A, HD_A)
    return x_new, (ka, va, conv_new, s_new, hr, hi)


def kernel(x_prompt, x_sample, cache_k, cache_v, page_table, state_conv, state_delta, state_s5_re, state_s5_im,
           p_prompt, p_sample, norm_g, w_in, dn_conv_w, dn_a_log, dn_dt_bias, dn_norm_g, s5_lambda_re, s5_lambda_im,
           s5_b_re, s5_b_im, s5_c_re, s5_c_im, s5_d, s5_log_dt, s5_w_glu, s5_b_glu, w_branch, w_out, ple_w,
           ple_norm_g, ple_w_gate, final_norm_g):
    depth = w_in.shape[0]
    bp, L, _ = x_prompt.shape
    nbatch = x_sample.shape[0]
    assert bp == 1 and x_sample.shape[1] == 1
    xp = x_prompt.reshape(L, D_MODEL)
    xs = x_sample.reshape(nbatch, D_MODEL)
    new_p, new_s = [], []
    for i in range(depth):
        lw = _layer_weights(i, norm_g, w_in, dn_conv_w, dn_a_log, dn_dt_bias, dn_norm_g, s5_lambda_re, s5_lambda_im,
                            s5_b_re, s5_b_im, s5_c_re, s5_c_im, s5_d, s5_log_dt, s5_w_glu, s5_b_glu, w_branch,
                            w_out, ple_w, ple_norm_g, ple_w_gate)
        final = i == depth - 1
        xp, st = _prompt_layer(xp, p_prompt[i, 0], lw, final_norm_g, final)
        new_p.append(st)
        xs, st = _sample_layer(xs, p_sample[i, :, 0], lw, final_norm_g, final, i, cache_k, cache_v, page_table,
                               state_conv, state_delta, state_s5_re, state_s5_im)
        new_s.append(st)

    def stk(states, j):
        return jnp.stack([s[j] for s in states])

    return (xp.reshape(1, L, D_MODEL), xs.reshape(nbatch, 1, D_MODEL),
            stk(new_p, 0), stk(new_p, 1), stk(new_p, 2), stk(new_p, 3), stk(new_p, 4), stk(new_p, 5),
            stk(new_s, 0), stk(new_s, 1), stk(new_s, 2), stk(new_s, 3), stk(new_s, 4), stk(new_s, 5))
```

```python
import functools
import math

import jax
import jax.numpy as jnp
from jax import lax
from jax.experimental import pallas as pl
from jax.experimental.pallas import tpu as pltpu

f32 = jnp.float32
bf16 = jnp.bfloat16

D_MODEL = 2048
H_A, KVH_A, HD_A = 8, 4, 128
GROUP_A = H_A // KVH_A
MOBA_BLOCK, MOBA_TOPK = 256, 3
MOBA_RC = 256
MOBA_GB = 8
H_B, DK_B, DV_B, CONV_W = 8, 128, 128, 4
DN_CHUNK = 64
W_C, S5_GROUP_CH, S5_GROUPS, S5_STATE = 1024, 16, 64, 64
PLE_DIM = 256
PAGE_SIZE = 128
EPS = 1e-6
Q_A = H_A * HD_A
KV_A = KVH_A * HD_A
V_B = H_B * DV_B
CONV_CH = 2 * H_B * DK_B + V_B
N_GATE = 3 * D_MODEL

OFF_QA = 0
OFF_KA = OFF_QA + Q_A
OFF_VA = OFF_KA + KV_A
OFF_ZA = OFF_VA + KV_A
OFF_CONV = OFF_ZA + Q_A
OFF_AB = OFF_CONV + CONV_CH
AB_W = 1024
OFF_ZB = OFF_AB + AB_W
OFF_UC = OFF_ZB + V_B
OFF_ZC = OFF_UC + W_C
OFF_G = OFF_ZC + W_C
PW = OFF_G + N_GATE

LANE = 128
SUBLANE = 8
TN_IN = 1024
NEG = -1e30
ATT_SCALE = HD_A ** -0.5
LOG2E = math.log2(math.e)
ATT_C1 = ATT_SCALE * LOG2E
VMEM_BIG = 56 * 1024 * 1024
HIGHEST = lax.Precision.HIGHEST
NT = (((1,), (1,)), ((), ()))


def _cparams(n_axes, vmem=None):
    return pltpu.CompilerParams(dimension_semantics=("arbitrary",) * n_axes, vmem_limit_bytes=vmem)


def _sigmoid(x):
    return 1.0 / (1.0 + jnp.exp(-x))


def _silu(x):
    return x * _sigmoid(x)


def _softplus(x):
    return jnp.maximum(x, 0.0) + jnp.log1p(jnp.exp(-jnp.abs(x)))


def _rms(x, g):
    return x * lax.rsqrt(jnp.mean(x * x, axis=-1, keepdims=True) + EPS) * g


def _dot(a, b):
    return jnp.dot(a, b, preferred_element_type=f32)


def _inproj_kernel(x_ref, g_ref, w_ref, o_ref, k_ref, v_ref, h_scr):
    n = pl.program_id(1)

    @pl.when(n == 0)
    def _():
        h_scr[...] = _rms(x_ref[...], g_ref[...]).astype(bf16)

    res = _dot(h_scr[...], w_ref[...])
    o_ref[...] = res

    @pl.when(n == OFF_KA // TN_IN)
    def _():
        tm = res.shape[0]
        c0 = OFF_KA % TN_IN
        for h in range(KVH_A):
            k_ref[pl.ds(h, tm, stride=KVH_A), :] = res[:, c0 + h * HD_A:c0 + (h + 1) * HD_A]
            v_ref[pl.ds(h, tm, stride=KVH_A), :] = res[:, c0 + KV_A + h * HD_A:c0 + KV_A + (h + 1) * HD_A]


def _inproj(x2d, g, w_pad, tm):
    m = x2d.shape[0]
    assert OFF_KA // TN_IN == (OFF_VA + KV_A - 1) // TN_IN
    kv_shape = jax.ShapeDtypeStruct((m * KVH_A, HD_A), f32)
    kv_spec = pl.BlockSpec((tm * KVH_A, HD_A), lambda i, n: (i, 0))
    return pl.pallas_call(
        _inproj_kernel,
        out_shape=(jax.ShapeDtypeStruct((m, PW), f32), kv_shape, kv_shape),
        grid=(m // tm, PW // TN_IN),
        in_specs=[pl.BlockSpec((tm, D_MODEL), lambda i, n: (i, 0)),
                  pl.BlockSpec((1, D_MODEL), lambda i, n: (0, 0)),
                  pl.BlockSpec((D_MODEL, TN_IN), lambda i, n: (0, n))],
        out_specs=(pl.BlockSpec((tm, TN_IN), lambda i, n: (i, n)), kv_spec, kv_spec),
        scratch_shapes=[pltpu.VMEM((tm, D_MODEL), bf16)],
        compiler_params=_cparams(2, VMEM_BIG),
        name="inproj",
    )(x2d, g.reshape(1, D_MODEL), w_pad)


def _topk_select(scores, n_valid, lane_f, n_sel):
    sc = jnp.where(lane_f < n_valid, scores, -jnp.inf)
    sel = jnp.zeros_like(sc)
    for _ in range(n_sel):
        m = jnp.max(sc, axis=-1, keepdims=True)
        idx = jnp.min(jnp.where(sc == m, lane_f, float(LANE)), axis=-1, keepdims=True)
        hit = lane_f == idx
        sel = jnp.maximum(sel, jnp.where(hit, jnp.where(m > -jnp.inf, 1.0, 0.0), 0.0))
        sc = jnp.where(hit, -jnp.inf, sc)
    return sel


def _moba_prompt_kernel(q_ref, k_ref, v_ref, za_ref, o_ref,
                        kb_scr, vb_scr, kmean_scr, bias_scr, off_scr, acc_scr, m_scr, *, nb):
    g = pl.program_id(0)
    i = pl.program_id(1)
    rows = GROUP_A * MOBA_BLOCK
    wide = MOBA_GB * MOBA_BLOCK

    @pl.when(i == 0)
    def _():
        kb_scr[...] = k_ref[...].astype(bf16)
        vb_scr[:, 0:HD_A] = v_ref[...].astype(bf16)
        vb_scr[:, HD_A:] = jnp.ones((nb * MOBA_BLOCK, HD_A), bf16)
        kmean_scr[...] = jnp.zeros_like(kmean_scr)
        for n in range(nb):
            kmean_scr[n:n + 1, :] = jnp.mean(k_ref[n * MOBA_BLOCK:(n + 1) * MOBA_BLOCK, :], axis=0, keepdims=True)
        row = lax.broadcasted_iota(jnp.int32, (rows, wide), 0)
        col = lax.broadcasted_iota(jnp.int32, (rows, wide), 1)
        second = row >= MOBA_BLOCK
        gf = jnp.zeros((rows, wide), f32) + g.astype(f32)
        slope = jnp.exp2(-(2.0 * gf + 1.0) - jnp.where(second, 1.0, 0.0)) * LOG2E
        rin = row - jnp.where(second, MOBA_BLOCK, 0)
        bias_scr[...] = slope * (col - rin).astype(f32)
        off_scr[...] = slope[:, :LANE] * float(MOBA_BLOCK)

    q = q_ref[...]
    q2 = jnp.concatenate([q[:, :HD_A], q[:, HD_A:]], axis=0)
    lane_f = lax.broadcasted_iota(jnp.int32, (rows, LANE), 1).astype(f32)
    scores = lax.dot_general(q2, kmean_scr[...], NT, precision=HIGHEST, preferred_element_type=f32)
    i_f = jnp.zeros((rows, LANE), f32) + i.astype(f32)
    sel = _topk_select(scores, i_f, lane_f, min(MOBA_TOPK, nb))
    selneg = jnp.where(sel > 0.0, 0.0, NEG)
    lhs = jnp.concatenate([q2.astype(bf16), selneg.astype(bf16)], axis=1)

    acc_scr[...] = jnp.zeros_like(acc_scr)
    m_scr[...] = jnp.full_like(m_scr, -jnp.inf)

    def attend(n, width, aux, dist, causal):
        off = pl.multiple_of(n * MOBA_BLOCK, MOBA_BLOCK)
        rhs = jnp.concatenate([kb_scr[pl.ds(off, width), :], aux], axis=1)
        vblk = vb_scr[pl.ds(off, width), :]
        rss = [slice(c * MOBA_RC, (c + 1) * MOBA_RC) for c in range(rows // MOBA_RC)]
        t = [lax.dot_general(lhs[rs], rhs, NT, preferred_element_type=f32) * ATT_C1 + bias_scr[rs, :width]
             for rs in rss]
        if causal:
            c_i = lax.broadcasted_iota(jnp.int32, (MOBA_RC, width), 1)
            r_i = lax.broadcasted_iota(jnp.int32, (MOBA_RC, width), 0)
            t = [jnp.where(c_i <= r_i + (k * MOBA_RC) % MOBA_BLOCK, t[k], NEG) for k in range(len(rss))]
        m_prev = [m_scr[rs] for rs in rss]
        m_cur = [jnp.max(x, axis=1, keepdims=True) for x in t]
        shift = None
        if dist is not None:
            shift = [off_scr[rs] * dist for rs in rss]
            m_cur = [mc - sh for mc, sh in zip(m_cur, shift)]
        m_new = [jnp.maximum(a, b) for a, b in zip(m_prev, m_cur)]
        alpha = [jnp.exp2(a - b) for a, b in zip(m_prev, m_new)]
        sub = m_new if shift is None else [a + b for a, b in zip(m_new, shift)]
        p = [jnp.exp2(x - jnp.concatenate([s] * (width // LANE), axis=1)).astype(bf16) for x, s in zip(t, sub)]
        pv = [_dot(x, vblk) for x in p]
        for k, rs in enumerate(rss):
            acc_scr[rs] = acc_scr[rs] * jnp.concatenate([alpha[k], alpha[k]], axis=1) + pv[k]
            m_scr[rs] = m_new[k]

    def dvec(d):
        return jnp.zeros((MOBA_RC, LANE), f32) + d

    attend(i, MOBA_BLOCK, jnp.zeros((MOBA_BLOCK, LANE), bf16), None, True)

    def past(n, nblk):
        w = nblk * MOBA_BLOCK
        lane_k = lax.broadcasted_iota(jnp.int32, (w, LANE), 1)
        blk_k = lax.shift_right_logical(lax.broadcasted_iota(jnp.int32, (w, LANE), 0), int(math.log2(MOBA_BLOCK)))
        aux = jnp.where(lane_k == n + blk_k, 1.0, 0.0).astype(bf16)
        attend(n, w, aux, dvec((i - n).astype(f32)), False)

    def body(j, c):
        past(MOBA_GB * j, MOBA_GB)
        return c

    ngroups = lax.div(i, MOBA_GB)
    lax.fori_loop(0, ngroups, body, 0)
    done = ngroups * MOBA_GB
    nblk = MOBA_GB // 2
    while nblk >= 1:
        take = lax.rem(lax.div(i, nblk), 2) == 1

        @pl.when(take)
        def _(done=done, nblk=nblk):
            past(done, nblk)

        done = done + jnp.where(take, nblk, 0)
        nblk //= 2

    acc = acc_scr[...]
    o = acc[:, :HD_A] / acc[:, HD_A:]
    o2 = jnp.concatenate([o[:MOBA_BLOCK], o[MOBA_BLOCK:]], axis=1)
    o_ref[...] = o2 * _silu(za_ref[...])


def _moba_prompt(proj):
    L = proj.shape[0]
    nb = L // MOBA_BLOCK
    rows = GROUP_A * MOBA_BLOCK
    wq = GROUP_A * HD_A
    return pl.pallas_call(
        functools.partial(_moba_prompt_kernel, nb=nb),
        out_shape=jax.ShapeDtypeStruct((L, Q_A), f32),
        grid=(KVH_A, nb),
        in_specs=[pl.BlockSpec((MOBA_BLOCK, wq), lambda g, i: (i, OFF_QA // wq + g)),
                  pl.BlockSpec((L, HD_A), lambda g, i: (0, OFF_KA // HD_A + g)),
                  pl.BlockSpec((L, HD_A), lambda g, i: (0, OFF_VA // HD_A + g)),
                  pl.BlockSpec((MOBA_BLOCK, wq), lambda g, i: (i, OFF_ZA // wq + g))],
        out_specs=pl.BlockSpec((MOBA_BLOCK, wq), lambda g, i: (i, g)),
        scratch_shapes=[pltpu.VMEM((L, HD_A), bf16),
                        pltpu.VMEM((L, 2 * HD_A), bf16),
                        pltpu.VMEM((LANE, HD_A), f32),
                        pltpu.VMEM((rows, MOBA_GB * MOBA_BLOCK), f32),
                        pltpu.VMEM((rows, LANE), f32),
                        pltpu.VMEM((rows, 2 * HD_A), f32),
                        pltpu.VMEM((rows, HD_A), f32)],
        compiler_params=_cparams(2, VMEM_BIG),
        name="moba_prompt",
    )(proj, proj, proj, proj)


def _moba_decode_kernel(pt_ref, q_ref, kn_ref, vn_ref, za_ref, ck_ref, cv_ref, o_ref,
                        kbuf, vbuf, sems, *, layer, n_pages, nbatch):
    b = pl.program_id(0)
    past = n_pages * PAGE_SIZE
    nbp = past // MOBA_BLOCK
    prow = PAGE_SIZE * KVH_A

    def copies(bb, slot):
        out = []
        for p in range(n_pages):
            page = pt_ref[bb * n_pages + p]
            out.append(pltpu.make_async_copy(ck_ref.at[layer, page], kbuf.at[slot, pl.ds(p * prow, prow), :],
                                             sems.at[0, slot]))
            out.append(pltpu.make_async_copy(cv_ref.at[layer, page], vbuf.at[slot, pl.ds(p * prow, prow), :],
                                             sems.at[1, slot]))
        return out

    slot = lax.rem(b, 2)

    @pl.when(b == 0)
    def _():
        for c in copies(0, 0):
            c.start()

    @pl.when(b + 1 < nbatch)
    def _():
        for c in copies(b + 1, 1 - slot):
            c.start()

    for c in copies(b, slot):
        c.wait()

    q = q_ref[0]
    kn = kn_ref[0].astype(bf16).astype(f32)
    vn = vn_ref[0].astype(bf16).astype(f32)
    lane_f = lax.broadcasted_iota(jnp.int32, (SUBLANE, LANE), 1).astype(f32)
    rrow = lax.broadcasted_iota(jnp.int32, (SUBLANE, past), 0).astype(f32)
    pos = lax.broadcasted_iota(jnp.int32, (SUBLANE, past), 1)
    dist = (past - pos).astype(f32)
    zpad = jnp.zeros((SUBLANE - GROUP_A, HD_A), f32)
    groups = range(KVH_A)
    hs = [slice(g * HD_A, (g + 1) * HD_A) for g in groups]
    kf = [kbuf[slot, pl.ds(g, past, stride=KVH_A), :] for g in groups]
    qg = [jnp.concatenate([q[:, (GROUP_A * g + r) * HD_A:(GROUP_A * g + r + 1) * HD_A] for r in range(GROUP_A)]
                          + [zpad], axis=0) for g in groups]
    qb = [x.astype(bf16) for x in qg]
    s = [lax.dot_general(qb[g], kf[g].astype(bf16), NT, preferred_element_type=f32) for g in groups]
    kmean = [jnp.concatenate(
        [jnp.mean(kf[g][n * MOBA_BLOCK:(n + 1) * MOBA_BLOCK, :], axis=0, keepdims=True) for n in range(nbp)]
        + [jnp.zeros((LANE - nbp, HD_A), f32)], axis=0) for g in groups]
    scores = [lax.dot_general(qg[g], kmean[g], NT, precision=HIGHEST, preferred_element_type=f32) for g in groups]
    sel = [_topk_select(scores[g], float(nbp), lane_f, min(MOBA_TOPK, nbp + 1)) for g in groups]
    vb = [vbuf[slot, pl.ds(g, past, stride=KVH_A), :].astype(bf16) for g in groups]
    pb, p_own, l = [], [], []
    for g in groups:
        slope = jnp.exp2(-(rrow + float(GROUP_A * g + 1)))
        sg = s[g] * ATT_SCALE - slope * dist
        selx = jnp.concatenate([jnp.broadcast_to(sel[g][:, n:n + 1], (SUBLANE, MOBA_BLOCK)) for n in range(nbp)],
                               axis=1)
        sg = jnp.where(selx > 0.0, sg, NEG)
        s_own = jnp.sum(qb[g].astype(f32) * kn[:, hs[g]], axis=1, keepdims=True) * ATT_SCALE
        m = jnp.maximum(jnp.max(sg, axis=1, keepdims=True), s_own)
        pb.append(jnp.exp(sg - m).astype(bf16))
        p_own.append(jnp.exp(s_own - m).astype(bf16).astype(f32))
        l.append(jnp.sum(pb[g].astype(f32), axis=1, keepdims=True) + p_own[g])
    pv = [_dot(pb[g], vb[g]) for g in groups]
    outs = []
    for g in groups:
        o = (pv[g] + p_own[g] * vn[:, hs[g]]) / l[g]
        outs.extend(o[r:r + 1, :] for r in range(GROUP_A))
    o_ref[0] = jnp.concatenate(outs, axis=1) * _silu(za_ref[0])


def _moba_decode(proj_s, cache_k, cache_v, page_table, layer):
    nbatch, n_pages = page_table.shape
    past = n_pages * PAGE_SIZE
    n_phys = cache_k.shape[1]
    ck = cache_k.reshape(cache_k.shape[0], n_phys, PAGE_SIZE * KVH_A, HD_A)
    cv = cache_v.reshape(cache_v.shape[0], n_phys, PAGE_SIZE * KVH_A, HD_A)
    p3 = proj_s.reshape(nbatch, 1, PW)
    grid_spec = pltpu.PrefetchScalarGridSpec(
        num_scalar_prefetch=1,
        grid=(nbatch,),
        in_specs=[pl.BlockSpec((1, 1, Q_A), lambda b, pt: (b, 0, OFF_QA // Q_A)),
                  pl.BlockSpec((1, 1, KV_A), lambda b, pt: (b, 0, OFF_KA // KV_A)),
                  pl.BlockSpec((1, 1, KV_A), lambda b, pt: (b, 0, OFF_VA // KV_A)),
                  pl.BlockSpec((1, 1, Q_A), lambda b, pt: (b, 0, OFF_ZA // Q_A)),
                  pl.BlockSpec(memory_space=pl.ANY),
                  pl.BlockSpec(memory_space=pl.ANY)],
        out_specs=pl.BlockSpec((1, 1, Q_A), lambda b, pt: (b, 0, 0)),
        scratch_shapes=[pltpu.VMEM((2, past * KVH_A, HD_A), f32),
                        pltpu.VMEM((2, past * KVH_A, HD_A), f32),
                        pltpu.SemaphoreType.DMA((2, 2))],
    )
    out = pl.pallas_call(
        functools.partial(_moba_decode_kernel, layer=layer, n_pages=n_pages, nbatch=nbatch),
        out_shape=jax.ShapeDtypeStruct((nbatch, 1, Q_A), f32),
        grid_spec=grid_spec,
        compiler_params=_cparams(1, VMEM_BIG),
        name="moba_decode",
    )(page_table.reshape(-1), p3, p3, p3, p3, ck, cv)
    return out.reshape(nbatch, Q_A)


GDN_T = 256
GDN_HB = 8


def _gdn_prompt_kernel(alog_ref, dtb_ref, q_ref, k_ref, v_ref, ab_ref, zb_ref, wq_ref, wk_ref, wv_ref, ng_ref,
                       o_ref, s_out_ref, s_scr, prev_scr, vnew_scr, *, T):
    hp = pl.program_id(0)
    t = pl.program_id(1)
    C = min(DN_CHUNK, T)
    n_chunks = T // C
    shift = int(math.log2(C))
    heads = range(GDN_HB)

    @pl.when(t == 0)
    def _():
        s_scr[...] = jnp.zeros_like(s_scr)
        prev_scr[...] = jnp.zeros_like(prev_scr)
        vnew_scr[...] = jnp.zeros_like(vnew_scr)

    row = lax.broadcasted_iota(jnp.int32, (T, T), 0)
    col = lax.broadcasted_iota(jnp.int32, (T, T), 1)
    same = lax.shift_right_logical(row, shift) == lax.shift_right_logical(col, shift)
    tri_incl = jnp.logical_and(same, row >= col)
    tri_strict = jnp.logical_and(same, row > col)
    ltri = jnp.where(tri_incl, 1.0, 0.0)
    ab = ab_ref[...]
    lane = lax.broadcasted_iota(jnp.int32, (T, LANE), 1)
    r8 = lax.broadcasted_iota(jnp.int32, (SUBLANE, LANE), 0)
    colT = lax.broadcasted_iota(jnp.int32, (DK_B, T), 1)

    reps = T // LANE

    def conv(x_ref, w_ref, hb, which):
        sl = slice(hb * LANE, (hb + 1) * LANE)
        x = x_ref[:, sl]
        w = w_ref[:, sl]
        prev8 = prev_scr[3 * hb + which]
        acc = x * w[CONV_W - 1:CONV_W, :]
        for s in range(1, CONV_W):
            xs = pltpu.roll(x, s, axis=0)
            top = jnp.where(r8 < s, pltpu.roll(prev8, s, axis=0), xs[:SUBLANE])
            xs = jnp.concatenate([top, xs[SUBLANE:]], axis=0)
            acc = acc + xs * w[CONV_W - 1 - s:CONV_W - s, :]
        prev_scr[3 * hb + which] = x[T - SUBLANE:, :]
        return _silu(acc)

    qn, kn, vc, beta, g_rep = [], [], [], [], []
    for hb in heads:
        h = hp * GDN_HB + hb
        qc = conv(q_ref, wq_ref, hb, 0)
        kc = conv(k_ref, wk_ref, hb, 1)
        vc.append(conv(v_ref, wv_ref, hb, 2))
        qn.append(qc * lax.rsqrt(jnp.sum(qc * qc, axis=-1, keepdims=True) + EPS) * (DK_B ** -0.5))
        kn.append(kc * lax.rsqrt(jnp.sum(kc * kc, axis=-1, keepdims=True) + EPS))
        a = jnp.sum(jnp.where(lane == h, ab, 0.0), axis=-1, keepdims=True)
        b = jnp.sum(jnp.where(lane == H_B + h, ab, 0.0), axis=-1, keepdims=True)
        beta.append(_sigmoid(b))
        glog = -jnp.exp(jnp.zeros((T, 1), f32) + alog_ref[h]) * _softplus(a + dtb_ref[h])
        g_rep.append(jnp.broadcast_to(glog, (T, LANE)))

    gam = [jnp.dot(ltri, g_rep[hb], precision=HIGHEST, preferred_element_type=f32) for hb in heads]
    gam_row = [gam[hb].T for hb in heads]
    kb = [kn[hb].astype(bf16) for hb in heads]
    kk = [lax.dot_general(kb[hb], kb[hb], NT, preferred_element_type=f32) for hb in heads]
    qk_raw = [lax.dot_general(qn[hb].astype(bf16), kb[hb], NT, preferred_element_type=f32) for hb in heads]

    decay, pw = [], []
    for hb in heads:
        gam_i = jnp.concatenate([gam[hb]] * reps, axis=1)
        gam_j = jnp.concatenate([gam_row[hb]] * reps, axis=0)
        decay.append(jnp.exp(jnp.where(tri_incl, gam_i - gam_j, -jnp.inf)))
        pw.append(jnp.where(tri_strict, -(jnp.broadcast_to(beta[hb], (T, T)) * kk[hb] * decay[hb]), 0.0))

    x_acc = list(pw)
    pwb = [pw[hb].astype(bf16) for hb in heads]
    for r in range(1, shift):
        pw = [_dot(pwb[hb], pwb[hb]) for hb in heads]
        pwb = [pw[hb].astype(bf16) for hb in heads]
        x_acc = [x_acc[hb] + pw[hb] + _dot(pwb[hb], x_acc[hb].astype(bf16)) for hb in heads]

    egam = [jnp.exp(gam[hb]) for hb in heads]
    rhs = [jnp.concatenate([vc[hb] * beta[hb], kn[hb] * (beta[hb] * egam[hb])], axis=1) for hb in heads]
    sol = [rhs[hb] + _dot(x_acc[hb].astype(bf16), rhs[hb].astype(bf16)) for hb in heads]
    u_val = [sol[hb][:, :DV_B] for hb in heads]
    k_cum_b = [sol[hb][:, DV_B:].astype(bf16) for hb in heads]
    qkb = [(qk_raw[hb] * decay[hb]).astype(bf16) for hb in heads]
    q_dec = [(qn[hb] * egam[hb]).astype(bf16) for hb in heads]
    knT = [kn[hb].T for hb in heads]

    outs = [[] for _ in heads]
    for c in range(n_chunks):
        rs = slice(c * C, (c + 1) * C)
        last = (c + 1) * C - 1
        in_chunk = jnp.logical_and(colT >= c * C, colT <= last)
        s_old = [s_scr[hb] for hb in heads]
        s_b = [s_old[hb].astype(bf16) for hb in heads]
        v_new = [u_val[hb][rs] - _dot(k_cum_b[hb][rs], s_b[hb]) for hb in heads]
        o_state = [_dot(q_dec[hb][rs], s_b[hb]) for hb in heads]
        for hb in heads:
            vnew_scr[hb, rs, :] = v_new[hb]
        vn_b = [vnew_scr[hb].astype(bf16) for hb in heads]
        for hb in heads:
            outs[hb].append(o_state[hb] + _dot(qkb[hb][rs, :], vn_b[hb]))
        for hb in heads:
            w_end = jnp.exp(jnp.where(in_chunk, gam_row[hb][:, last:last + 1] - gam_row[hb], -jnp.inf))
            k_end_t = (knT[hb] * w_end).astype(bf16)
            s_scr[hb] = s_old[hb] * jnp.exp(gam[hb][last:last + 1, :]) + _dot(k_end_t, vn_b[hb])

    for hb in heads:
        sl = slice(hb * LANE, (hb + 1) * LANE)
        o = jnp.concatenate(outs[hb], axis=0)
        o_ref[:, sl] = _rms(o, ng_ref[...]) * _silu(zb_ref[:, sl])
        s_out_ref[hb] = s_scr[hb]


def _gdn_prompt(proj, conv_w, a_log, dt_bias, norm_g):
    L = proj.shape[0]
    T = min(GDN_T, L)
    wb = GDN_HB * LANE
    nq = H_B * DK_B // wb
    return pl.pallas_call(
        functools.partial(_gdn_prompt_kernel, T=T),
        out_shape=(jax.ShapeDtypeStruct((L, V_B), f32), jax.ShapeDtypeStruct((H_B, DK_B, DV_B), f32)),
        grid=(H_B // GDN_HB, L // T),
        in_specs=[pl.BlockSpec(memory_space=pltpu.SMEM),
                  pl.BlockSpec(memory_space=pltpu.SMEM),
                  pl.BlockSpec((T, wb), lambda hp, t: (t, OFF_CONV // wb + hp)),
                  pl.BlockSpec((T, wb), lambda hp, t: (t, OFF_CONV // wb + nq + hp)),
                  pl.BlockSpec((T, wb), lambda hp, t: (t, OFF_CONV // wb + 2 * nq + hp)),
                  pl.BlockSpec((T, LANE), lambda hp, t: (t, OFF_AB // LANE)),
                  pl.BlockSpec((T, wb), lambda hp, t: (t, OFF_ZB // wb + hp)),
                  pl.BlockSpec((CONV_W, wb), lambda hp, t: (0, hp)),
                  pl.BlockSpec((CONV_W, wb), lambda hp, t: (0, nq + hp)),
                  pl.BlockSpec((CONV_W, wb), lambda hp, t: (0, 2 * nq + hp)),
                  pl.BlockSpec((1, DV_B), lambda hp, t: (0, 0))],
        out_specs=(pl.BlockSpec((T, wb), lambda hp, t: (t, hp)),
                   pl.BlockSpec((GDN_HB, DK_B, DV_B), lambda hp, t: (hp, 0, 0))),
        scratch_shapes=[pltpu.VMEM((GDN_HB, DK_B, DV_B), f32),
                        pltpu.VMEM((3 * GDN_HB, SUBLANE, LANE), f32),
                        pltpu.VMEM((GDN_HB, T, DV_B), f32)],
        compiler_params=_cparams(2, VMEM_BIG),
        name="gdn_prompt",
    )(a_log, dt_bias, proj, proj, proj, proj, proj, conv_w, conv_w, conv_w, norm_g.reshape(1, DV_B))


GDN_BB = 8


def _gdn_decode_kernel(x_ref, ab_ref, zb_ref, cs_ref, sd_ref, w_ref, alog_ref, dtb_ref, ng_ref,
                       o_ref, cs_out_ref, sd_out_ref):
    x = x_ref[...]
    w = w_ref[...]
    c0, c1, c2 = cs_ref[:, 0, :], cs_ref[:, 1, :], cs_ref[:, 2, :]
    y = c0 * w[0:1, :] + c1 * w[1:2, :] + c2 * w[2:3, :] + x * w[3:4, :]
    cs_out_ref[:, 0, :] = c1
    cs_out_ref[:, 1, :] = c2
    cs_out_ref[:, 2, :] = x
    y = _silu(y)
    ab = ab_ref[...]
    a = ab[:, 0:H_B]
    beta = _sigmoid(ab[:, H_B:2 * H_B])
    g = -jnp.exp(alog_ref[...]) * _softplus(a + dtb_ref[...])
    eg = jnp.exp(g)
    nq = H_B * DK_B
    outs = []
    for h in range(H_B):
        qc = y[:, h * DK_B:(h + 1) * DK_B]
        kc = y[:, nq + h * DK_B:nq + (h + 1) * DK_B]
        vc = y[:, 2 * nq + h * DV_B:2 * nq + (h + 1) * DV_B]
        qn = qc * lax.rsqrt(jnp.sum(qc * qc, axis=-1, keepdims=True) + EPS) * (DK_B ** -0.5)
        kn = kc * lax.rsqrt(jnp.sum(kc * kc, axis=-1, keepdims=True) + EPS)
        qk = jnp.sum(qn * kn, axis=-1, keepdims=True)
        knT = kn.T
        qnT = qn.T
        rows = []
        for bb in range(GDN_BB):
            s_old = sd_ref[bb, h]
            kcol = knT[:, bb:bb + 1]
            qcol = qnT[:, bb:bb + 1]
            be = beta[bb:bb + 1, h:h + 1]
            e = eg[bb:bb + 1, h:h + 1]
            ks = jnp.sum(kcol * s_old, axis=0, keepdims=True)
            qs = jnp.sum(qcol * s_old, axis=0, keepdims=True)
            v_new = be * vc[bb:bb + 1, :] - (be * e) * ks
            rows.append(e * qs + qk[bb:bb + 1, :] * v_new)
            sd_out_ref[bb, h] = s_old * e + kcol * v_new
        o = jnp.concatenate(rows, axis=0)
        outs.append(_rms(o, ng_ref[...]) * _silu(zb_ref[:, h * DV_B:(h + 1) * DV_B]))
    o_ref[...] = jnp.concatenate(outs, axis=1)


def _gdn_decode(proj_s, state_conv, state_delta, conv_w, a_log, dt_bias, norm_g, layer):
    nbatch = proj_s.shape[0]
    bb = GDN_BB
    return pl.pallas_call(
        _gdn_decode_kernel,
        out_shape=(jax.ShapeDtypeStruct((nbatch, V_B), f32),
                   jax.ShapeDtypeStruct((nbatch, CONV_W - 1, CONV_CH), f32),
                   jax.ShapeDtypeStruct((nbatch, H_B, DK_B, DV_B), f32)),
        grid=(nbatch // bb,),
        in_specs=[pl.BlockSpec((bb, CONV_CH), lambda i: (i, OFF_CONV // CONV_CH)),
                  pl.BlockSpec((bb, LANE), lambda i: (i, OFF_AB // LANE)),
                  pl.BlockSpec((bb, V_B), lambda i: (i, OFF_ZB // V_B)),
                  pl.BlockSpec((None, bb, CONV_W - 1, CONV_CH), lambda i: (layer, i, 0, 0)),
                  pl.BlockSpec((None, bb, H_B, DK_B, DV_B), lambda i: (layer, i, 0, 0, 0)),
                  pl.BlockSpec((CONV_W, CONV_CH), lambda i: (0, 0)),
                  pl.BlockSpec((1, H_B), lambda i: (0, 0)),
                  pl.BlockSpec((1, H_B), lambda i: (0, 0)),
                  pl.BlockSpec((1, DV_B), lambda i: (0, 0))],
        out_specs=(pl.BlockSpec((bb, V_B), lambda i: (i, 0)),
                   pl.BlockSpec((bb, CONV_W - 1, CONV_CH), lambda i: (i, 0, 0)),
                   pl.BlockSpec((bb, H_B, DK_B, DV_B), lambda i: (i, 0, 0, 0))),
        compiler_params=_cparams(1, VMEM_BIG),
        name="gdn_decode",
    )(proj_s, proj_s, proj_s, state_conv, state_delta, conv_w,
      a_log.reshape(1, H_B), dt_bias.reshape(1, H_B), norm_g.reshape(1, DV_B))


S5_JB = W_C // LANE
S5_GPB = LANE // S5_GROUP_CH
S5_SW = S5_GPB * S5_STATE
S5_TT = 512
S5_NSEG = SUBLANE
S5_PAD = 4


def _s5_param_kernel(lr_ref, li_ref, ldt_ref, bre_ref, bim_ref, are_ref, aim_ref, bbre_ref, bbim_ref):
    lr = lr_ref[...]
    li = li_ref[...]
    dt = jnp.exp(ldt_ref[...])
    mag = jnp.exp(lr * dt)
    ang = li * dt
    a_re = mag * jnp.cos(ang)
    a_im = mag * jnp.sin(ang)
    den = lr * lr + li * li
    f_re = ((a_re - 1.0) * lr + a_im * li) / den
    f_im = (a_im * lr - (a_re - 1.0) * li) / den
    are_ref[...] = a_re
    aim_ref[...] = a_im
    fr = f_re[:, None, :]
    fi = f_im[:, None, :]
    bre = bre_ref[...]
    bim = bim_ref[...]
    bbre_ref[...] = fr * bre - fi * bim
    bbim_ref[...] = fr * bim + fi * bre


def _s5_params(lam_re, lam_im, log_dt, b_re, b_im, c_re, c_im, d):
    G, P, CH = S5_GROUPS, S5_STATE, S5_GROUP_CH
    bt_re = jnp.swapaxes(b_re, 1, 2)
    bt_im = jnp.swapaxes(b_im, 1, 2)
    a_re, a_im, bb_re, bb_im = pl.pallas_call(
        _s5_param_kernel,
        out_shape=(jax.ShapeDtypeStruct((G, P), f32), jax.ShapeDtypeStruct((G, P), f32),
                   jax.ShapeDtypeStruct((G, CH, P), f32), jax.ShapeDtypeStruct((G, CH, P), f32)),
        name="s5_params",
    )(lam_re, lam_im, log_dt.reshape(G, 1), bt_re, bt_im)
    eye = jnp.eye(S5_GPB, dtype=f32)

    def bdiag_in(bb):
        x = bb.reshape(S5_JB, S5_GPB, CH, P)
        return jnp.einsum("jacp,ab->jacbp", x, eye).reshape(S5_JB, LANE, S5_SW)

    def bdiag_out(c):
        x = c.reshape(S5_JB, S5_GPB, CH, P)
        return jnp.einsum("jacp,ab->jbpac", x, eye).reshape(S5_JB, S5_SW, LANE)

    wb = jnp.concatenate([bdiag_in(bb_re), bdiag_in(bb_im)], axis=-1).astype(bf16)
    wc_re = bdiag_out(c_re).astype(bf16)
    wc_im = bdiag_out(c_im).astype(bf16)
    return dict(wb=wb, wc_re=wc_re, wc_im=wc_im,
                a_re=a_re.reshape(S5_JB, 1, S5_SW), a_im=a_im.reshape(S5_JB, 1, S5_SW),
                d=d.reshape(S5_JB, 1, LANE))


def _s5_prompt_kernel(u_ref, wb_ref, wcre_ref, wcim_ref, are_ref, aim_ref, d_ref,
                      y_ref, hre_ref, him_ref, a_scr, b_scr, carry_scr, *, TT):
    t = pl.program_id(1)
    nlb = S5_SW // LANE
    seg = TT // S5_NSEG
    pitch = seg + S5_PAD

    @pl.when(t == 0)
    def _():
        carry_scr[...] = jnp.zeros_like(carry_scr)

    u = u_ref[...]
    zpad = jnp.zeros((S5_PAD, LANE), f32)
    u_pad = jnp.concatenate([x for s in range(S5_NSEG) for x in (u[s * seg:(s + 1) * seg], zpad)], axis=0)
    bu = _dot(u_pad.astype(bf16), wb_ref[...])
    for c in range(2 * nlb):
        a_scr[c] = bu[:, c * LANE:(c + 1) * LANE]

    are = are_ref[...]
    aim = aim_ref[...]
    ar8 = [jnp.broadcast_to(are[:, c * LANE:(c + 1) * LANE], (S5_NSEG, LANE)) for c in range(nlb)]
    ai8 = [jnp.broadcast_to(aim[:, c * LANE:(c + 1) * LANE], (S5_NSEG, LANE)) for c in range(nlb)]

    def seg_rows(k):
        return pl.ds(k, S5_NSEG, stride=pitch)

    zero8 = jnp.zeros((S5_NSEG, LANE), f32)
    hr = [zero8] * nlb
    hi = [zero8] * nlb
    for k in range(seg):
        for c in range(nlb):
            nr = ar8[c] * hr[c] - ai8[c] * hi[c] + a_scr[c, seg_rows(k), :]
            ni = ar8[c] * hi[c] + ai8[c] * hr[c] + a_scr[nlb + c, seg_rows(k), :]
            b_scr[c, seg_rows(k), :] = nr
            b_scr[nlb + c, seg_rows(k), :] = ni
            hr[c], hi[c] = nr, ni
    ends = hr + hi

    pr, pi = are, aim
    for _ in range(int(math.log2(seg))):
        pr, pi = pr * pr - pi * pi, 2.0 * pr * pi

    carry = carry_scr[...]
    er, ei = carry[:, :S5_SW], carry[:, S5_SW:]
    rows_r, rows_i = [], []
    for s in range(S5_NSEG):
        rows_r.append(er)
        rows_i.append(ei)
        loc_r = jnp.concatenate([ends[c][s:s + 1, :] for c in range(nlb)], axis=1)
        loc_i = jnp.concatenate([ends[nlb + c][s:s + 1, :] for c in range(nlb)], axis=1)
        er, ei = pr * er - pi * ei + loc_r, pr * ei + pi * er + loc_i
    carry_scr[...] = jnp.concatenate([er, ei], axis=1)
    hre_ref[0] = er
    him_ref[0] = ei
    ein_r = jnp.concatenate(rows_r, axis=0)
    ein_i = jnp.concatenate(rows_i, axis=0)

    pw_r = list(ar8)
    pw_i = list(ai8)
    for k in range(seg):
        for c in range(nlb):
            e_r = ein_r[:, c * LANE:(c + 1) * LANE]
            e_i = ein_i[:, c * LANE:(c + 1) * LANE]
            a_scr[c, seg_rows(k), :] = b_scr[c, seg_rows(k), :] + (pw_r[c] * e_r - pw_i[c] * e_i)
            a_scr[nlb + c, seg_rows(k), :] = b_scr[nlb + c, seg_rows(k), :] + (pw_r[c] * e_i + pw_i[c] * e_r)
            if k + 1 < seg:
                pw_r[c], pw_i[c] = pw_r[c] * ar8[c] - pw_i[c] * ai8[c], pw_r[c] * ai8[c] + pw_i[c] * ar8[c]

    h_re = jnp.concatenate([a_scr[c] for c in range(nlb)], axis=1).astype(bf16)
    h_im = jnp.concatenate([a_scr[nlb + c] for c in range(nlb)], axis=1).astype(bf16)
    y_pad = _dot(h_re, wcre_ref[...]) - _dot(h_im, wcim_ref[...])
    y = jnp.concatenate([y_pad[s * pitch:s * pitch + seg] for s in range(S5_NSEG)], axis=0)
    y_ref[...] = y + d_ref[...] * u


def _s5_prompt(proj, sp):
    L = proj.shape[0]
    TT = min(S5_TT, L)
    jb_spec = lambda shape: pl.BlockSpec((None,) + shape, lambda j, t: (j, 0, 0))
    y, hre, him = pl.pallas_call(
        functools.partial(_s5_prompt_kernel, TT=TT),
        out_shape=(jax.ShapeDtypeStruct((L, W_C), f32),
                   jax.ShapeDtypeStruct((S5_JB, 1, S5_SW), f32),
                   jax.ShapeDtypeStruct((S5_JB, 1, S5_SW), f32)),
        grid=(S5_JB, L // TT),
        in_specs=[pl.BlockSpec((TT, LANE), lambda j, t: (t, OFF_UC // LANE + j)),
                  jb_spec((LANE, 2 * S5_SW)), jb_spec((S5_SW, LANE)), jb_spec((S5_SW, LANE)),
                  jb_spec((1, S5_SW)), jb_spec((1, S5_SW)), jb_spec((1, LANE))],
        out_specs=(pl.BlockSpec((TT, LANE), lambda j, t: (t, j)),
                   pl.BlockSpec((1, 1, S5_SW), lambda j, t: (j, 0, 0)),
                   pl.BlockSpec((1, 1, S5_SW), lambda j, t: (j, 0, 0))),
        scratch_shapes=[pltpu.VMEM((2 * S5_SW // LANE, TT + S5_NSEG * S5_PAD, LANE), f32),
                        pltpu.VMEM((2 * S5_SW // LANE, TT + S5_NSEG * S5_PAD, LANE), f32),
                        pltpu.VMEM((1, 2 * S5_SW), f32)],
        compiler_params=_cparams(2, VMEM_BIG),
        name="s5_prompt",
    )(proj, sp["wb"], sp["wc_re"], sp["wc_im"], sp["a_re"], sp["a_im"], sp["d"])
    return y, hre.reshape(S5_GROUPS, S5_STATE), him.reshape(S5_GROUPS, S5_STATE)


def _s5_decode_kernel(u_ref, h0r_ref, h0i_ref, wb_ref, wcre_ref, wcim_ref, are_ref, aim_ref, d_ref,
                      y_ref, hr_ref, hi_ref):
    u = u_ref[...]
    bu = _dot(u.astype(bf16), wb_ref[...])
    are, aim = are_ref[...], aim_ref[...]
    h0r, h0i = h0r_ref[...], h0i_ref[...]
    hr = bu[:, :S5_SW] + are * h0r - aim * h0i
    hi = bu[:, S5_SW:] + are * h0i + aim * h0r
    hr_ref[...] = hr
    hi_ref[...] = hi
    y_ref[...] = _dot(hr.astype(bf16), wcre_ref[...]) - _dot(hi.astype(bf16), wcim_ref[...]) + d_ref[...] * u


def _s5_decode(proj_s, st_re, st_im, sp, layer):
    nbatch = proj_s.shape[0]
    sre = st_re.reshape(st_re.shape[0], nbatch, S5_GROUPS * S5_STATE)
    sim = st_im.reshape(st_im.shape[0], nbatch, S5_GROUPS * S5_STATE)
    jb_spec = lambda shape: pl.BlockSpec((None,) + shape, lambda j: (j, 0, 0))
    st_spec = pl.BlockSpec((None, nbatch, S5_SW), lambda j: (layer, 0, j))
    y, hr, hi = pl.pallas_call(
        _s5_decode_kernel,
        out_shape=(jax.ShapeDtypeStruct((nbatch, W_C), f32),
                   jax.ShapeDtypeStruct((nbatch, S5_GROUPS * S5_STATE), f32),
                   jax.ShapeDtypeStruct((nbatch, S5_GROUPS * S5_STATE), f32)),
        grid=(S5_JB,),
        in_specs=[pl.BlockSpec((nbatch, LANE), lambda j: (0, OFF_UC // LANE + j)),
                  st_spec, st_spec,
                  jb_spec((LANE, 2 * S5_SW)), jb_spec((S5_SW, LANE)), jb_spec((S5_SW, LANE)),
                  jb_spec((1, S5_SW)), jb_spec((1, S5_SW)), jb_spec((1, LANE))],
        out_specs=(pl.BlockSpec((nbatch, LANE), lambda j: (0, j)),
                   pl.BlockSpec((nbatch, S5_SW), lambda j: (0, j)),
                   pl.BlockSpec((nbatch, S5_SW), lambda j: (0, j))),
        compiler_params=_cparams(1),
        name="s5_decode",
    )(proj_s, sre, sim, sp["wb"], sp["wc_re"], sp["wc_im"], sp["a_re"], sp["a_im"], sp["d"])
    return (y, hr.reshape(nbatch, S5_GROUPS, S5_STATE), hi.reshape(nbatch, S5_GROUPS, S5_STATE))


def _s5_glu_kernel(y_ref, zc_ref, w_ref, b_ref, o_ref):
    y = y_ref[...]
    ge = 0.5 * y * (1.0 + lax.erf(y * (0.5 ** 0.5)))
    gl = _dot(ge.astype(bf16), w_ref[...]) + b_ref[...]
    o_ref[...] = ge * _sigmoid(gl) * _silu(zc_ref[...])


def _s5_glu(y, proj, w_glu, b_glu, tm):
    m = y.shape[0]
    return pl.pallas_call(
        _s5_glu_kernel,
        out_shape=jax.ShapeDtypeStruct((m, W_C), f32),
        grid=(m // tm,),
        in_specs=[pl.BlockSpec((tm, W_C), lambda i: (i, 0)),
                  pl.BlockSpec((tm, W_C), lambda i: (i, OFF_ZC // W_C)),
                  pl.BlockSpec((W_C, W_C), lambda i: (0, 0)),
                  pl.BlockSpec((1, W_C), lambda i: (0, 0))],
        out_specs=pl.BlockSpec((tm, W_C), lambda i: (i, 0)),
        compiler_params=_cparams(1, VMEM_BIG),
        name="s5_glu",
    )(y, proj, w_glu, b_glu.reshape(1, W_C))


def _merge_kernel(oa_ref, ob_ref, oc_ref, ga_ref, gb_ref, gc_ref, wb_ref, o_ref):
    acc = _sigmoid(ga_ref[...]) * _dot(oa_ref[...].astype(bf16), wb_ref[0])
    acc = acc + _sigmoid(gb_ref[...]) * _dot(ob_ref[...].astype(bf16), wb_ref[1])
    acc = acc + _sigmoid(gc_ref[...]) * _dot(oc_ref[...].astype(bf16), wb_ref[2])
    o_ref[...] = acc.astype(bf16)


def _merge(out_a, out_b, out_c, proj, w_branch, tm):
    m = out_a.shape[0]
    act = pl.BlockSpec((tm, Q_A), lambda i: (i, 0))
    gate = lambda k: pl.BlockSpec((tm, D_MODEL), lambda i: (i, OFF_G // D_MODEL + k))
    return pl.pallas_call(
        _merge_kernel,
        out_shape=jax.ShapeDtypeStruct((m, D_MODEL), bf16),
        grid=(m // tm,),
        in_specs=[act, act, act, gate(0), gate(1), gate(2),
                  pl.BlockSpec((3, Q_A, D_MODEL), lambda i: (0, 0, 0), pipeline_mode=pl.Buffered(1))],
        out_specs=pl.BlockSpec((tm, D_MODEL), lambda i: (i, 0)),
        compiler_params=_cparams(1, VMEM_BIG),
        name="merge",
    )(out_a, out_b, out_c, proj, proj, proj, w_branch)


def _post_kernel(x_ref, mg_ref, pe_ref, wout_ref, plew_ref, pleg_ref, png_ref, fng_ref, o_ref, *, final):
    x2 = x_ref[...] + _dot(mg_ref[...], wout_ref[...])
    gate = _sigmoid(_dot(_rms(x2, png_ref[...]).astype(bf16), pleg_ref[...]))
    x3 = x2 + _dot(pe_ref[...].astype(bf16), plew_ref[...]) * gate
    if final:
        x3 = _rms(x3, fng_ref[...])
    o_ref[...] = x3


def _post(x2d, merged, pe, w_out, ple_w, ple_w_gate, ple_norm_g, final_norm_g, final, tm):
    m = x2d.shape[0]
    const = lambda shape: pl.BlockSpec(shape, lambda i: (0, 0), pipeline_mode=pl.Buffered(1))
    return pl.pallas_call(
        functools.partial(_post_kernel, final=final),
        out_shape=jax.ShapeDtypeStruct((m, D_MODEL), f32),
        grid=(m // tm,),
        in_specs=[pl.BlockSpec((tm, D_MODEL), lambda i: (i, 0)),
                  pl.BlockSpec((tm, D_MODEL), lambda i: (i, 0)),
                  pl.BlockSpec((tm, PLE_DIM), lambda i: (i, 0)),
                  const((D_MODEL, D_MODEL)), const((PLE_DIM, D_MODEL)), const((D_MODEL, D_MODEL)),
                  const((1, D_MODEL)), const((1, D_MODEL))],
        out_specs=pl.BlockSpec((tm, D_MODEL), lambda i: (i, 0)),
        compiler_params=_cparams(1, VMEM_BIG),
        name="post",
    )(x2d, merged, pe, w_out, ple_w, ple_w_gate, ple_norm_g.reshape(1, D_MODEL), final_norm_g.reshape(1, D_MODEL))


WP_TN = 1024
WP_TM = 1024


def _wprep_kernel(a_ref, b_ref, o_ref):
    t = pl.program_id(1)
    n_plain = OFF_AB // WP_TN
    lead = 2 * H_B

    @pl.when(t < n_plain)
    def _():
        o_ref[...] = a_ref[...].astype(bf16)

    @pl.when(t == n_plain)
    def _():
        lane = lax.broadcasted_iota(jnp.int32, (WP_TM, WP_TN), 1)
        o_ref[...] = jnp.where(lane < lead, a_ref[...], 0.0).astype(bf16)

    @pl.when(t > n_plain)
    def _():
        x = jnp.concatenate([a_ref[...], b_ref[...]], axis=1)
        o_ref[...] = x[:, lead:lead + WP_TN].astype(bf16)


def _pad_w_in(w_in, layer):
    k = w_in.shape[1]
    n_plain = OFF_AB // WP_TN
    per = WP_TN // LANE

    def a_map(r, t):
        return (layer, r, jnp.where(t <= n_plain, t, t - 1))

    def b_map(r, t):
        return (layer, r, jnp.where(t > n_plain, t * per, per))

    return pl.pallas_call(
        _wprep_kernel,
        out_shape=jax.ShapeDtypeStruct((k, PW), bf16),
        grid=(k // WP_TM, PW // WP_TN),
        in_specs=[pl.BlockSpec((None, WP_TM, WP_TN), a_map),
                  pl.BlockSpec((None, WP_TM, LANE), b_map)],
        out_specs=pl.BlockSpec((WP_TM, WP_TN), lambda r, t: (r, t)),
        compiler_params=_cparams(2, VMEM_BIG),
        name="wprep",
    )(w_in, w_in)


def _row_tile(m, pref):
    return pref if m % pref == 0 else m


def _layer_weights(i, norm_g, w_in, dn_conv_w, dn_a_log, dn_dt_bias, dn_norm_g, s5_lambda_re, s5_lambda_im,
                   s5_b_re, s5_b_im, s5_c_re, s5_c_im, s5_d, s5_log_dt, s5_w_glu, s5_b_glu, w_branch, w_out,
                   ple_w, ple_norm_g, ple_w_gate):
    return dict(
        norm_g=norm_g[i], w_pad=_pad_w_in(w_in, i), conv_w=dn_conv_w[i], a_log=dn_a_log[i], dt_bias=dn_dt_bias[i],
        dn_norm_g=dn_norm_g[i],
        s5=_s5_params(s5_lambda_re[i], s5_lambda_im[i], s5_log_dt[i], s5_b_re[i], s5_b_im[i], s5_c_re[i],
                      s5_c_im[i], s5_d[i]),
        w_glu=s5_w_glu[i].astype(bf16), b_glu=s5_b_glu[i], w_branch=w_branch[i].astype(bf16),
        w_out=w_out[i].astype(bf16), ple_w=ple_w[i].astype(bf16), ple_norm_g=ple_norm_g[i],
        ple_w_gate=ple_w_gate[i].astype(bf16))


def _finish(x2d, proj, out_a, out_b, y_c, pe, lw, final_norm_g, final):
    m = x2d.shape[0]
    out_c = _s5_glu(y_c, proj, lw["w_glu"], lw["b_glu"], _row_tile(m, 512))
    merged = _merge(out_a, out_b, out_c, proj, lw["w_branch"], _row_tile(m, 256))
    return _post(x2d, merged, pe, lw["w_out"], lw["ple_w"], lw["ple_w_gate"], lw["ple_norm_g"], final_norm_g,
                 final, _row_tile(m, 256))


def _prompt_layer(x2d, pe, lw, final_norm_g, final):
    L = x2d.shape[0]
    proj, k_new, v_new = _inproj(x2d, lw["norm_g"], lw["w_pad"], _row_tile(L, 1024))
    out_a = _moba_prompt(proj)
    out_b, s_new = _gdn_prompt(proj, lw["conv_w"], lw["a_log"], lw["dt_bias"], lw["dn_norm_g"])
    y_c, hre, him = _s5_prompt(proj, lw["s5"])
    x_new = _finish(x2d, proj, out_a, out_b, y_c, pe, lw, final_norm_g, final)
    ka = k_new.reshape(1, L, KVH_A, HD_A)
    va = v_new.reshape(1, L, KVH_A, HD_A)
    conv_new = proj[L - (CONV_W - 1):, OFF_CONV:OFF_CONV + CONV_CH][None]
    return x_new, (ka, va, conv_new, s_new[None], hre[None], him[None])


def _sample_layer(x2d, pe, lw, final_norm_g, final, layer, cache_k, cache_v, page_table, state_conv, state_delta,
                  state_s5_re, state_s5_im):
    nbatch = x2d.shape[0]
    proj, k_new, v_new = _inproj(x2d, lw["norm_g"], lw["w_pad"], nbatch)
    out_a = _moba_decode(proj, cache_k, cache_v, page_table, layer)
    out_b, conv_new, s_new = _gdn_decode(proj, state_conv, state_delta, lw["conv_w"], lw["a_log"], lw["dt_bias"],
                                         lw["dn_norm_g"], layer)
    y_c, hr, hi = _s5_decode(proj, state_s5_re, state_s5_im, lw["s5"], layer)
    x_new = _finish(x2d, proj, out_a, out_b, y_c, pe, lw, final_norm_g, final)
    ka = k_new.reshape(nbatch, 1, KVH_A, HD_A)
    va = v_new.reshape(nbatch, 1, KVH_A, HD_A)
    return x_new, (ka, va, conv_new, s_new, hr, hi)


def kernel(x_prompt, x_sample, cache_k, cache_v, page_table, state_conv, state_delta, state_s5_re, state_s5_im,
           p_prompt, p_sample, norm_g, w_in, dn_conv_w, dn_a_log, dn_dt_bias, dn_norm_g, s5_lambda_re, s5_lambda_im,
           s5_b_re, s5_b_im, s5_c_re, s5_c_im, s5_d, s5_log_dt, s5_w_glu, s5_b_glu, w_branch, w_out, ple_w,
           ple_norm_g, ple_w_gate, final_norm_g):
    depth = w_in.shape[0]
    bp, L, _ = x_prompt.shape
    nbatch = x_sample.shape[0]
    assert bp == 1 and x_sample.shape[1] == 1
    xp = x_prompt.reshape(L, D_MODEL)
    xs = x_sample.reshape(nbatch, D_MODEL)
    new_p, new_s = [], []
    for i in range(depth):
        lw = _layer_weights(i, norm_g, w_in, dn_conv_w, dn_a_log, dn_dt_bias, dn_norm_g, s5_lambda_re, s5_lambda_im,
                            s5_b_re, s5_b_im, s5_c_re, s5_c_im, s5_d, s5_log_dt, s5_w_glu, s5_b_glu, w_branch,
                            w_out, ple_w, ple_norm_g, ple_w_gate)
        final = i == depth - 1
        xp, st = _prompt_layer(xp, p_prompt[i, 0], lw, final_norm_g, final)
        new_p.append(st)
        xs, st = _sample_layer(xs, p_sample[i, :, 0], lw, final_norm_g, final, i, cache_k, cache_v, page_table,
                               state_conv, state_delta, state_s5_re, state_s5_im)
        new_s.append(st)

    def stk(states, j):
        return jnp.stack([s[j] for s in states])

    return (xp.reshape(1, L, D_MODEL), xs.reshape(nbatch, 1, D_MODEL),
            stk(new_p, 0), stk(new_p, 1), stk(new_p, 2), stk(new_p, 3), stk(new_p, 4), stk(new_p, 5),
            stk(new_s, 0), stk(new_s, 1), stk(new_s, 2), stk(new_s, 3), stk(new_s, 4), stk(new_s, 5))
```

```python
import functools
import math

import jax
import jax.numpy as jnp
from jax import lax
from jax.experimental import pallas as pl
from jax.experimental.pallas import tpu as pltpu

f32 = jnp.float32
bf16 = jnp.bfloat16

D_MODEL = 2048
H_A, KVH_A, HD_A = 8, 4, 128
GROUP_A = H_A // KVH_A
MOBA_BLOCK, MOBA_TOPK = 256, 3
MOBA_RC = 256
MOBA_GB = 8
H_B, DK_B, DV_B, CONV_W = 8, 128, 128, 4
DN_CHUNK = 64
W_C, S5_GROUP_CH, S5_GROUPS, S5_STATE = 1024, 16, 64, 64
PLE_DIM = 256
PAGE_SIZE = 128
EPS = 1e-6
Q_A = H_A * HD_A
KV_A = KVH_A * HD_A
V_B = H_B * DV_B
CONV_CH = 2 * H_B * DK_B + V_B
N_GATE = 3 * D_MODEL

OFF_QA = 0
OFF_KA = OFF_QA + Q_A
OFF_VA = OFF_KA + KV_A
OFF_ZA = OFF_VA + KV_A
OFF_CONV = OFF_ZA + Q_A
OFF_AB = OFF_CONV + CONV_CH
AB_W = 1024
OFF_ZB = OFF_AB + AB_W
OFF_UC = OFF_ZB + V_B
OFF_ZC = OFF_UC + W_C
OFF_G = OFF_ZC + W_C
PW = OFF_G + N_GATE

LANE = 128
SUBLANE = 8
TN_IN = 1024
NEG = -1e30
ATT_SCALE = HD_A ** -0.5
LOG2E = math.log2(math.e)
ATT_C1 = ATT_SCALE * LOG2E
VMEM_BIG = 56 * 1024 * 1024
HIGHEST = lax.Precision.HIGHEST
NT = (((1,), (1,)), ((), ()))


def _cparams(n_axes, vmem=None):
    return pltpu.CompilerParams(dimension_semantics=("arbitrary",) * n_axes, vmem_limit_bytes=vmem)


def _sigmoid(x):
    return 1.0 / (1.0 + jnp.exp(-x))


def _silu(x):
    return x * _sigmoid(x)


def _softplus(x):
    return jnp.maximum(x, 0.0) + jnp.log1p(jnp.exp(-jnp.abs(x)))


def _rms(x, g):
    return x * lax.rsqrt(jnp.mean(x * x, axis=-1, keepdims=True) + EPS) * g


def _dot(a, b):
    return jnp.dot(a, b, preferred_element_type=f32)


def _inproj_kernel(x_ref, g_ref, w_ref, o_ref, k_ref, v_ref, h_scr):
    n = pl.program_id(1)

    @pl.when(n == 0)
    def _():
        h_scr[...] = _rms(x_ref[...], g_ref[...]).astype(bf16)

    res = _dot(h_scr[...], w_ref[...])
    o_ref[...] = res

    @pl.when(n == OFF_KA // TN_IN)
    def _():
        tm = res.shape[0]
        c0 = OFF_KA % TN_IN
        for h in range(KVH_A):
            k_ref[pl.ds(h, tm, stride=KVH_A), :] = res[:, c0 + h * HD_A:c0 + (h + 1) * HD_A]
            v_ref[pl.ds(h, tm, stride=KVH_A), :] = res[:, c0 + KV_A + h * HD_A:c0 + KV_A + (h + 1) * HD_A]


def _inproj(x2d, g, w_pad, tm):
    m = x2d.shape[0]
    assert OFF_KA // TN_IN == (OFF_VA + KV_A - 1) // TN_IN
    kv_shape = jax.ShapeDtypeStruct((m * KVH_A, HD_A), f32)
    kv_spec = pl.BlockSpec((tm * KVH_A, HD_A), lambda i, n: (i, 0))
    return pl.pallas_call(
        _inproj_kernel,
        out_shape=(jax.ShapeDtypeStruct((m, PW), f32), kv_shape, kv_shape),
        grid=(m // tm, PW // TN_IN),
        in_specs=[pl.BlockSpec((tm, D_MODEL), lambda i, n: (i, 0)),
                  pl.BlockSpec((1, D_MODEL), lambda i, n: (0, 0)),
                  pl.BlockSpec((D_MODEL, TN_IN), lambda i, n: (0, n))],
        out_specs=(pl.BlockSpec((tm, TN_IN), lambda i, n: (i, n)), kv_spec, kv_spec),
        scratch_shapes=[pltpu.VMEM((tm, D_MODEL), bf16)],
        compiler_params=_cparams(2, VMEM_BIG),
        name="inproj",
    )(x2d, g.reshape(1, D_MODEL), w_pad)


def _topk_select(scores, n_valid, lane_f, n_sel):
    sc = jnp.where(lane_f < n_valid, scores, -jnp.inf)
    sel = jnp.zeros_like(sc)
    for _ in range(n_sel):
        m = jnp.max(sc, axis=-1, keepdims=True)
        idx = jnp.min(jnp.where(sc == m, lane_f, float(LANE)), axis=-1, keepdims=True)
        hit = lane_f == idx
        sel = jnp.maximum(sel, jnp.where(hit, jnp.where(m > -jnp.inf, 1.0, 0.0), 0.0))
        sc = jnp.where(hit, -jnp.inf, sc)
    return sel


def _moba_prompt_kernel(q_ref, k_ref, v_ref, za_ref, o_ref,
                        kb_scr, vb_scr, kmean_scr, bias_scr, off_scr, acc_scr, m_scr, *, nb):
    g = pl.program_id(0)
    i = pl.program_id(1)
    rows = GROUP_A * MOBA_BLOCK
    wide = MOBA_GB * MOBA_BLOCK

    @pl.when(i == 0)
    def _():
        kb_scr[...] = k_ref[...].astype(bf16)
        vb_scr[:, 0:HD_A] = v_ref[...].astype(bf16)
        vb_scr[:, HD_A:] = jnp.ones((nb * MOBA_BLOCK, HD_A), bf16)
        kmean_scr[...] = jnp.zeros_like(kmean_scr)
        for n in range(nb):
            kmean_scr[n:n + 1, :] = jnp.mean(k_ref[n * MOBA_BLOCK:(n + 1) * MOBA_BLOCK, :], axis=0, keepdims=True)
        row = lax.broadcasted_iota(jnp.int32, (rows, wide), 0)
        col = lax.broadcasted_iota(jnp.int32, (rows, wide), 1)
        second = row >= MOBA_BLOCK
        gf = jnp.zeros((rows, wide), f32) + g.astype(f32)
        slope = jnp.exp2(-(2.0 * gf + 1.0) - jnp.where(second, 1.0, 0.0)) * LOG2E
        rin = row - jnp.where(second, MOBA_BLOCK, 0)
        bias_scr[...] = slope * (col - rin).astype(f32)
        off_scr[...] = slope[:, :LANE] * float(MOBA_BLOCK)

    q = q_ref[...]
    q2 = jnp.concatenate([q[:, :HD_A], q[:, HD_A:]], axis=0)
    lane_f = lax.broadcasted_iota(jnp.int32, (rows, LANE), 1).astype(f32)
    scores = lax.dot_general(q2, kmean_scr[...], NT, precision=HIGHEST, preferred_element_type=f32)
    i_f = jnp.zeros((rows, LANE), f32) + i.astype(f32)
    sel = _topk_select(scores, i_f, lane_f, min(MOBA_TOPK, nb))
    selneg = jnp.where(sel > 0.0, 0.0, NEG)
    lhs = jnp.concatenate([q2.astype(bf16), selneg.astype(bf16)], axis=1)

    acc_scr[...] = jnp.zeros_like(acc_scr)
    m_scr[...] = jnp.full_like(m_scr, -jnp.inf)

    def attend(n, width, aux, dist, causal):
        off = pl.multiple_of(n * MOBA_BLOCK, MOBA_BLOCK)
        rhs = jnp.concatenate([kb_scr[pl.ds(off, width), :], aux], axis=1)
        vblk = vb_scr[pl.ds(off, width), :]
        rss = [slice(c * MOBA_RC, (c + 1) * MOBA_RC) for c in range(rows // MOBA_RC)]
        t = [lax.dot_general(lhs[rs], rhs, NT, preferred_element_type=f32) * ATT_C1 + bias_scr[rs, :width]
             for rs in rss]
        if causal:
            c_i = lax.broadcasted_iota(jnp.int32, (MOBA_RC, width), 1)
            r_i = lax.broadcasted_iota(jnp.int32, (MOBA_RC, width), 0)
            t = [jnp.where(c_i <= r_i + (k * MOBA_RC) % MOBA_BLOCK, t[k], NEG) for k in range(len(rss))]
        m_prev = [m_scr[rs] for rs in rss]
        m_cur = [jnp.max(x, axis=1, keepdims=True) for x in t]
        shift = None
        if dist is not None:
            shift = [off_scr[rs] * dist for rs in rss]
            m_cur = [mc - sh for mc, sh in zip(m_cur, shift)]
        m_new = [jnp.maximum(a, b) for a, b in zip(m_prev, m_cur)]
        alpha = [jnp.exp2(a - b) for a, b in zip(m_prev, m_new)]
        sub = m_new if shift is None else [a + b for a, b in zip(m_new, shift)]
        p = [jnp.exp2(x - jnp.concatenate([s] * (width // LANE), axis=1)).astype(bf16) for x, s in zip(t, sub)]
        pv = [_dot(x, vblk) for x in p]
        for k, rs in enumerate(rss):
            acc_scr[rs] = acc_scr[rs] * jnp.concatenate([alpha[k], alpha[k]], axis=1) + pv[k]
            m_scr[rs] = m_new[k]

    def dvec(d):
        return jnp.zeros((MOBA_RC, LANE), f32) + d

    attend(i, MOBA_BLOCK, jnp.zeros((MOBA_BLOCK, LANE), bf16), None, True)

    def past(n, nblk):
        w = nblk * MOBA_BLOCK
        lane_k = lax.broadcasted_iota(jnp.int32, (w, LANE), 1)
        blk_k = lax.shift_right_logical(lax.broadcasted_iota(jnp.int32, (w, LANE), 0), int(math.log2(MOBA_BLOCK)))
        aux = jnp.where(lane_k == n + blk_k, 1.0, 0.0).astype(bf16)
        attend(n, w, aux, dvec((i - n).astype(f32)), False)

    def body(j, c):
        past(MOBA_GB * j, MOBA_GB)
        return c

    ngroups = lax.div(i, MOBA_GB)
    lax.fori_loop(0, ngroups, body, 0)
    done = ngroups * MOBA_GB
    nblk = MOBA_GB // 2
    while nblk >= 1:
        take = lax.rem(lax.div(i, nblk), 2) == 1

        @pl.when(take)
        def _(done=done, nblk=nblk):
            past(done, nblk)

        done = done + jnp.where(take, nblk, 0)
        nblk //= 2

    acc = acc_scr[...]
    o = acc[:, :HD_A] / acc[:, HD_A:]
    o2 = jnp.concatenate([o[:MOBA_BLOCK], o[MOBA_BLOCK:]], axis=1)
    o_ref[...] = o2 * _silu(za_ref[...])


def _moba_prompt(proj):
    L = proj.shape[0]
    nb = L // MOBA_BLOCK
    rows = GROUP_A * MOBA_BLOCK
    wq = GROUP_A * HD_A
    return pl.pallas_call(
        functools.partial(_moba_prompt_kernel, nb=nb),
        out_shape=jax.ShapeDtypeStruct((L, Q_A), f32),
        grid=(KVH_A, nb),
        in_specs=[pl.BlockSpec((MOBA_BLOCK, wq), lambda g, i: (i, OFF_QA // wq + g)),
                  pl.BlockSpec((L, HD_A), lambda g, i: (0, OFF_KA // HD_A + g)),
                  pl.BlockSpec((L, HD_A), lambda g, i: (0, OFF_VA // HD_A + g)),
                  pl.BlockSpec((MOBA_BLOCK, wq), lambda g, i: (i, OFF_ZA // wq + g))],
        out_specs=pl.BlockSpec((MOBA_BLOCK, wq), lambda g, i: (i, g)),
        scratch_shapes=[pltpu.VMEM((L, HD_A), bf16),
                        pltpu.VMEM((L, 2 * HD_A), bf16),
                        pltpu.VMEM((LANE, HD_A), f32),
                        pltpu.VMEM((rows, MOBA_GB * MOBA_BLOCK), f32),
                        pltpu.VMEM((rows, LANE), f32),
                        pltpu.VMEM((rows, 2 * HD_A), f32),
                        pltpu.VMEM((rows, HD_A), f32)],
        compiler_params=_cparams(2, VMEM_BIG),
        name="moba_prompt",
    )(proj, proj, proj, proj)


def _moba_decode_kernel(pt_ref, q_ref, kn_ref, vn_ref, za_ref, ck_ref, cv_ref, o_ref,
                        kbuf, vbuf, sems, *, layer, n_pages, nbatch):
    b = pl.program_id(0)
    past = n_pages * PAGE_SIZE
    nbp = past // MOBA_BLOCK
    prow = PAGE_SIZE * KVH_A

    def copies(bb, slot):
        out = []
        for p in range(n_pages):
            page = pt_ref[bb * n_pages + p]
            out.append(pltpu.make_async_copy(ck_ref.at[layer, page], kbuf.at[slot, pl.ds(p * prow, prow), :],
                                             sems.at[0, slot]))
            out.append(pltpu.make_async_copy(cv_ref.at[layer, page], vbuf.at[slot, pl.ds(p * prow, prow), :],
                                             sems.at[1, slot]))
        return out

    slot = lax.rem(b, 2)

    @pl.when(b == 0)
    def _():
        for c in copies(0, 0):
            c.start()

    @pl.when(b + 1 < nbatch)
    def _():
        for c in copies(b + 1, 1 - slot):
            c.start()

    for c in copies(b, slot):
        c.wait()

    q = q_ref[0]
    kn = kn_ref[0].astype(bf16).astype(f32)
    vn = vn_ref[0].astype(bf16).astype(f32)
    lane_f = lax.broadcasted_iota(jnp.int32, (SUBLANE, LANE), 1).astype(f32)
    rrow = lax.broadcasted_iota(jnp.int32, (SUBLANE, past), 0).astype(f32)
    pos = lax.broadcasted_iota(jnp.int32, (SUBLANE, past), 1)
    dist = (past - pos).astype(f32)
    zpad = jnp.zeros((SUBLANE - GROUP_A, HD_A), f32)
    groups = range(KVH_A)
    hs = [slice(g * HD_A, (g + 1) * HD_A) for g in groups]
    kf = [kbuf[slot, pl.ds(g, past, stride=KVH_A), :] for g in groups]
    qg = [jnp.concatenate([q[:, (GROUP_A * g + r) * HD_A:(GROUP_A * g + r + 1) * HD_A] for r in range(GROUP_A)]
                          + [zpad], axis=0) for g in groups]
    qb = [x.astype(bf16) for x in qg]
    s = [lax.dot_general(qb[g], kf[g].astype(bf16), NT, preferred_element_type=f32) for g in groups]
    kmean = [jnp.concatenate(
        [jnp.mean(kf[g][n * MOBA_BLOCK:(n + 1) * MOBA_BLOCK, :], axis=0, keepdims=True) for n in range(nbp)]
        + [jnp.zeros((LANE - nbp, HD_A), f32)], axis=0) for g in groups]
    scores = [lax.dot_general(qg[g], kmean[g], NT, precision=HIGHEST, preferred_element_type=f32) for g in groups]
    sel = [_topk_select(scores[g], float(nbp), lane_f, min(MOBA_TOPK, nbp + 1)) for g in groups]
    vb = [vbuf[slot, pl.ds(g, past, stride=KVH_A), :].astype(bf16) for g in groups]
    pb, p_own, l = [], [], []
    for g in groups:
        slope = jnp.exp2(-(rrow + float(GROUP_A * g + 1)))
        sg = s[g] * ATT_SCALE - slope * dist
        selx = jnp.concatenate([jnp.broadcast_to(sel[g][:, n:n + 1], (SUBLANE, MOBA_BLOCK)) for n in range(nbp)],
                               axis=1)
        sg = jnp.where(selx > 0.0, sg, NEG)
        s_own = jnp.sum(qb[g].astype(f32) * kn[:, hs[g]], axis=1, keepdims=True) * ATT_SCALE
        m = jnp.maximum(jnp.max(sg, axis=1, keepdims=True), s_own)
        pb.append(jnp.exp(sg - m).astype(bf16))
        p_own.append(jnp.exp(s_own - m).astype(bf16).astype(f32))
        l.append(jnp.sum(pb[g].astype(f32), axis=1, keepdims=True) + p_own[g])
    pv = [_dot(pb[g], vb[g]) for g in groups]
    outs = []
    for g in groups:
        o = (pv[g] + p_own[g] * vn[:, hs[g]]) / l[g]
        outs.extend(o[r:r + 1, :] for r in range(GROUP_A))
    o_ref[0] = jnp.concatenate(outs, axis=1) * _silu(za_ref[0])


def _moba_decode(proj_s, cache_k, cache_v, page_table, layer):
    nbatch, n_pages = page_table.shape
    past = n_pages * PAGE_SIZE
    n_phys = cache_k.shape[1]
    ck = cache_k.reshape(cache_k.shape[0], n_phys, PAGE_SIZE * KVH_A, HD_A)
    cv = cache_v.reshape(cache_v.shape[0], n_phys, PAGE_SIZE * KVH_A, HD_A)
    p3 = proj_s.reshape(nbatch, 1, PW)
    grid_spec = pltpu.PrefetchScalarGridSpec(
        num_scalar_prefetch=1,
        grid=(nbatch,),
        in_specs=[pl.BlockSpec((1, 1, Q_A), lambda b, pt: (b, 0, OFF_QA // Q_A)),
                  pl.BlockSpec((1, 1, KV_A), lambda b, pt: (b, 0, OFF_KA // KV_A)),
                  pl.BlockSpec((1, 1, KV_A), lambda b, pt: (b, 0, OFF_VA // KV_A)),
                  pl.BlockSpec((1, 1, Q_A), lambda b, pt: (b, 0, OFF_ZA // Q_A)),
                  pl.BlockSpec(memory_space=pl.ANY),
                  pl.BlockSpec(memory_space=pl.ANY)],
        out_specs=pl.BlockSpec((1, 1, Q_A), lambda b, pt: (b, 0, 0)),
        scratch_shapes=[pltpu.VMEM((2, past * KVH_A, HD_A), f32),
                        pltpu.VMEM((2, past * KVH_A, HD_A), f32),
                        pltpu.SemaphoreType.DMA((2, 2))],
    )
    out = pl.pallas_call(
        functools.partial(_moba_decode_kernel, layer=layer, n_pages=n_pages, nbatch=nbatch),
        out_shape=jax.ShapeDtypeStruct((nbatch, 1, Q_A), f32),
        grid_spec=grid_spec,
        compiler_params=_cparams(1, VMEM_BIG),
        name="moba_decode",
    )(page_table.reshape(-1), p3, p3, p3, p3, ck, cv)
    return out.reshape(nbatch, Q_A)


GDN_T = 128
GDN_HB = 8


def _gdn_prompt_kernel(alog_ref, dtb_ref, q_ref, k_ref, v_ref, ab_ref, zb_ref, wq_ref, wk_ref, wv_ref, ng_ref,
                       o_ref, s_out_ref, s_scr, prev_scr, vnew_scr, *, T):
    hp = pl.program_id(0)
    t = pl.program_id(1)
    C = min(DN_CHUNK, T)
    n_chunks = T // C
    shift = int(math.log2(C))
    heads = range(GDN_HB)

    @pl.when(t == 0)
    def _():
        s_scr[...] = jnp.zeros_like(s_scr)
        prev_scr[...] = jnp.zeros_like(prev_scr)
        vnew_scr[...] = jnp.zeros_like(vnew_scr)

    row = lax.broadcasted_iota(jnp.int32, (T, T), 0)
    col = lax.broadcasted_iota(jnp.int32, (T, T), 1)
    same = lax.shift_right_logical(row, shift) == lax.shift_right_logical(col, shift)
    tri_incl = jnp.logical_and(same, row >= col)
    tri_strict = jnp.logical_and(same, row > col)
    ltri = jnp.where(tri_incl, 1.0, 0.0)
    ab = ab_ref[...]
    lane = lax.broadcasted_iota(jnp.int32, (T, LANE), 1)
    r8 = lax.broadcasted_iota(jnp.int32, (SUBLANE, LANE), 0)
    colT = lax.broadcasted_iota(jnp.int32, (DK_B, T), 1)

    reps = T // LANE

    def conv(x_ref, w_ref, hb, which):
        sl = slice(hb * LANE, (hb + 1) * LANE)
        x = x_ref[:, sl]
        w = w_ref[:, sl]
        prev8 = prev_scr[3 * hb + which]
        acc = x * w[CONV_W - 1:CONV_W, :]
        for s in range(1, CONV_W):
            xs = pltpu.roll(x, s, axis=0)
            top = jnp.where(r8 < s, pltpu.roll(prev8, s, axis=0), xs[:SUBLANE])
            xs = jnp.concatenate([top, xs[SUBLANE:]], axis=0)
            acc = acc + xs * w[CONV_W - 1 - s:CONV_W - s, :]
        prev_scr[3 * hb + which] = x[T - SUBLANE:, :]
        return _silu(acc)

    qn, kn, vc, beta, g_rep = [], [], [], [], []
    for hb in heads:
        h = hp * GDN_HB + hb
        qc = conv(q_ref, wq_ref, hb, 0)
        kc = conv(k_ref, wk_ref, hb, 1)
        vc.append(conv(v_ref, wv_ref, hb, 2))
        qn.append(qc * lax.rsqrt(jnp.sum(qc * qc, axis=-1, keepdims=True) + EPS) * (DK_B ** -0.5))
        kn.append(kc * lax.rsqrt(jnp.sum(kc * kc, axis=-1, keepdims=True) + EPS))
        a = jnp.sum(jnp.where(lane == h, ab, 0.0), axis=-1, keepdims=True)
        b = jnp.sum(jnp.where(lane == H_B + h, ab, 0.0), axis=-1, keepdims=True)
        beta.append(_sigmoid(b))
        glog = -jnp.exp(jnp.zeros((T, 1), f32) + alog_ref[h]) * _softplus(a + dtb_ref[h])
        g_rep.append(jnp.broadcast_to(glog, (T, LANE)))

    gam = [jnp.dot(ltri, g_rep[hb], precision=HIGHEST, preferred_element_type=f32) for hb in heads]
    gam_row = [gam[hb].T for hb in heads]
    kb = [kn[hb].astype(bf16) for hb in heads]
    kk = [lax.dot_general(kb[hb], kb[hb], NT, preferred_element_type=f32) for hb in heads]
    qk_raw = [lax.dot_general(qn[hb].astype(bf16), kb[hb], NT, preferred_element_type=f32) for hb in heads]

    decay, pw = [], []
    for hb in heads:
        gam_i = jnp.concatenate([gam[hb]] * reps, axis=1)
        gam_j = jnp.concatenate([gam_row[hb]] * reps, axis=0)
        decay.append(jnp.exp(jnp.where(tri_incl, gam_i - gam_j, -jnp.inf)))
        pw.append(jnp.where(tri_strict, -(jnp.broadcast_to(beta[hb], (T, T)) * kk[hb] * decay[hb]), 0.0))

    x_acc = list(pw)
    pwb = [pw[hb].astype(bf16) for hb in heads]
    for r in range(1, shift):
        pw = [_dot(pwb[hb], pwb[hb]) for hb in heads]
        pwb = [pw[hb].astype(bf16) for hb in heads]
        x_acc = [x_acc[hb] + pw[hb] + _dot(pwb[hb], x_acc[hb].astype(bf16)) for hb in heads]

    egam = [jnp.exp(gam[hb]) for hb in heads]
    rhs = [jnp.concatenate([vc[hb] * beta[hb], kn[hb] * (beta[hb] * egam[hb])], axis=1) for hb in heads]
    sol = [rhs[hb] + _dot(x_acc[hb].astype(bf16), rhs[hb].astype(bf16)) for hb in heads]
    u_val = [sol[hb][:, :DV_B] for hb in heads]
    k_cum_b = [sol[hb][:, DV_B:].astype(bf16) for hb in heads]
    qkb = [(qk_raw[hb] * decay[hb]).astype(bf16) for hb in heads]
    q_dec = [(qn[hb] * egam[hb]).astype(bf16) for hb in heads]
    knT = [kn[hb].T for hb in heads]

    outs = [[] for _ in heads]
    for c in range(n_chunks):
        rs = slice(c * C, (c + 1) * C)
        last = (c + 1) * C - 1
        in_chunk = jnp.logical_and(colT >= c * C, colT <= last)
        s_old = [s_scr[hb] for hb in heads]
        s_b = [s_old[hb].astype(bf16) for hb in heads]
        v_new = [u_val[hb][rs] - _dot(k_cum_b[hb][rs], s_b[hb]) for hb in heads]
        o_state = [_dot(q_dec[hb][rs], s_b[hb]) for hb in heads]
        for hb in heads:
            vnew_scr[hb, rs, :] = v_new[hb]
        vn_b = [vnew_scr[hb].astype(bf16) for hb in heads]
        for hb in heads:
            outs[hb].append(o_state[hb] + _dot(qkb[hb][rs, :], vn_b[hb]))
        for hb in heads:
            w_end = jnp.exp(jnp.where(in_chunk, gam_row[hb][:, last:last + 1] - gam_row[hb], -jnp.inf))
            k_end_t = (knT[hb] * w_end).astype(bf16)
            s_scr[hb] = s_old[hb] * jnp.exp(gam[hb][last:last + 1, :]) + _dot(k_end_t, vn_b[hb])

    for hb in heads:
        sl = slice(hb * LANE, (hb + 1) * LANE)
        o = jnp.concatenate(outs[hb], axis=0)
        o_ref[:, sl] = _rms(o, ng_ref[...]) * _silu(zb_ref[:, sl])
        s_out_ref[hb] = s_scr[hb]


def _gdn_prompt(proj, conv_w, a_log, dt_bias, norm_g):
    L = proj.shape[0]
    T = min(GDN_T, L)
    wb = GDN_HB * LANE
    nq = H_B * DK_B // wb
    return pl.pallas_call(
        functools.partial(_gdn_prompt_kernel, T=T),
        out_shape=(jax.ShapeDtypeStruct((L, V_B), f32), jax.ShapeDtypeStruct((H_B, DK_B, DV_B), f32)),
        grid=(H_B // GDN_HB, L // T),
        in_specs=[pl.BlockSpec(memory_space=pltpu.SMEM),
                  pl.BlockSpec(memory_space=pltpu.SMEM),
                  pl.BlockSpec((T, wb), lambda hp, t: (t, OFF_CONV // wb + hp)),
                  pl.BlockSpec((T, wb), lambda hp, t: (t, OFF_CONV // wb + nq + hp)),
                  pl.BlockSpec((T, wb), lambda hp, t: (t, OFF_CONV // wb + 2 * nq + hp)),
                  pl.BlockSpec((T, LANE), lambda hp, t: (t, OFF_AB // LANE)),
                  pl.BlockSpec((T, wb), lambda hp, t: (t, OFF_ZB // wb + hp)),
                  pl.BlockSpec((CONV_W, wb), lambda hp, t: (0, hp)),
                  pl.BlockSpec((CONV_W, wb), lambda hp, t: (0, nq + hp)),
                  pl.BlockSpec((CONV_W, wb), lambda hp, t: (0, 2 * nq + hp)),
                  pl.BlockSpec((1, DV_B), lambda hp, t: (0, 0))],
        out_specs=(pl.BlockSpec((T, wb), lambda hp, t: (t, hp)),
                   pl.BlockSpec((GDN_HB, DK_B, DV_B), lambda hp, t: (hp, 0, 0))),
        scratch_shapes=[pltpu.VMEM((GDN_HB, DK_B, DV_B), f32),
                        pltpu.VMEM((3 * GDN_HB, SUBLANE, LANE), f32),
                        pltpu.VMEM((GDN_HB, T, DV_B), f32)],
        compiler_params=_cparams(2, VMEM_BIG),
        name="gdn_prompt",
    )(a_log, dt_bias, proj, proj, proj, proj, proj, conv_w, conv_w, conv_w, norm_g.reshape(1, DV_B))


GDN_BB = 8


def _gdn_decode_kernel(x_ref, ab_ref, zb_ref, cs_ref, sd_ref, w_ref, alog_ref, dtb_ref, ng_ref,
                       o_ref, cs_out_ref, sd_out_ref):
    x = x_ref[...]
    w = w_ref[...]
    c0, c1, c2 = cs_ref[:, 0, :], cs_ref[:, 1, :], cs_ref[:, 2, :]
    y = c0 * w[0:1, :] + c1 * w[1:2, :] + c2 * w[2:3, :] + x * w[3:4, :]
    cs_out_ref[:, 0, :] = c1
    cs_out_ref[:, 1, :] = c2
    cs_out_ref[:, 2, :] = x
    y = _silu(y)
    ab = ab_ref[...]
    a = ab[:, 0:H_B]
    beta = _sigmoid(ab[:, H_B:2 * H_B])
    g = -jnp.exp(alog_ref[...]) * _softplus(a + dtb_ref[...])
    eg = jnp.exp(g)
    nq = H_B * DK_B
    outs = []
    for h in range(H_B):
        qc = y[:, h * DK_B:(h + 1) * DK_B]
        kc = y[:, nq + h * DK_B:nq + (h + 1) * DK_B]
        vc = y[:, 2 * nq + h * DV_B:2 * nq + (h + 1) * DV_B]
        qn = qc * lax.rsqrt(jnp.sum(qc * qc, axis=-1, keepdims=True) + EPS) * (DK_B ** -0.5)
        kn = kc * lax.rsqrt(jnp.sum(kc * kc, axis=-1, keepdims=True) + EPS)
        qk = jnp.sum(qn * kn, axis=-1, keepdims=True)
        knT = kn.T
        qnT = qn.T
        rows = []
        for bb in range(GDN_BB):
            s_old = sd_ref[bb, h]
            kcol = knT[:, bb:bb + 1]
            qcol = qnT[:, bb:bb + 1]
            be = beta[bb:bb + 1, h:h + 1]
            e = eg[bb:bb + 1, h:h + 1]
            ks = jnp.sum(kcol * s_old, axis=0, keepdims=True)
            qs = jnp.sum(qcol * s_old, axis=0, keepdims=True)
            v_new = be * vc[bb:bb + 1, :] - (be * e) * ks
            rows.append(e * qs + qk[bb:bb + 1, :] * v_new)
            sd_out_ref[bb, h] = s_old * e + kcol * v_new
        o = jnp.concatenate(rows, axis=0)
        outs.append(_rms(o, ng_ref[...]) * _silu(zb_ref[:, h * DV_B:(h + 1) * DV_B]))
    o_ref[...] = jnp.concatenate(outs, axis=1)


def _gdn_decode(proj_s, state_conv, state_delta, conv_w, a_log, dt_bias, norm_g, layer):
    nbatch = proj_s.shape[0]
    bb = GDN_BB
    return pl.pallas_call(
        _gdn_decode_kernel,
        out_shape=(jax.ShapeDtypeStruct((nbatch, V_B), f32),
                   jax.ShapeDtypeStruct((nbatch, CONV_W - 1, CONV_CH), f32),
                   jax.ShapeDtypeStruct((nbatch, H_B, DK_B, DV_B), f32)),
        grid=(nbatch // bb,),
        in_specs=[pl.BlockSpec((bb, CONV_CH), lambda i: (i, OFF_CONV // CONV_CH)),
                  pl.BlockSpec((bb, LANE), lambda i: (i, OFF_AB // LANE)),
                  pl.BlockSpec((bb, V_B), lambda i: (i, OFF_ZB // V_B)),
                  pl.BlockSpec((None, bb, CONV_W - 1, CONV_CH), lambda i: (layer, i, 0, 0)),
                  pl.BlockSpec((None, bb, H_B, DK_B, DV_B), lambda i: (layer, i, 0, 0, 0)),
                  pl.BlockSpec((CONV_W, CONV_CH), lambda i: (0, 0)),
                  pl.BlockSpec((1, H_B), lambda i: (0, 0)),
                  pl.BlockSpec((1, H_B), lambda i: (0, 0)),
                  pl.BlockSpec((1, DV_B), lambda i: (0, 0))],
        out_specs=(pl.BlockSpec((bb, V_B), lambda i: (i, 0)),
                   pl.BlockSpec((bb, CONV_W - 1, CONV_CH), lambda i: (i, 0, 0)),
                   pl.BlockSpec((bb, H_B, DK_B, DV_B), lambda i: (i, 0, 0, 0))),
        compiler_params=_cparams(1, VMEM_BIG),
        name="gdn_decode",
    )(proj_s, proj_s, proj_s, state_conv, state_delta, conv_w,
      a_log.reshape(1, H_B), dt_bias.reshape(1, H_B), norm_g.reshape(1, DV_B))


S5_JB = W_C // LANE
S5_GPB = LANE // S5_GROUP_CH
S5_SW = S5_GPB * S5_STATE
S5_TT = 512
S5_NSEG = SUBLANE
S5_PAD = 4


def _s5_param_kernel(lr_ref, li_ref, ldt_ref, bre_ref, bim_ref, are_ref, aim_ref, bbre_ref, bbim_ref):
    lr = lr_ref[...]
    li = li_ref[...]
    dt = jnp.exp(ldt_ref[...])
    mag = jnp.exp(lr * dt)
    ang = li * dt
    a_re = mag * jnp.cos(ang)
    a_im = mag * jnp.sin(ang)
    den = lr * lr + li * li
    f_re = ((a_re - 1.0) * lr + a_im * li) / den
    f_im = (a_im * lr - (a_re - 1.0) * li) / den
    are_ref[...] = a_re
    aim_ref[...] = a_im
    fr = f_re[:, None, :]
    fi = f_im[:, None, :]
    bre = bre_ref[...]
    bim = bim_ref[...]
    bbre_ref[...] = fr * bre - fi * bim
    bbim_ref[...] = fr * bim + fi * bre


def _s5_params(lam_re, lam_im, log_dt, b_re, b_im, c_re, c_im, d):
    G, P, CH = S5_GROUPS, S5_STATE, S5_GROUP_CH
    bt_re = jnp.swapaxes(b_re, 1, 2)
    bt_im = jnp.swapaxes(b_im, 1, 2)
    a_re, a_im, bb_re, bb_im = pl.pallas_call(
        _s5_param_kernel,
        out_shape=(jax.ShapeDtypeStruct((G, P), f32), jax.ShapeDtypeStruct((G, P), f32),
                   jax.ShapeDtypeStruct((G, CH, P), f32), jax.ShapeDtypeStruct((G, CH, P), f32)),
        name="s5_params",
    )(lam_re, lam_im, log_dt.reshape(G, 1), bt_re, bt_im)
    eye = jnp.eye(S5_GPB, dtype=f32)

    def bdiag_in(bb):
        x = bb.reshape(S5_JB, S5_GPB, CH, P)
        return jnp.einsum("jacp,ab->jacbp", x, eye).reshape(S5_JB, LANE, S5_SW)

    def bdiag_out(c):
        x = c.reshape(S5_JB, S5_GPB, CH, P)
        return jnp.einsum("jacp,ab->jbpac", x, eye).reshape(S5_JB, S5_SW, LANE)

    wb = jnp.concatenate([bdiag_in(bb_re), bdiag_in(bb_im)], axis=-1).astype(bf16)
    wc_re = bdiag_out(c_re).astype(bf16)
    wc_im = bdiag_out(c_im).astype(bf16)
    return dict(wb=wb, wc_re=wc_re, wc_im=wc_im,
                a_re=a_re.reshape(S5_JB, 1, S5_SW), a_im=a_im.reshape(S5_JB, 1, S5_SW),
                d=d.reshape(S5_JB, 1, LANE))


def _s5_prompt_kernel(u_ref, wb_ref, wcre_ref, wcim_ref, are_ref, aim_ref, d_ref,
                      y_ref, hre_ref, him_ref, a_scr, b_scr, carry_scr, *, TT):
    t = pl.program_id(1)
    nlb = S5_SW // LANE
    seg = TT // S5_NSEG
    pitch = seg + S5_PAD

    @pl.when(t == 0)
    def _():
        carry_scr[...] = jnp.zeros_like(carry_scr)

    u = u_ref[...]
    zpad = jnp.zeros((S5_PAD, LANE), f32)
    u_pad = jnp.concatenate([x for s in range(S5_NSEG) for x in (u[s * seg:(s + 1) * seg], zpad)], axis=0)
    bu = _dot(u_pad.astype(bf16), wb_ref[...])
    for c in range(2 * nlb):
        a_scr[c] = bu[:, c * LANE:(c + 1) * LANE]

    are = are_ref[...]
    aim = aim_ref[...]
    ar8 = [jnp.broadcast_to(are[:, c * LANE:(c + 1) * LANE], (S5_NSEG, LANE)) for c in range(nlb)]
    ai8 = [jnp.broadcast_to(aim[:, c * LANE:(c + 1) * LANE], (S5_NSEG, LANE)) for c in range(nlb)]

    def seg_rows(k):
        return pl.ds(k, S5_NSEG, stride=pitch)

    zero8 = jnp.zeros((S5_NSEG, LANE), f32)
    hr = [zero8] * nlb
    hi = [zero8] * nlb
    for k in range(seg):
        for c in range(nlb):
            nr = ar8[c] * hr[c] - ai8[c] * hi[c] + a_scr[c, seg_rows(k), :]
            ni = ar8[c] * hi[c] + ai8[c] * hr[c] + a_scr[nlb + c, seg_rows(k), :]
            b_scr[c, seg_rows(k), :] = nr
            b_scr[nlb + c, seg_rows(k), :] = ni
            hr[c], hi[c] = nr, ni
    ends = hr + hi

    pr, pi = are, aim
    for _ in range(int(math.log2(seg))):
        pr, pi = pr * pr - pi * pi, 2.0 * pr * pi

    carry = carry_scr[...]
    er, ei = carry[:, :S5_SW], carry[:, S5_SW:]
    rows_r, rows_i = [], []
    for s in range(S5_NSEG):
        rows_r.append(er)
        rows_i.append(ei)
        loc_r = jnp.concatenate([ends[c][s:s + 1, :] for c in range(nlb)], axis=1)
        loc_i = jnp.concatenate([ends[nlb + c][s:s + 1, :] for c in range(nlb)], axis=1)
        er, ei = pr * er - pi * ei + loc_r, pr * ei + pi * er + loc_i
    carry_scr[...] = jnp.concatenate([er, ei], axis=1)
    hre_ref[0] = er
    him_ref[0] = ei
    ein_r = jnp.concatenate(rows_r, axis=0)
    ein_i = jnp.concatenate(rows_i, axis=0)

    pw_r = list(ar8)
    pw_i = list(ai8)
    for k in range(seg):
        for c in range(nlb):
            e_r = ein_r[:, c * LANE:(c + 1) * LANE]
            e_i = ein_i[:, c * LANE:(c + 1) * LANE]
            a_scr[c, seg_rows(k), :] = b_scr[c, seg_rows(k), :] + (pw_r[c] * e_r - pw_i[c] * e_i)
            a_scr[nlb + c, seg_rows(k), :] = b_scr[nlb + c, seg_rows(k), :] + (pw_r[c] * e_i + pw_i[c] * e_r)
            if k + 1 < seg:
                pw_r[c], pw_i[c] = pw_r[c] * ar8[c] - pw_i[c] * ai8[c], pw_r[c] * ai8[c] + pw_i[c] * ar8[c]

    h_re = jnp.concatenate([a_scr[c] for c in range(nlb)], axis=1).astype(bf16)
    h_im = jnp.concatenate([a_scr[nlb + c] for c in range(nlb)], axis=1).astype(bf16)
    y_pad = _dot(h_re, wcre_ref[...]) - _dot(h_im, wcim_ref[...])
    y = jnp.concatenate([y_pad[s * pitch:s * pitch + seg] for s in range(S5_NSEG)], axis=0)
    y_ref[...] = y + d_ref[...] * u


def _s5_prompt(proj, sp):
    L = proj.shape[0]
    TT = min(S5_TT, L)
    jb_spec = lambda shape: pl.BlockSpec((None,) + shape, lambda j, t: (j, 0, 0))
    y, hre, him = pl.pallas_call(
        functools.partial(_s5_prompt_kernel, TT=TT),
        out_shape=(jax.ShapeDtypeStruct((L, W_C), f32),
                   jax.ShapeDtypeStruct((S5_JB, 1, S5_SW), f32),
                   jax.ShapeDtypeStruct((S5_JB, 1, S5_SW), f32)),
        grid=(S5_JB, L // TT),
        in_specs=[pl.BlockSpec((TT, LANE), lambda j, t: (t, OFF_UC // LANE + j)),
                  jb_spec((LANE, 2 * S5_SW)), jb_spec((S5_SW, LANE)), jb_spec((S5_SW, LANE)),
                  jb_spec((1, S5_SW)), jb_spec((1, S5_SW)), jb_spec((1, LANE))],
        out_specs=(pl.BlockSpec((TT, LANE), lambda j, t: (t, j)),
                   pl.BlockSpec((1, 1, S5_SW), lambda j, t: (j, 0, 0)),
                   pl.BlockSpec((1, 1, S5_SW), lambda j, t: (j, 0, 0))),
        scratch_shapes=[pltpu.VMEM((2 * S5_SW // LANE, TT + S5_NSEG * S5_PAD, LANE), f32),
                        pltpu.VMEM((2 * S5_SW // LANE, TT + S5_NSEG * S5_PAD, LANE), f32),
                        pltpu.VMEM((1, 2 * S5_SW), f32)],
        compiler_params=_cparams(2, VMEM_BIG),
        name="s5_prompt",
    )(proj, sp["wb"], sp["wc_re"], sp["wc_im"], sp["a_re"], sp["a_im"], sp["d"])
    return y, hre.reshape(S5_GROUPS, S5_STATE), him.reshape(S5_GROUPS, S5_STATE)


def _s5_decode_kernel(u_ref, h0r_ref, h0i_ref, wb_ref, wcre_ref, wcim_ref, are_ref, aim_ref, d_ref,
                      y_ref, hr_ref, hi_ref):
    u = u_ref[...]
    bu = _dot(u.astype(bf16), wb_ref[...])
    are, aim = are_ref[...], aim_ref[...]
    h0r, h0i = h0r_ref[...], h0i_ref[...]
    hr = bu[:, :S5_SW] + are * h0r - aim * h0i
    hi = bu[:, S5_SW:] + are * h0i + aim * h0r
    hr_ref[...] = hr
    hi_ref[...] = hi
    y_ref[...] = _dot(hr.astype(bf16), wcre_ref[...]) - _dot(hi.astype(bf16), wcim_ref[...]) + d_ref[...] * u


def _s5_decode(proj_s, st_re, st_im, sp, layer):
    nbatch = proj_s.shape[0]
    sre = st_re.reshape(st_re.shape[0], nbatch, S5_GROUPS * S5_STATE)
    sim = st_im.reshape(st_im.shape[0], nbatch, S5_GROUPS * S5_STATE)
    jb_spec = lambda shape: pl.BlockSpec((None,) + shape, lambda j: (j, 0, 0))
    st_spec = pl.BlockSpec((None, nbatch, S5_SW), lambda j: (layer, 0, j))
    y, hr, hi = pl.pallas_call(
        _s5_decode_kernel,
        out_shape=(jax.ShapeDtypeStruct((nbatch, W_C), f32),
                   jax.ShapeDtypeStruct((nbatch, S5_GROUPS * S5_STATE), f32),
                   jax.ShapeDtypeStruct((nbatch, S5_GROUPS * S5_STATE), f32)),
        grid=(S5_JB,),
        in_specs=[pl.BlockSpec((nbatch, LANE), lambda j: (0, OFF_UC // LANE + j)),
                  st_spec, st_spec,
                  jb_spec((LANE, 2 * S5_SW)), jb_spec((S5_SW, LANE)), jb_spec((S5_SW, LANE)),
                  jb_spec((1, S5_SW)), jb_spec((1, S5_SW)), jb_spec((1, LANE))],
        out_specs=(pl.BlockSpec((nbatch, LANE), lambda j: (0, j)),
                   pl.BlockSpec((nbatch, S5_SW), lambda j: (0, j)),
                   pl.BlockSpec((nbatch, S5_SW), lambda j: (0, j))),
        compiler_params=_cparams(1),
        name="s5_decode",
    )(proj_s, sre, sim, sp["wb"], sp["wc_re"], sp["wc_im"], sp["a_re"], sp["a_im"], sp["d"])
    return (y, hr.reshape(nbatch, S5_GROUPS, S5_STATE), hi.reshape(nbatch, S5_GROUPS, S5_STATE))


def _s5_glu_kernel(y_ref, zc_ref, w_ref, b_ref, o_ref):
    y = y_ref[...]
    ge = 0.5 * y * (1.0 + lax.erf(y * (0.5 ** 0.5)))
    gl = _dot(ge.astype(bf16), w_ref[...]) + b_ref[...]
    o_ref[...] = ge * _sigmoid(gl) * _silu(zc_ref[...])


def _s5_glu(y, proj, w_glu, b_glu, tm):
    m = y.shape[0]
    return pl.pallas_call(
        _s5_glu_kernel,
        out_shape=jax.ShapeDtypeStruct((m, W_C), f32),
        grid=(m // tm,),
        in_specs=[pl.BlockSpec((tm, W_C), lambda i: (i, 0)),
                  pl.BlockSpec((tm, W_C), lambda i: (i, OFF_ZC // W_C)),
                  pl.BlockSpec((W_C, W_C), lambda i: (0, 0)),
                  pl.BlockSpec((1, W_C), lambda i: (0, 0))],
        out_specs=pl.BlockSpec((tm, W_C), lambda i: (i, 0)),
        compiler_params=_cparams(1, VMEM_BIG),
        name="s5_glu",
    )(y, proj, w_glu, b_glu.reshape(1, W_C))


def _merge_kernel(oa_ref, ob_ref, oc_ref, ga_ref, gb_ref, gc_ref, wb_ref, o_ref):
    acc = _sigmoid(ga_ref[...]) * _dot(oa_ref[...].astype(bf16), wb_ref[0])
    acc = acc + _sigmoid(gb_ref[...]) * _dot(ob_ref[...].astype(bf16), wb_ref[1])
    acc = acc + _sigmoid(gc_ref[...]) * _dot(oc_ref[...].astype(bf16), wb_ref[2])
    o_ref[...] = acc.astype(bf16)


def _merge(out_a, out_b, out_c, proj, w_branch, tm):
    m = out_a.shape[0]
    act = pl.BlockSpec((tm, Q_A), lambda i: (i, 0))
    gate = lambda k: pl.BlockSpec((tm, D_MODEL), lambda i: (i, OFF_G // D_MODEL + k))
    return pl.pallas_call(
        _merge_kernel,
        out_shape=jax.ShapeDtypeStruct((m, D_MODEL), bf16),
        grid=(m // tm,),
        in_specs=[act, act, act, gate(0), gate(1), gate(2),
                  pl.BlockSpec((3, Q_A, D_MODEL), lambda i: (0, 0, 0), pipeline_mode=pl.Buffered(1))],
        out_specs=pl.BlockSpec((tm, D_MODEL), lambda i: (i, 0)),
        compiler_params=_cparams(1, VMEM_BIG),
        name="merge",
    )(out_a, out_b, out_c, proj, proj, proj, w_branch)


def _post_kernel(x_ref, mg_ref, pe_ref, wout_ref, plew_ref, pleg_ref, png_ref, fng_ref, o_ref, *, final):
    x2 = x_ref[...] + _dot(mg_ref[...], wout_ref[...])
    gate = _sigmoid(_dot(_rms(x2, png_ref[...]).astype(bf16), pleg_ref[...]))
    x3 = x2 + _dot(pe_ref[...].astype(bf16), plew_ref[...]) * gate
    if final:
        x3 = _rms(x3, fng_ref[...])
    o_ref[...] = x3


def _post(x2d, merged, pe, w_out, ple_w, ple_w_gate, ple_norm_g, final_norm_g, final, tm):
    m = x2d.shape[0]
    const = lambda shape: pl.BlockSpec(shape, lambda i: (0, 0), pipeline_mode=pl.Buffered(1))
    return pl.pallas_call(
        functools.partial(_post_kernel, final=final),
        out_shape=jax.ShapeDtypeStruct((m, D_MODEL), f32),
        grid=(m // tm,),
        in_specs=[pl.BlockSpec((tm, D_MODEL), lambda i: (i, 0)),
                  pl.BlockSpec((tm, D_MODEL), lambda i: (i, 0)),
                  pl.BlockSpec((tm, PLE_DIM), lambda i: (i, 0)),
                  const((D_MODEL, D_MODEL)), const((PLE_DIM, D_MODEL)), const((D_MODEL, D_MODEL)),
                  const((1, D_MODEL)), const((1, D_MODEL))],
        out_specs=pl.BlockSpec((tm, D_MODEL), lambda i: (i, 0)),
        compiler_params=_cparams(1, VMEM_BIG),
        name="post",
    )(x2d, merged, pe, w_out, ple_w, ple_w_gate, ple_norm_g.reshape(1, D_MODEL), final_norm_g.reshape(1, D_MODEL))


WP_TN = 1024
WP_TM = 1024


def _wprep_kernel(a_ref, b_ref, o_ref):
    t = pl.program_id(1)
    n_plain = OFF_AB // WP_TN
    lead = 2 * H_B

    @pl.when(t < n_plain)
    def _():
        o_ref[...] = a_ref[...].astype(bf16)

    @pl.when(t == n_plain)
    def _():
        lane = lax.broadcasted_iota(jnp.int32, (WP_TM, WP_TN), 1)
        o_ref[...] = jnp.where(lane < lead, a_ref[...], 0.0).astype(bf16)

    @pl.when(t > n_plain)
    def _():
        x = jnp.concatenate([a_ref[...], b_ref[...]], axis=1)
        o_ref[...] = x[:, lead:lead + WP_TN].astype(bf16)


def _pad_w_in(w_in, layer):
    k = w_in.shape[1]
    n_plain = OFF_AB // WP_TN
    per = WP_TN // LANE

    def a_map(r, t):
        return (layer, r, jnp.where(t <= n_plain, t, t - 1))

    def b_map(r, t):
        return (layer, r, jnp.where(t > n_plain, t * per, per))

    return pl.pallas_call(
        _wprep_kernel,
        out_shape=jax.ShapeDtypeStruct((k, PW), bf16),
        grid=(k // WP_TM, PW // WP_TN),
        in_specs=[pl.BlockSpec((None, WP_TM, WP_TN), a_map),
                  pl.BlockSpec((None, WP_TM, LANE), b_map)],
        out_specs=pl.BlockSpec((WP_TM, WP_TN), lambda r, t: (r, t)),
        compiler_params=_cparams(2, VMEM_BIG),
        name="wprep",
    )(w_in, w_in)


def _row_tile(m, pref):
    return pref if m % pref == 0 else m


def _layer_weights(i, norm_g, w_in, dn_conv_w, dn_a_log, dn_dt_bias, dn_norm_g, s5_lambda_re, s5_lambda_im,
                   s5_b_re, s5_b_im, s5_c_re, s5_c_im, s5_d, s5_log_dt, s5_w_glu, s5_b_glu, w_branch, w_out,
                   ple_w, ple_norm_g, ple_w_gate):
    return dict(
        norm_g=norm_g[i], w_pad=_pad_w_in(w_in, i), conv_w=dn_conv_w[i], a_log=dn_a_log[i], dt_bias=dn_dt_bias[i],
        dn_norm_g=dn_norm_g[i],
        s5=_s5_params(s5_lambda_re[i], s5_lambda_im[i], s5_log_dt[i], s5_b_re[i], s5_b_im[i], s5_c_re[i],
                      s5_c_im[i], s5_d[i]),
        w_glu=s5_w_glu[i].astype(bf16), b_glu=s5_b_glu[i], w_branch=w_branch[i].astype(bf16),
        w_out=w_out[i].astype(bf16), ple_w=ple_w[i].astype(bf16), ple_norm_g=ple_norm_g[i],
        ple_w_gate=ple_w_gate[i].astype(bf16))


def _finish(x2d, proj, out_a, out_b, y_c, pe, lw, final_norm_g, final):
    m = x2d.shape[0]
    out_c = _s5_glu(y_c, proj, lw["w_glu"], lw["b_glu"], _row_tile(m, 512))
    merged = _merge(out_a, out_b, out_c, proj, lw["w_branch"], _row_tile(m, 256))
    return _post(x2d, merged, pe, lw["w_out"], lw["ple_w"], lw["ple_w_gate"], lw["ple_norm_g"], final_norm_g,
                 final, _row_tile(m, 256))


def _prompt_layer(x2d, pe, lw, final_norm_g, final):
    L = x2d.shape[0]
    proj, k_new, v_new = _inproj(x2d, lw["norm_g"], lw["w_pad"], _row_tile(L, 1024))
    out_a = _moba_prompt(proj)
    out_b, s_new = _gdn_prompt(proj, lw["conv_w"], lw["a_log"], lw["dt_bias"], lw["dn_norm_g"])
    y_c, hre, him = _s5_prompt(proj, lw["s5"])
    x_new = _finish(x2d, proj, out_a, out_b, y_c, pe, lw, final_norm_g, final)
    ka = k_new.reshape(1, L, KVH_A, HD_A)
    va = v_new.reshape(1, L, KVH_A, HD_A)
    conv_new = proj[L - (CONV_W - 1):, OFF_CONV:OFF_CONV + CONV_CH][None]
    return x_new, (ka, va, conv_new, s_new[None], hre[None], him[None])


def _sample_layer(x2d, pe, lw, final_norm_g, final, layer, cache_k, cache_v, page_table, state_conv, state_delta,
                  state_s5_re, state_s5_im):
    nbatch = x2d.shape[0]
    proj, k_new, v_new = _inproj(x2d, lw["norm_g"], lw["w_pad"], nbatch)
    out_a = _moba_decode(proj, cache_k, cache_v, page_table, layer)
    out_b, conv_new, s_new = _gdn_decode(proj, state_conv, state_delta, lw["conv_w"], lw["a_log"], lw["dt_bias"],
                                         lw["dn_norm_g"], layer)
    y_c, hr, hi = _s5_decode(proj, state_s5_re, state_s5_im, lw["s5"], layer)
    x_new = _finish(x2d, proj, out_a, out_b, y_c, pe, lw, final_norm_g, final)
    ka = k_new.reshape(nbatch, 1, KVH_A, HD_A)
    va = v_new.reshape(nbatch, 1, KVH_A, HD_A)
    return x_new, (ka, va, conv_new, s_new, hr, hi)


def kernel(x_prompt, x_sample, cache_k, cache_v, page_table, state_conv, state_delta, state_s5_re, state_s5_im,
           p_prompt, p_sample, norm_g, w_in, dn_conv_w, dn_a_log, dn_dt_bias, dn_norm_g, s5_lambda_re, s5_lambda_im,
           s5_b_re, s5_b_im, s5_c_re, s5_c_im, s5_d, s5_log_dt, s5_w_glu, s5_b_glu, w_branch, w_out, ple_w,
           ple_norm_g, ple_w_gate, final_norm_g):
    depth = w_in.shape[0]
    bp, L, _ = x_prompt.shape
    nbatch = x_sample.shape[0]
    assert bp == 1 and x_sample.shape[1] == 1
    xp = x_prompt.reshape(L, D_MODEL)
    xs = x_sample.reshape(nbatch, D_MODEL)
    new_p, new_s = [], []
    for i in range(depth):
        lw = _layer_weights(i, norm_g, w_in, dn_conv_w, dn_a_log, dn_dt_bias, dn_norm_g, s5_lambda_re, s5_lambda_im,
                            s5_b_re, s5_b_im, s5_c_re, s5_c_im, s5_d, s5_log_dt, s5_w_glu, s5_b_glu, w_branch,
                            w_out, ple_w, ple_norm_g, ple_w_gate)
        final = i == depth - 1
        xp, st = _prompt_layer(xp, p_prompt[i, 0], lw, final_norm_g, final)
        new_p.append(st)
        xs, st = _sample_layer(xs, p_sample[i, :, 0], lw, final_norm_g, final, i, cache_k, cache_v, page_table,
                               state_conv, state_delta, state_s5_re, state_s5_im)
        new_s.append(st)

    def stk(states, j):
        return jnp.stack([s[j] for s in states])

    return (xp.reshape(1, L, D_MODEL), xs.reshape(nbatch, 1, D_MODEL),
            stk(new_p, 0), stk(new_p, 1), stk(new_p, 2), stk(new_p, 3), stk(new_p, 4), stk(new_p, 5),
            stk(new_s, 0), stk(new_s, 1), stk(new_s, 2), stk(new_s, 3), stk(new_s, 4), stk(new_s, 5))
```
